```python
import math
import jax
import jax.numpy as jnp
from jax import lax
import numpy as np

D_MODEL = 1024
BATCH = 4
SEQ = 8192
DEPTH = 2
DEC_BATCH = 8
DEC_SEQ = 8192
PAST_LEN = 128

N_MIXERS = 2
N_LRU_LAYERS = (DEPTH + 1) // 2
N_RET_LAYERS = DEPTH // 2
EPS = 1e-6

LRU_WIDTH = D_MODEL
LRU_BLOCKS = 4
LRU_BLOCK_W = LRU_WIDTH // LRU_BLOCKS
LRU_CONV_W = 4
LRU_CONV_LEFT = 2
LRU_C = 8.0

RET_HEADS = 4
RET_DK = D_MODEL // RET_HEADS
RET_DV = 2 * RET_DK
RET_QK = RET_HEADS * RET_DK
RET_V = RET_HEADS * RET_DV
RET_CHUNK = 128
ROPE_BASE = 10000.0

D_FF = 2816
FFN_CONV_W = 3

kernel_name = "hybrid_rglru_retention_encoder"


def rms_norm(x, g):
    xf = x.astype(jnp.float32)
    y = xf * lax.rsqrt(jnp.mean(xf * xf, axis=-1, keepdims=True) + EPS)
    return (y * g.astype(jnp.float32)).astype(x.dtype)


def depthwise_conv(x, w, b, pad_left):
    width = w.shape[0]
    S = x.shape[1]
    xp = jnp.pad(x, ((0, 0), (pad_left, width - 1 - pad_left), (0, 0)))
    y = xp[:, 0:S] * w[0]
    for k in range(1, width):
        y = y + xp[:, k:k + S] * w[k]
    return y + b


def linear_scan(a, b, reverse):
    def step(h, ab):
        h = ab[0] * h + ab[1]
        return h, h
    h0 = jnp.zeros((a.shape[0], a.shape[2]), jnp.float32)
    _, hs = lax.scan(step, h0, (jnp.swapaxes(a, 0, 1), jnp.swapaxes(b, 0, 1)), reverse=reverse)
    return jnp.swapaxes(hs, 0, 1)


def rglru_mixer(x, w_in, conv_w, conv_b, w_a, b_a, w_x, b_x, lam, w_out):
    B, S, _ = x.shape
    gate_in, rec = jnp.split(x @ w_in, 2, axis=-1)
    gate = jax.nn.gelu(gate_in, approximate=True)
    xc = depthwise_conv(rec, conv_w, conv_b, LRU_CONV_LEFT)
    xb = xc.reshape(B, S, LRU_BLOCKS, LRU_BLOCK_W)
    r = jax.nn.sigmoid(jnp.einsum('bsnj,dnjk->dbsnk', xb, w_a).reshape(2, B, S, LRU_WIDTH)
                       + b_a[:, None, None, :])
    i = jax.nn.sigmoid(jnp.einsum('bsnj,dnjk->dbsnk', xb, w_x).reshape(2, B, S, LRU_WIDTH)
                       + b_x[:, None, None, :])
    log_a = -LRU_C * r.astype(jnp.float32) * jax.nn.softplus(-lam.astype(jnp.float32))[:, None, None, :]
    a = jnp.exp(log_a)
    u = jnp.sqrt(-jnp.expm1(2.0 * log_a)) * (i * xc[None]).astype(jnp.float32)
    h = linear_scan(a[0], u[0], reverse=False) + linear_scan(a[1], u[1], reverse=True)
    return (h.astype(x.dtype) * gate) @ w_out


def rotary(t, cos, sin):
    t1, t2 = jnp.split(t, 2, axis=-1)
    return jnp.concatenate([t1 * cos - t2 * sin, t2 * cos + t1 * sin], axis=-1)


def retention_one_direction(q, k, v, log_g, include_diag):
    B, S, H, dk = q.shape
    dv = v.shape[-1]
    n_chunks = S // RET_CHUNK

    def to_chunks(t):
        return t.reshape(B, n_chunks, RET_CHUNK, H, t.shape[-1]).transpose(1, 0, 3, 2, 4)

    qc, kc, vc = to_chunks(q), to_chunks(k), to_chunks(v)
    pos = jnp.arange(RET_CHUNK, dtype=jnp.float32)
    diff = pos[:, None] - pos[None, :]
    mask = (diff >= 0) if include_diag else (diff > 0)
    decay = jnp.where(mask[None], jnp.exp(log_g[:, None, None] * jnp.maximum(diff, 0.0)[None]), 0.0)
    xi = jnp.exp(log_g[:, None] * (pos + 1.0)[None])
    zeta = jnp.exp(log_g[:, None] * (RET_CHUNK - 1.0 - pos)[None])
    g_chunk = jnp.exp(log_g * RET_CHUNK)

    def step(state, chunk):
        qb, kb, vb = chunk
        scores = jnp.einsum('bhnk,bhmk->bhnm', qb, kb) * decay[None]
        out = (jnp.einsum('bhnm,bhmv->bhnv', scores, vb)
               + jnp.einsum('bhnk,bhkv->bhnv', qb, state) * xi[None, :, :, None])
        state = (state * g_chunk[None, :, None, None]
                 + jnp.einsum('bhmk,bhmv->bhkv', kb * zeta[None, :, :, None], vb))
        return state, out

    state0 = jnp.zeros((B, H, dk, dv), jnp.float32)
    _, out = lax.scan(step, state0, (qc, kc, vc))
    return out.transpose(1, 0, 3, 2, 4).reshape(B, S, H, dv)


def retention_mixer(x, w_in, decay_logit, norm_g, w_out):
    B, S, _ = x.shape
    q, k, v, g = jnp.split(x @ w_in, [RET_QK, 2 * RET_QK, 2 * RET_QK + RET_V], axis=-1)
    q = q.reshape(B, S, RET_HEADS, RET_DK).astype(jnp.float32)
    k = k.reshape(B, S, RET_HEADS, RET_DK).astype(jnp.float32) * (RET_DK ** -0.5)
    v = v.reshape(B, S, RET_HEADS, RET_DV).astype(jnp.float32)
    half = RET_DK // 2
    theta = ROPE_BASE ** (-jnp.arange(half, dtype=jnp.float32) / half)
    ang = jnp.arange(S, dtype=jnp.float32)[:, None] * theta[None, :]
    cos = jnp.cos(ang)[None, :, None, :]
    sin = jnp.sin(ang)[None, :, None, :]
    q = rotary(q, cos, sin)
    k = rotary(k, cos, sin)
    log_g = jax.nn.log_sigmoid(decay_logit.astype(jnp.float32))
    y_fwd = retention_one_direction(q, k, v, log_g[0], include_diag=True)
    y_bwd = jnp.flip(retention_one_direction(jnp.flip(q, 1), jnp.flip(k, 1), jnp.flip(v, 1),
                                             log_g[1], include_diag=False), 1)
    y = y_fwd + y_bwd
    y = y * lax.rsqrt(jnp.mean(y * y, axis=-1, keepdims=True) + EPS)
    y = (y.reshape(B, S, RET_V) * norm_g.astype(jnp.float32)).astype(x.dtype)
    return (jax.nn.silu(g) * y) @ w_out


def conv_ffn(x, w_in, conv_w, conv_b, w_out):
    u, v = jnp.split(x @ w_in, 2, axis=-1)
    u = depthwise_conv(u, conv_w, conv_b, FFN_CONV_W // 2)
    return (jax.nn.gelu(u, approximate=True) * v) @ w_out


def encoder(x, norm_mix, norm_ffn, norm_final,
            lru_w_in, lru_conv_w, lru_conv_b, lru_w_a, lru_b_a, lru_w_x, lru_b_x, lru_lambda, lru_w_out,
            ret_w_in, ret_decay_logit, ret_norm, ret_w_out,
            ffn_w_in, ffn_conv_w, ffn_conv_b, ffn_w_out):
    for i in range(DEPTH):
        j = i // N_MIXERS
        h = rms_norm(x, norm_mix[i])
        if i % N_MIXERS == 0:
            h = rglru_mixer(h, lru_w_in[j], lru_conv_w[j], lru_conv_b[j], lru_w_a[j], lru_b_a[j],
                            lru_w_x[j], lru_b_x[j], lru_lambda[j], lru_w_out[j])
        else:
            h = retention_mixer(h, ret_w_in[j], ret_decay_logit[j], ret_norm[j], ret_w_out[j])
        x = x + h
        x = x + conv_ffn(rms_norm(x, norm_ffn[i]), ffn_w_in[i], ffn_conv_w[i], ffn_conv_b[i], ffn_w_out[i])
    return rms_norm(x, norm_final)


def setup_inputs(seed: int = 0) -> dict:
    key = jax.random.key(seed)
    ks = jax.random.split(key, 24)
    f32 = jnp.float32

    def nrm(k, shape, scale):
        return jax.random.normal(k, shape, f32) * scale

    a_pow_c = jax.random.uniform(ks[10], (N_LRU_LAYERS, 2, LRU_WIDTH), f32, 0.9, 0.999)
    a_base = jnp.exp(jnp.log(a_pow_c) / LRU_C)
    lru_lambda = jnp.log(a_base) - jnp.log1p(-a_base)
    h_idx = jnp.arange(RET_HEADS, dtype=f32)
    gamma0 = 1.0 - 2.0 ** (-5.0 - h_idx)
    logit0 = jnp.log(gamma0) - jnp.log1p(-gamma0)
    ret_decay_logit = logit0[None, None, :] + nrm(ks[13], (N_RET_LAYERS, 2, RET_HEADS), 0.05)

    return {
        "x_prompt": nrm(ks[0], (BATCH, SEQ, D_MODEL), 1.0),
        "x_sample": nrm(ks[1], (DEC_BATCH, DEC_SEQ, D_MODEL), 1.0),
        "norm_mix": 1.0 + nrm(ks[2], (DEPTH, D_MODEL), 0.02),
        "norm_ffn": 1.0 + nrm(ks[3], (DEPTH, D_MODEL), 0.02),
        "norm_final": 1.0 + nrm(ks[4], (D_MODEL,), 0.02),
        "lru_w_in": nrm(ks[5], (N_LRU_LAYERS, D_MODEL, 2 * LRU_WIDTH), D_MODEL ** -0.5),
        "lru_conv_w": nrm(ks[6], (N_LRU_LAYERS, LRU_CONV_W, LRU_WIDTH), LRU_CONV_W ** -0.5),
        "lru_conv_b": nrm(ks[7], (N_LRU_LAYERS, LRU_WIDTH), 0.01),
        "lru_w_a": nrm(ks[8], (N_LRU_LAYERS, 2, LRU_BLOCKS, LRU_BLOCK_W, LRU_BLOCK_W), LRU_BLOCK_W ** -0.5),
        "lru_b_a": nrm(ks[9], (N_LRU_LAYERS, 2, LRU_WIDTH), 0.01),
        "lru_w_x": nrm(ks[11], (N_LRU_LAYERS, 2, LRU_BLOCKS, LRU_BLOCK_W, LRU_BLOCK_W), LRU_BLOCK_W ** -0.5),
        "lru_b_x": nrm(ks[12], (N_LRU_LAYERS, 2, LRU_WIDTH), 0.01),
        "lru_lambda": lru_lambda,
        "lru_w_out": nrm(ks[14], (N_LRU_LAYERS, LRU_WIDTH, D_MODEL), LRU_WIDTH ** -0.5),
        "ret_w_in": nrm(ks[15], (N_RET_LAYERS, D_MODEL, 2 * RET_QK + 2 * RET_V), D_MODEL ** -0.5),
        "ret_decay_logit": ret_decay_logit,
        "ret_norm": 1.0 + nrm(ks[16], (N_RET_LAYERS, RET_V), 0.02),
        "ret_w_out": nrm(ks[17], (N_RET_LAYERS, RET_V, D_MODEL), RET_V ** -0.5),
        "ffn_w_in": nrm(ks[18], (DEPTH, D_MODEL, 2 * D_FF), D_MODEL ** -0.5),
        "ffn_conv_w": nrm(ks[19], (DEPTH, FFN_CONV_W, D_FF), FFN_CONV_W ** -0.5),
        "ffn_conv_b": nrm(ks[20], (DEPTH, D_FF), 0.01),
        "ffn_w_out": nrm(ks[21], (DEPTH, D_FF, D_MODEL), D_FF ** -0.5),
    }


def reference(x_prompt, x_sample, norm_mix, norm_ffn, norm_final,
              lru_w_in, lru_conv_w, lru_conv_b, lru_w_a, lru_b_a, lru_w_x, lru_b_x, lru_lambda, lru_w_out,
              ret_w_in, ret_decay_logit, ret_norm, ret_w_out,
              ffn_w_in, ffn_conv_w, ffn_conv_b, ffn_w_out):
    y_prompt = encoder(x_prompt, norm_mix, norm_ffn, norm_final,
                       lru_w_in, lru_conv_w, lru_conv_b, lru_w_a, lru_b_a, lru_w_x, lru_b_x, lru_lambda, lru_w_out,
                       ret_w_in, ret_decay_logit, ret_norm, ret_w_out,
                       ffn_w_in, ffn_conv_w, ffn_conv_b, ffn_w_out)
    y_sample = encoder(x_sample, norm_mix, norm_ffn, norm_final,
                       lru_w_in, lru_conv_w, lru_conv_b, lru_w_a, lru_b_a, lru_w_x, lru_b_x, lru_lambda, lru_w_out,
                       ret_w_in, ret_decay_logit, ret_norm, ret_w_out,
                       ffn_w_in, ffn_conv_w, ffn_conv_b, ffn_w_out)
    return (y_prompt, y_sample)
```

```python
import functools

import jax
import jax.numpy as jnp
from jax import lax
from jax.experimental import pallas as pl
from jax.experimental.pallas import tpu as pltpu

F32 = jnp.float32
BF16 = jnp.bfloat16

EPS = 1e-6
D_MODEL = 1024
LRU_BLOCKS = 4
LRU_BLOCK_W = D_MODEL // LRU_BLOCKS
LRU_C = 8.0
RET_HEADS = 4
RET_DK = 256
RET_DV = 512
RET_QK = RET_HEADS * RET_DK
RET_V = RET_HEADS * RET_DV
RET_CHUNK = 128
ROPE_BASE = 10000.0
D_FF = 2816

SUBLANES = 8
HALO = SUBLANES
SEQ_TILE = 512
FF_CHUNK = 256
VMEM_LIMIT_BYTES = 56 * 1024 * 1024


def _params():
    return pltpu.CompilerParams(
        dimension_semantics=("parallel", "arbitrary"),
        vmem_limit_bytes=VMEM_LIMIT_BYTES,
    )


def _const_spec(shape):
    zeros = (0,) * len(shape)
    return pl.BlockSpec(shape, lambda *_: zeros, pipeline_mode=pl.Buffered(1))


def _rms(x, g):
    return x * lax.rsqrt(jnp.mean(x * x, axis=-1, keepdims=True) + EPS) * g


def _gelu(x):
    return jax.nn.gelu(x, approximate=True)


def _dot(a, b):
    return jnp.dot(a, b, preferred_element_type=F32)


def _dot_tn(a, b):
    return lax.dot_general(a, b, (((0,), (0,)), ((), ())), preferred_element_type=F32)


def _dot_nt(a, b):
    return lax.dot_general(a, b, (((1,), (1,)), ((), ())), preferred_element_type=F32)


def _lru_in_kernel(x_ref, g_ref, w_ref, gate_ref, rec_ref):
    w = D_MODEL
    xn = _rms(x_ref[0], g_ref[...]).astype(BF16)
    gate_ref[0] = _gelu(_dot(xn, w_ref[:, :w]))
    rec_ref[0] = _dot(xn, w_ref[:, w:])


def _lru_in(x, g, w_in, ts):
    b, s, d = x.shape
    tile = pl.BlockSpec((1, ts, d), lambda bi, i: (bi, i, 0))
    return pl.pallas_call(
        _lru_in_kernel,
        grid=(b, s // ts),
        in_specs=[tile, _const_spec((1, d)), _const_spec((d, 2 * d))],
        out_specs=[tile, tile],
        out_shape=[jax.ShapeDtypeStruct((b, s, d), F32)] * 2,
        compiler_params=_params(),
        name="lru_in",
    )(x, g, w_in)


def _lru_scan_kernel(rec_ref, prev_ref, next_ref, cw_ref, cb_ref, wa_ref, ba_ref, wx_ref, bx_ref,
                     lam_ref, h_ref, carry_ref, a_ref, u_ref, *, reverse, n_tiles, ts):
    i = pl.program_id(1)
    j = (n_tiles - 1 - i) if reverse else i

    @pl.when(i == 0)
    def _():
        carry_ref[...] = jnp.zeros_like(carry_ref)

    cur = rec_ref[0]
    prev = jnp.where(j == 0, 0.0, prev_ref[0])
    nxt = jnp.where(j == n_tiles - 1, 0.0, next_ref[0])
    ext = jnp.concatenate([prev, cur, nxt], axis=0)
    n_ext = ts + 2 * HALO

    def shifted(k):
        return pltpu.roll(ext, (-k) % n_ext, 0)[HALO:HALO + ts]

    cw = cw_ref[...]
    xc = (shifted(-2) * cw[0:1] + shifted(-1) * cw[1:2] + cur * cw[2:3] + shifted(1) * cw[3:4]
          + cb_ref[...])

    neg_lam = -lam_ref[...]
    softplus = jnp.maximum(neg_lam, 0.0) + jnp.log1p(jnp.exp(-jnp.abs(neg_lam)))
    for nb in range(LRU_BLOCKS):
        sl = slice(nb * LRU_BLOCK_W, (nb + 1) * LRU_BLOCK_W)
        xb = xc[:, sl]
        xb16 = xb.astype(BF16)
        r = jax.nn.sigmoid(_dot(xb16, wa_ref[nb]) + ba_ref[:, sl])
        ig = jax.nn.sigmoid(_dot(xb16, wx_ref[nb]) + bx_ref[:, sl])
        log_a = (-LRU_C * r) * softplus[:, sl]
        a = jnp.exp(log_a)
        one_minus_a2 = -jnp.tanh(log_a) * (a * a + 1.0)
        a_ref[:, sl] = a
        u_ref[:, sl] = jnp.sqrt(one_minus_a2) * (ig * xb)

    def step(t, h):
        row = (ts - 1 - t) if reverse else t
        h = a_ref[pl.ds(row, 1), :] * h + u_ref[pl.ds(row, 1), :]
        h_ref[0, pl.ds(row, 1), :] = h
        return h

    carry_ref[...] = lax.fori_loop(0, ts, step, carry_ref[...], unroll=8)


def _lru_scan(rec, cw, cb, wa, ba, wx, bx, lam, ts, reverse):
    b, s, w = rec.shape
    n_tiles = s // ts
    halo_per_tile = ts // HALO
    n_halo = s // HALO

    def tile_of(i):
        return (n_tiles - 1 - i) if reverse else i

    tile = pl.BlockSpec((1, ts, w), lambda bi, i: (bi, tile_of(i), 0))
    prev = pl.BlockSpec((1, HALO, w),
                        lambda bi, i: (bi, jnp.maximum(tile_of(i) * halo_per_tile - 1, 0), 0))
    nxt = pl.BlockSpec((1, HALO, w),
                       lambda bi, i: (bi, jnp.minimum((tile_of(i) + 1) * halo_per_tile, n_halo - 1), 0))
    row = _const_spec((1, w))
    gate_w = _const_spec((LRU_BLOCKS, LRU_BLOCK_W, LRU_BLOCK_W))
    kern = functools.partial(_lru_scan_kernel, reverse=reverse, n_tiles=n_tiles, ts=ts)
    return pl.pallas_call(
        kern,
        grid=(b, n_tiles),
        in_specs=[tile, prev, nxt, _const_spec(cw.shape), row, gate_w, row, gate_w, row, row],
        out_specs=tile,
        out_shape=jax.ShapeDtypeStruct((b, s, w), F32),
        scratch_shapes=[pltpu.VMEM((1, w), F32), pltpu.VMEM((ts, w), F32), pltpu.VMEM((ts, w), F32)],
        compiler_params=_params(),
        name="lru_scan_bwd" if reverse else "lru_scan_fwd",
    )(rec, rec, rec, cw, cb, wa, ba, wx, bx, lam)


def _lru_out_kernel(hf_ref, hb_ref, gate_ref, x_ref, w_ref, o_ref):
    z = ((hf_ref[0] + hb_ref[0]) * gate_ref[0]).astype(BF16)
    o_ref[0] = x_ref[0] + _dot(z, w_ref[...])


def _lru_out(hf, hb, gate, x, w_out, ts):
    b, s, d = x.shape
    tile = pl.BlockSpec((1, ts, d), lambda bi, i: (bi, i, 0))
    return pl.pallas_call(
        _lru_out_kernel,
        grid=(b, s // ts),
        in_specs=[tile, tile, tile, tile, _const_spec((d, d))],
        out_specs=tile,
        out_shape=jax.ShapeDtypeStruct((b, s, d), F32),
        compiler_params=_params(),
        name="lru_out",
    )(hf, hb, gate, x, w_out)


def _ffn_kernel(*refs, n_tiles, ts, n_chunks, final):
    if final:
        (x_ref, xp_ref, xn_ref, g_ref, wu_ref, wv_ref, cw_ref, cb_ref, wo_ref, gf_ref,
         o_ref, xs_ref, acc_ref) = refs
    else:
        (x_ref, xp_ref, xn_ref, g_ref, wu_ref, wv_ref, cw_ref, cb_ref, wo_ref,
         o_ref, xs_ref, acc_ref) = refs
    j = pl.program_id(1)
    g = g_ref[...]
    x = x_ref[0]
    xs_ref[0:ts] = _rms(x, g).astype(BF16)
    halo = jnp.concatenate([xn_ref[0], xp_ref[0]], axis=0)
    xs_ref[ts:ts + 2 * HALO] = _rms(halo, g).astype(BF16)
    acc_ref[...] = x
    first = j == 0
    last = j == n_tiles - 1
    row = lax.broadcasted_iota(jnp.int32, (ts, 1), 0)

    def chunk(c, carry):
        u = _dot(xs_ref[...], wu_ref[c])
        v = _dot(xs_ref[0:ts], wv_ref[c])
        uc = u[0:ts]
        u_next = jnp.where(last, 0.0, u[ts:ts + 1])
        u_prev = jnp.where(first, 0.0, u[ts + 2 * HALO - 1:ts + 2 * HALO])
        um1 = jnp.where(row == 0, u_prev, pltpu.roll(uc, 1, 0))
        up1 = jnp.where(row == ts - 1, u_next, pltpu.roll(uc, ts - 1, 0))
        cw = cw_ref[c]
        y = um1 * cw[0:1] + uc * cw[1:2] + up1 * cw[2:3] + cb_ref[c]
        act = (_gelu(y) * v).astype(BF16)
        acc_ref[...] += _dot(act, wo_ref[c])
        return carry

    lax.fori_loop(0, n_chunks, chunk, 0)
    out = acc_ref[...]
    if final:
        out = _rms(out, gf_ref[...])
    o_ref[0] = out


def _ffn(x, g, wu, wv, cw, cb, wo, ts, final_g=None):
    b, s, d = x.shape
    n_tiles = s // ts
    halo_per_tile = ts // HALO
    n_halo = s // HALO
    n_chunks, _, fc = wu.shape
    tile = pl.BlockSpec((1, ts, d), lambda bi, i: (bi, i, 0))
    prev = pl.BlockSpec((1, HALO, d), lambda bi, i: (bi, jnp.maximum(i * halo_per_tile - 1, 0), 0))
    nxt = pl.BlockSpec((1, HALO, d),
                       lambda bi, i: (bi, jnp.minimum((i + 1) * halo_per_tile, n_halo - 1), 0))
    final = final_g is not None
    in_specs = [tile, prev, nxt, _const_spec((1, d)), _const_spec(wu.shape), _const_spec(wv.shape),
                _const_spec(cw.shape), _const_spec(cb.shape), _const_spec(wo.shape)]
    args = [x, x, x, g, wu, wv, cw, cb, wo]
    if final:
        in_specs.append(_const_spec((1, d)))
        args.append(final_g)
    kern = functools.partial(_ffn_kernel, n_tiles=n_tiles, ts=ts, n_chunks=n_chunks, final=final)
    return pl.pallas_call(
        kern,
        grid=(b, n_tiles),
        in_specs=in_specs,
        out_specs=tile,
        out_shape=jax.ShapeDtypeStruct((b, s, d), F32),
        scratch_shapes=[pltpu.VMEM((ts + 2 * HALO, d), BF16), pltpu.VMEM((ts, d), F32)],
        compiler_params=_params(),
        name="ffn_final" if final else "ffn",
    )(*args)


def _ret_proj_kernel(x_ref, g_ref, w_ref, cos_ref, sin_ref, q_ref, k_ref, v_ref, gate_ref, xs_ref):
    xs_ref[...] = _rms(x_ref[0], g_ref[...]).astype(BF16)
    cos = cos_ref[...]
    sin = sin_ref[...]
    half = RET_DK // 2
    for h in range(RET_HEADS):
        for base, out_ref, scale in ((0, q_ref, None), (RET_QK, k_ref, RET_DK ** -0.5)):
            lo = base + h * RET_DK
            t = _dot(xs_ref[...], w_ref[:, lo:lo + RET_DK])
            if scale is not None:
                t = t * scale
            t1 = t[:, :half]
            t2 = t[:, half:]
            col = h * RET_DK
            out_ref[0, :, col:col + half] = (t1 * cos - t2 * sin).astype(out_ref.dtype)
            out_ref[0, :, col + half:col + RET_DK] = (t2 * cos + t1 * sin).astype(out_ref.dtype)
    for h in range(RET_HEADS):
        lo = 2 * RET_QK + h * RET_DV
        v_ref[0, :, h * RET_DV:(h + 1) * RET_DV] = _dot(xs_ref[...], w_ref[:, lo:lo + RET_DV]).astype(BF16)
        lo = 2 * RET_QK + RET_V + h * RET_DV
        gate_ref[0, :, h * RET_DV:(h + 1) * RET_DV] = _dot(xs_ref[...], w_ref[:, lo:lo + RET_DV])


def _ret_proj(x, g, w_in, cos, sin, ts):
    b, s, d = x.shape
    tile = lambda width: pl.BlockSpec((1, ts, width), lambda bi, i: (bi, i, 0))
    rope = pl.BlockSpec((ts, RET_DK // 2), lambda bi, i: (i, 0))
    return pl.pallas_call(
        _ret_proj_kernel,
        grid=(b, s // ts),
        in_specs=[tile(d), _const_spec((1, d)), _const_spec(w_in.shape), rope, rope],
        out_specs=[tile(RET_QK), tile(RET_QK), tile(RET_V), tile(RET_V)],
        out_shape=[jax.ShapeDtypeStruct((b, s, RET_QK), BF16), jax.ShapeDtypeStruct((b, s, RET_QK), F32),
                   jax.ShapeDtypeStruct((b, s, RET_V), BF16), jax.ShapeDtypeStruct((b, s, RET_V), F32)],
        scratch_shapes=[pltpu.VMEM((ts, d), BF16)],
        compiler_params=_params(),
        name="ret_proj",
    )(x, g, w_in, cos, sin)


def _chunk_pos():
    return lax.broadcasted_iota(jnp.int32, (RET_CHUNK, 1), 0).astype(F32)


def _ret_bwd_kernel(lg_ref, q_ref, k_ref, v_ref, ob_ref, s_ref, *, ts):
    i = pl.program_id(1)

    @pl.when(i == 0)
    def _():
        s_ref[...] = jnp.zeros_like(s_ref)

    pos = _chunk_pos()
    n_chunks = ts // RET_CHUNK

    def chunk(ci, carry):
        r0 = pl.multiple_of((n_chunks - 1 - ci) * RET_CHUNK, RET_CHUNK)
        rows = pl.ds(r0, RET_CHUNK)
        for h in range(RET_HEADS):
            lg = lg_ref[RET_HEADS + h]
            xi = jnp.exp(lg * (RET_CHUNK - pos))
            zeta = jnp.exp(lg * pos)
            g_chunk = jnp.exp(jnp.full((1, 1), lg * RET_CHUNK, F32))
            qk = slice(h * RET_DK, (h + 1) * RET_DK)
            vv = slice(h * RET_DV, (h + 1) * RET_DV)
            qc = q_ref[0, rows, qk]
            kc = k_ref[0, rows, qk]
            vc = v_ref[0, rows, vv]
            state = s_ref[h]
            ob_ref[0, rows, vv] = _dot(qc, state.astype(BF16)) * xi
            s_ref[h] = state * g_chunk + _dot_tn((kc * zeta).astype(BF16), vc)
        return carry

    lax.fori_loop(0, n_chunks, chunk, 0)


def _ret_bwd(log_g, q, k, v, ts):
    b, s, _ = q.shape
    n_tiles = s // ts
    tile = lambda width: pl.BlockSpec((1, ts, width), lambda bi, i: (bi, n_tiles - 1 - i, 0))
    return pl.pallas_call(
        functools.partial(_ret_bwd_kernel, ts=ts),
        grid=(b, n_tiles),
        in_specs=[pl.BlockSpec(memory_space=pltpu.SMEM), tile(RET_QK), tile(RET_QK), tile(RET_V)],
        out_specs=tile(RET_V),
        out_shape=jax.ShapeDtypeStruct((b, s, RET_V), F32),
        scratch_shapes=[pltpu.VMEM((RET_HEADS, RET_DK, RET_DV), F32)],
        compiler_params=_params(),
        name="ret_bwd",
    )(log_g, q, k, v)


def _ret_fwd_kernel(lg_ref, q_ref, k_ref, v_ref, gate_ref, ob_ref, x_ref, ng_ref, w_ref, o_ref,
                    s_ref, z_ref, *, ts):
    i = pl.program_id(1)

    @pl.when(i == 0)
    def _():
        s_ref[...] = jnp.zeros_like(s_ref)

    pos = _chunk_pos()
    n_idx = lax.broadcasted_iota(jnp.int32, (RET_CHUNK, RET_CHUNK), 0)
    m_idx = lax.broadcasted_iota(jnp.int32, (RET_CHUNK, RET_CHUNK), 1)
    diff = (n_idx - m_idx).astype(F32)
    n_chunks = ts // RET_CHUNK

    def chunk(ci, carry):
        rows = pl.ds(pl.multiple_of(ci * RET_CHUNK, RET_CHUNK), RET_CHUNK)
        for h in range(RET_HEADS):
            lf = lg_ref[h]
            lb = lg_ref[RET_HEADS + h]
            decay = jnp.where(diff >= 0.0, jnp.exp(lf * jnp.maximum(diff, 0.0)),
                              jnp.exp(lb * jnp.maximum(-diff, 0.0)))
            xi = jnp.exp(lf * (pos + 1.0))
            zeta = jnp.exp(lf * (RET_CHUNK - 1.0 - pos))
            g_chunk = jnp.exp(jnp.full((1, 1), lf * RET_CHUNK, F32))
            qk = slice(h * RET_DK, (h + 1) * RET_DK)
            vv = slice(h * RET_DV, (h + 1) * RET_DV)
            qc = q_ref[0, rows, qk]
            kc = k_ref[0, rows, qk]
            vc = v_ref[0, rows, vv]
            state = s_ref[h]
            scores = _dot_nt(qc, kc.astype(BF16)) * decay
            y = (_dot(scores.astype(BF16), vc) + _dot(qc, state.astype(BF16)) * xi) + ob_ref[0, rows, vv]
            s_ref[h] = state * g_chunk + _dot_tn((kc * zeta).astype(BF16), vc)
            y = y * lax.rsqrt(jnp.mean(y * y, axis=-1, keepdims=True) + EPS)
            y = y * ng_ref[:, vv]
            z_ref[rows, vv] = (jax.nn.silu(gate_ref[0, rows, vv]) * y).astype(BF16)
        return carry

    lax.fori_loop(0, n_chunks, chunk, 0)
    o_ref[0] = x_ref[0] + _dot(z_ref[...], w_ref[...])


def _ret_fwd(log_g, q, k, v, gate, ob, x, ng, w_out, ts):
    b, s, d = x.shape
    tile = lambda width: pl.BlockSpec((1, ts, width), lambda bi, i: (bi, i, 0))
    return pl.pallas_call(
        functools.partial(_ret_fwd_kernel, ts=ts),
        grid=(b, s // ts),
        in_specs=[pl.BlockSpec(memory_space=pltpu.SMEM), tile(RET_QK), tile(RET_QK), tile(RET_V),
                  tile(RET_V), tile(RET_V), tile(d), _const_spec((1, RET_V)), _const_spec(w_out.shape)],
        out_specs=tile(d),
        out_shape=jax.ShapeDtypeStruct((b, s, d), F32),
        scratch_shapes=[pltpu.VMEM((RET_HEADS, RET_DK, RET_DV), F32), pltpu.VMEM((ts, RET_V), BF16)],
        compiler_params=_params(),
        name="ret_fwd",
    )(log_g, q, k, v, gate, ob, x, ng, w_out)


def _prepare(norm_mix, norm_ffn, norm_final, lru_w_in, lru_conv_w, lru_conv_b, lru_w_a, lru_b_a,
             lru_w_x, lru_b_x, lru_lambda, lru_w_out, ret_w_in, ret_decay_logit, ret_norm, ret_w_out,
             ffn_w_in, ffn_conv_w, ffn_conv_b, ffn_w_out, seq):
    n_chunks = D_FF // FF_CHUNK

    def ff_cols(w):
        return w.reshape(w.shape[0], n_chunks, FF_CHUNK).transpose(1, 0, 2)

    ffn = []
    for i in range(2):
        w_in = ffn_w_in[i].astype(BF16)
        ffn.append(dict(
            g=norm_ffn[i][None, :],
            wu=ff_cols(w_in[:, :D_FF]),
            wv=ff_cols(w_in[:, D_FF:]),
            cw=ffn_conv_w[i].reshape(3, n_chunks, FF_CHUNK).transpose(1, 0, 2),
            cb=ffn_conv_b[i].reshape(n_chunks, 1, FF_CHUNK),
            wo=ffn_w_out[i].astype(BF16).reshape(n_chunks, FF_CHUNK, D_MODEL),
        ))
    half = RET_DK // 2
    theta = ROPE_BASE ** (-jnp.arange(half, dtype=F32) / half)
    ang = jnp.arange(seq, dtype=F32)[:, None] * theta[None, :]
    return dict(
        ffn=ffn,
        norm_mix=[norm_mix[0][None, :], norm_mix[1][None, :]],
        norm_final=norm_final[None, :],
        lru_w_in=lru_w_in[0].astype(BF16),
        lru_cw=lru_conv_w[0],
        lru_cb=lru_conv_b[0][None, :],
        lru_wa=[lru_w_a[0, d].astype(BF16) for d in range(2)],
        lru_ba=[lru_b_a[0, d][None, :] for d in range(2)],
        lru_wx=[lru_w_x[0, d].astype(BF16) for d in range(2)],
        lru_bx=[lru_b_x[0, d][None, :] for d in range(2)],
        lru_lam=[lru_lambda[0, d][None, :] for d in range(2)],
        lru_w_out=lru_w_out[0].astype(BF16),
        ret_w_in=ret_w_in[0].astype(BF16),
        ret_log_g=jax.nn.log_sigmoid(ret_decay_logit[0].astype(F32)).reshape(2 * RET_HEADS),
        ret_norm=ret_norm[0][None, :],
        ret_w_out=ret_w_out[0].astype(BF16),
        cos=jnp.cos(ang),
        sin=jnp.sin(ang),
    )


def _encoder(x, p, ts):
    gate, rec = _lru_in(x, p["norm_mix"][0], p["lru_w_in"], ts)
    h = [_lru_scan(rec, p["lru_cw"], p["lru_cb"], p["lru_wa"][d], p["lru_ba"][d], p["lru_wx"][d],
                   p["lru_bx"][d], p["lru_lam"][d], ts, reverse=bool(d)) for d in range(2)]
    x = _lru_out(h[0], h[1], gate, x, p["lru_w_out"], ts)
    f = p["ffn"][0]
    x = _ffn(x, f["g"], f["wu"], f["wv"], f["cw"], f["cb"], f["wo"], ts)
    q, k, v, gate = _ret_proj(x, p["norm_mix"][1], p["ret_w_in"], p["cos"], p["sin"], ts)
    ob = _ret_bwd(p["ret_log_g"], q, k, v, ts)
    x = _ret_fwd(p["ret_log_g"], q, k, v, gate, ob, x, p["ret_norm"], p["ret_w_out"], ts)
    f = p["ffn"][1]
    return _ffn(x, f["g"], f["wu"], f["wv"], f["cw"], f["cb"], f["wo"], ts, final_g=p["norm_final"])


def kernel(x_prompt, x_sample, norm_mix, norm_ffn, norm_final, lru_w_in, lru_conv_w, lru_conv_b, lru_w_a, lru_b_a, lru_w_x, lru_b_x, lru_lambda, lru_w_out, ret_w_in, ret_decay_logit, ret_norm, ret_w_out, ffn_w_in, ffn_conv_w, ffn_conv_b, ffn_w_out):
    assert x_prompt.shape[1] == x_sample.shape[1] and x_prompt.shape[1] % SEQ_TILE == 0
    p = _prepare(norm_mix, norm_ffn, norm_final, lru_w_in, lru_conv_w, lru_conv_b, lru_w_a, lru_b_a,
                 lru_w_x, lru_b_x, lru_lambda, lru_w_out, ret_w_in, ret_decay_logit, ret_norm,
                 ret_w_out, ffn_w_in, ffn_conv_w, ffn_conv_b, ffn_w_out, x_prompt.shape[1])
    return (_encoder(x_prompt, p, SEQ_TILE), _encoder(x_sample, p, SEQ_TILE))
```

```python
import functools

import jax
import jax.numpy as jnp
from jax import lax
from jax.experimental import pallas as pl
from jax.experimental.pallas import tpu as pltpu

F32 = jnp.float32
BF16 = jnp.bfloat16

EPS = 1e-6
D_MODEL = 1024
LRU_BLOCKS = 4
LRU_BLOCK_W = D_MODEL // LRU_BLOCKS
LRU_C = 8.0
RET_HEADS = 4
RET_DK = 256
RET_DV = 512
RET_QK = RET_HEADS * RET_DK
RET_V = RET_HEADS * RET_DV
RET_CHUNK = 128
ROPE_BASE = 10000.0
D_FF = 2816

SUBLANES = 8
HALO = SUBLANES
SEQ_TILE = 512
FF_CHUNK = 256
VMEM_LIMIT_BYTES = 56 * 1024 * 1024


def _params():
    return pltpu.CompilerParams(
        dimension_semantics=("parallel", "arbitrary"),
        vmem_limit_bytes=VMEM_LIMIT_BYTES,
    )


def _const_spec(shape):
    zeros = (0,) * len(shape)
    return pl.BlockSpec(shape, lambda *_: zeros, pipeline_mode=pl.Buffered(1))


def _rms(x, g):
    return x * lax.rsqrt(jnp.mean(x * x, axis=-1, keepdims=True) + EPS) * g


def _gelu(x):
    return jax.nn.gelu(x, approximate=True)


def _dot(a, b):
    return jnp.dot(a, b, preferred_element_type=F32)


def _dot_tn(a, b):
    return lax.dot_general(a, b, (((0,), (0,)), ((), ())), preferred_element_type=F32)


def _dot_nt(a, b):
    return lax.dot_general(a, b, (((1,), (1,)), ((), ())), preferred_element_type=F32)


def _lru_in_kernel(x_ref, g_ref, w_ref, gate_ref, rec_ref):
    w = D_MODEL
    xn = _rms(x_ref[0], g_ref[...]).astype(BF16)
    gate_ref[0] = _gelu(_dot(xn, w_ref[:, :w]))
    rec_ref[0] = _dot(xn, w_ref[:, w:])


def _lru_in(x, g, w_in, ts):
    b, s, d = x.shape
    tile = pl.BlockSpec((1, ts, d), lambda bi, i: (bi, i, 0))
    return pl.pallas_call(
        _lru_in_kernel,
        grid=(b, s // ts),
        in_specs=[tile, _const_spec((1, d)), _const_spec((d, 2 * d))],
        out_specs=[tile, tile],
        out_shape=[jax.ShapeDtypeStruct((b, s, d), F32)] * 2,
        compiler_params=_params(),
        name="lru_in",
    )(x, g, w_in)


def _lru_scan_kernel(rec_ref, prev_ref, next_ref, cw_ref, cb_ref, wa_ref, ba_ref, wx_ref, bx_ref,
                     lam_ref, h_ref, carry_ref, a_ref, u_ref, *, reverse, n_tiles, ts):
    i = pl.program_id(1)
    j = (n_tiles - 1 - i) if reverse else i

    @pl.when(i == 0)
    def _():
        carry_ref[...] = jnp.zeros_like(carry_ref)

    cur = rec_ref[0]
    prev = jnp.where(j == 0, 0.0, prev_ref[0])
    nxt = jnp.where(j == n_tiles - 1, 0.0, next_ref[0])
    ext = jnp.concatenate([prev, cur, nxt], axis=0)
    n_ext = ts + 2 * HALO

    def shifted(k):
        return pltpu.roll(ext, (-k) % n_ext, 0)[HALO:HALO + ts]

    cw = cw_ref[...]
    xc = (shifted(-2) * cw[0:1] + shifted(-1) * cw[1:2] + cur * cw[2:3] + shifted(1) * cw[3:4]
          + cb_ref[...])

    neg_lam = -lam_ref[...]
    softplus = jnp.maximum(neg_lam, 0.0) + jnp.log1p(jnp.exp(-jnp.abs(neg_lam)))
    for nb in range(LRU_BLOCKS):
        sl = slice(nb * LRU_BLOCK_W, (nb + 1) * LRU_BLOCK_W)
        xb = xc[:, sl]
        xb16 = xb.astype(BF16)
        r = jax.nn.sigmoid(_dot(xb16, wa_ref[nb]) + ba_ref[:, sl])
        ig = jax.nn.sigmoid(_dot(xb16, wx_ref[nb]) + bx_ref[:, sl])
        log_a = (-LRU_C * r) * softplus[:, sl]
        a = jnp.exp(log_a)
        one_minus_a2 = -jnp.tanh(log_a) * (a * a + 1.0)
        a_ref[:, sl] = a
        u_ref[:, sl] = jnp.sqrt(one_minus_a2) * (ig * xb)

    def step(t, h):
        row = (ts - 1 - t) if reverse else t
        h = a_ref[pl.ds(row, 1), :] * h + u_ref[pl.ds(row, 1), :]
        h_ref[0, pl.ds(row, 1), :] = h
        return h

    carry_ref[...] = lax.fori_loop(0, ts, step, carry_ref[...], unroll=8)


def _lru_scan(rec, cw, cb, wa, ba, wx, bx, lam, ts, reverse):
    b, s, w = rec.shape
    n_tiles = s // ts
    halo_per_tile = ts // HALO
    n_halo = s // HALO

    def tile_of(i):
        return (n_tiles - 1 - i) if reverse else i

    tile = pl.BlockSpec((1, ts, w), lambda bi, i: (bi, tile_of(i), 0))
    prev = pl.BlockSpec((1, HALO, w),
                        lambda bi, i: (bi, jnp.maximum(tile_of(i) * halo_per_tile - 1, 0), 0))
    nxt = pl.BlockSpec((1, HALO, w),
                       lambda bi, i: (bi, jnp.minimum((tile_of(i) + 1) * halo_per_tile, n_halo - 1), 0))
    row = _const_spec((1, w))
    gate_w = _const_spec((LRU_BLOCKS, LRU_BLOCK_W, LRU_BLOCK_W))
    kern = functools.partial(_lru_scan_kernel, reverse=reverse, n_tiles=n_tiles, ts=ts)
    return pl.pallas_call(
        kern,
        grid=(b, n_tiles),
        in_specs=[tile, prev, nxt, _const_spec(cw.shape), row, gate_w, row, gate_w, row, row],
        out_specs=tile,
        out_shape=jax.ShapeDtypeStruct((b, s, w), F32),
        scratch_shapes=[pltpu.VMEM((1, w), F32), pltpu.VMEM((ts, w), F32), pltpu.VMEM((ts, w), F32)],
        compiler_params=_params(),
        name="lru_scan_bwd" if reverse else "lru_scan_fwd",
    )(rec, rec, rec, cw, cb, wa, ba, wx, bx, lam)


def _lru_out_kernel(hf_ref, hb_ref, gate_ref, x_ref, w_ref, o_ref):
    z = ((hf_ref[0] + hb_ref[0]) * gate_ref[0]).astype(BF16)
    o_ref[0] = x_ref[0] + _dot(z, w_ref[...])


def _lru_out(hf, hb, gate, x, w_out, ts):
    b, s, d = x.shape
    tile = pl.BlockSpec((1, ts, d), lambda bi, i: (bi, i, 0))
    return pl.pallas_call(
        _lru_out_kernel,
        grid=(b, s // ts),
        in_specs=[tile, tile, tile, tile, _const_spec((d, d))],
        out_specs=tile,
        out_shape=jax.ShapeDtypeStruct((b, s, d), F32),
        compiler_params=_params(),
        name="lru_out",
    )(hf, hb, gate, x, w_out)


def _ffn_kernel(*refs, n_tiles, ts, final):
    if final:
        (x_ref, xp_ref, xn_ref, g_ref, wu_ref, wv_ref, cw_ref, cb_ref, wo_ref, gf_ref,
         o_ref, xs_ref, act_ref) = refs
    else:
        (x_ref, xp_ref, xn_ref, g_ref, wu_ref, wv_ref, cw_ref, cb_ref, wo_ref,
         o_ref, xs_ref, act_ref) = refs
    j = pl.program_id(1)
    g = g_ref[...]
    x = x_ref[0]
    xs_ref[0:ts] = _rms(x, g).astype(BF16)
    halo = jnp.concatenate([xn_ref[0], xp_ref[0]], axis=0)
    xs_ref[ts:ts + 2 * HALO] = _rms(halo, g).astype(BF16)
    first = j == 0
    last = j == n_tiles - 1
    row = lax.broadcasted_iota(jnp.int32, (ts, 1), 0)

    for c in range(D_FF // FF_CHUNK):
        cols = slice(c * FF_CHUNK, (c + 1) * FF_CHUNK)
        u = _dot(xs_ref[...], wu_ref[:, cols])
        v = _dot(xs_ref[0:ts], wv_ref[:, cols])
        uc = u[0:ts]
        u_next = jnp.where(last, 0.0, u[ts:ts + 1])
        u_prev = jnp.where(first, 0.0, u[ts + 2 * HALO - 1:ts + 2 * HALO])
        um1 = jnp.where(row == 0, u_prev, pltpu.roll(uc, 1, 0))
        up1 = jnp.where(row == ts - 1, u_next, pltpu.roll(uc, ts - 1, 0))
        y = um1 * cw_ref[0:1, cols] + uc * cw_ref[1:2, cols] + up1 * cw_ref[2:3, cols] + cb_ref[:, cols]
        act_ref[:, cols] = (_gelu(y) * v).astype(BF16)

    out = x + _dot(act_ref[...], wo_ref[...])
    if final:
        out = _rms(out, gf_ref[...])
    o_ref[0] = out


def _ffn(x, g, wu, wv, cw, cb, wo, ts, final_g=None):
    b, s, d = x.shape
    n_tiles = s // ts
    halo_per_tile = ts // HALO
    n_halo = s // HALO
    tile = pl.BlockSpec((1, ts, d), lambda bi, i: (bi, i, 0))
    prev = pl.BlockSpec((1, HALO, d), lambda bi, i: (bi, jnp.maximum(i * halo_per_tile - 1, 0), 0))
    nxt = pl.BlockSpec((1, HALO, d),
                       lambda bi, i: (bi, jnp.minimum((i + 1) * halo_per_tile, n_halo - 1), 0))
    final = final_g is not None
    in_specs = [tile, prev, nxt, _const_spec((1, d)), _const_spec(wu.shape), _const_spec(wv.shape),
                _const_spec(cw.shape), _const_spec(cb.shape), _const_spec(wo.shape)]
    args = [x, x, x, g, wu, wv, cw, cb, wo]
    if final:
        in_specs.append(_const_spec((1, d)))
        args.append(final_g)
    kern = functools.partial(_ffn_kernel, n_tiles=n_tiles, ts=ts, final=final)
    return pl.pallas_call(
        kern,
        grid=(b, n_tiles),
        in_specs=in_specs,
        out_specs=tile,
        out_shape=jax.ShapeDtypeStruct((b, s, d), F32),
        scratch_shapes=[pltpu.VMEM((ts + 2 * HALO, d), BF16), pltpu.VMEM((ts, D_FF), BF16)],
        compiler_params=_params(),
        name="ffn_final" if final else "ffn",
    )(*args)


def _ret_proj_kernel(x_ref, g_ref, w_ref, cos_ref, sin_ref, q_ref, k_ref, v_ref, gate_ref, xs_ref):
    xs_ref[...] = _rms(x_ref[0], g_ref[...]).astype(BF16)
    cos = cos_ref[...]
    sin = sin_ref[...]
    half = RET_DK // 2
    for h in range(RET_HEADS):
        for base, out_ref, scale in ((0, q_ref, None), (RET_QK, k_ref, RET_DK ** -0.5)):
            lo = base + h * RET_DK
            t = _dot(xs_ref[...], w_ref[:, lo:lo + RET_DK])
            if scale is not None:
                t = t * scale
            t1 = t[:, :half]
            t2 = t[:, half:]
            col = h * RET_DK
            out_ref[0, :, col:col + half] = (t1 * cos - t2 * sin).astype(out_ref.dtype)
            out_ref[0, :, col + half:col + RET_DK] = (t2 * cos + t1 * sin).astype(out_ref.dtype)
    for h in range(RET_HEADS):
        lo = 2 * RET_QK + h * RET_DV
        v_ref[0, :, h * RET_DV:(h + 1) * RET_DV] = _dot(xs_ref[...], w_ref[:, lo:lo + RET_DV]).astype(BF16)
        lo = 2 * RET_QK + RET_V + h * RET_DV
        gate_ref[0, :, h * RET_DV:(h + 1) * RET_DV] = _dot(xs_ref[...], w_ref[:, lo:lo + RET_DV])


def _ret_proj(x, g, w_in, cos, sin, ts):
    b, s, d = x.shape
    tile = lambda width: pl.BlockSpec((1, ts, width), lambda bi, i: (bi, i, 0))
    rope = pl.BlockSpec((ts, RET_DK // 2), lambda bi, i: (i, 0))
    return pl.pallas_call(
        _ret_proj_kernel,
        grid=(b, s // ts),
        in_specs=[tile(d), _const_spec((1, d)), _const_spec(w_in.shape), rope, rope],
        out_specs=[tile(RET_QK), tile(RET_QK), tile(RET_V), tile(RET_V)],
        out_shape=[jax.ShapeDtypeStruct((b, s, RET_QK), BF16), jax.ShapeDtypeStruct((b, s, RET_QK), F32),
                   jax.ShapeDtypeStruct((b, s, RET_V), BF16), jax.ShapeDtypeStruct((b, s, RET_V), F32)],
        scratch_shapes=[pltpu.VMEM((ts, d), BF16)],
        compiler_params=_params(),
        name="ret_proj",
    )(x, g, w_in, cos, sin)


def _chunk_pos():
    return lax.broadcasted_iota(jnp.int32, (RET_CHUNK, 1), 0).astype(F32)


def _ret_bwd_kernel(lg_ref, q_ref, k_ref, v_ref, ob_ref, s_ref, *, ts):
    i = pl.program_id(1)

    @pl.when(i == 0)
    def _():
        s_ref[...] = jnp.zeros_like(s_ref)

    pos = _chunk_pos()
    n_chunks = ts // RET_CHUNK

    def chunk(ci, carry):
        r0 = pl.multiple_of((n_chunks - 1 - ci) * RET_CHUNK, RET_CHUNK)
        rows = pl.ds(r0, RET_CHUNK)
        for h in range(RET_HEADS):
            lg = lg_ref[RET_HEADS + h]
            xi = jnp.exp(lg * (RET_CHUNK - pos))
            zeta = jnp.exp(lg * pos)
            g_chunk = jnp.exp(jnp.full((1, 1), lg * RET_CHUNK, F32))
            qk = slice(h * RET_DK, (h + 1) * RET_DK)
            vv = slice(h * RET_DV, (h + 1) * RET_DV)
            qc = q_ref[0, rows, qk]
            kc = k_ref[0, rows, qk]
            vc = v_ref[0, rows, vv]
            state = s_ref[h]
            ob_ref[0, rows, vv] = _dot(qc, state.astype(BF16)) * xi
            s_ref[h] = state * g_chunk + _dot_tn((kc * zeta).astype(BF16), vc)
        return carry

    lax.fori_loop(0, n_chunks, chunk, 0)


def _ret_bwd(log_g, q, k, v, ts):
    b, s, _ = q.shape
    n_tiles = s // ts
    tile = lambda width: pl.BlockSpec((1, ts, width), lambda bi, i: (bi, n_tiles - 1 - i, 0))
    return pl.pallas_call(
        functools.partial(_ret_bwd_kernel, ts=ts),
        grid=(b, n_tiles),
        in_specs=[pl.BlockSpec(memory_space=pltpu.SMEM), tile(RET_QK), tile(RET_QK), tile(RET_V)],
        out_specs=tile(RET_V),
        out_shape=jax.ShapeDtypeStruct((b, s, RET_V), F32),
        scratch_shapes=[pltpu.VMEM((RET_HEADS, RET_DK, RET_DV), F32)],
        compiler_params=_params(),
        name="ret_bwd",
    )(log_g, q, k, v)


def _ret_fwd_kernel(lg_ref, q_ref, k_ref, v_ref, gate_ref, ob_ref, x_ref, ng_ref, w_ref, o_ref,
                    s_ref, z_ref, *, ts):
    i = pl.program_id(1)

    @pl.when(i == 0)
    def _():
        s_ref[...] = jnp.zeros_like(s_ref)

    pos = _chunk_pos()
    n_idx = lax.broadcasted_iota(jnp.int32, (RET_CHUNK, RET_CHUNK), 0)
    m_idx = lax.broadcasted_iota(jnp.int32, (RET_CHUNK, RET_CHUNK), 1)
    diff = (n_idx - m_idx).astype(F32)
    n_chunks = ts // RET_CHUNK

    def chunk(ci, carry):
        rows = pl.ds(pl.multiple_of(ci * RET_CHUNK, RET_CHUNK), RET_CHUNK)
        for h in range(RET_HEADS):
            lf = lg_ref[h]
            lb = lg_ref[RET_HEADS + h]
            decay = jnp.where(diff >= 0.0, jnp.exp(lf * jnp.maximum(diff, 0.0)),
                              jnp.exp(lb * jnp.maximum(-diff, 0.0)))
            xi = jnp.exp(lf * (pos + 1.0))
            zeta = jnp.exp(lf * (RET_CHUNK - 1.0 - pos))
            g_chunk = jnp.exp(jnp.full((1, 1), lf * RET_CHUNK, F32))
            qk = slice(h * RET_DK, (h + 1) * RET_DK)
            vv = slice(h * RET_DV, (h + 1) * RET_DV)
            qc = q_ref[0, rows, qk]
            kc = k_ref[0, rows, qk]
            vc = v_ref[0, rows, vv]
            state = s_ref[h]
            scores = _dot_nt(qc, kc.astype(BF16)) * decay
            y = (_dot(scores.astype(BF16), vc) + _dot(qc, state.astype(BF16)) * xi) + ob_ref[0, rows, vv]
            s_ref[h] = state * g_chunk + _dot_tn((kc * zeta).astype(BF16), vc)
            y = y * lax.rsqrt(jnp.mean(y * y, axis=-1, keepdims=True) + EPS)
            y = y * ng_ref[:, vv]
            z_ref[rows, vv] = (jax.nn.silu(gate_ref[0, rows, vv]) * y).astype(BF16)
        return carry

    lax.fori_loop(0, n_chunks, chunk, 0)
    o_ref[0] = x_ref[0] + _dot(z_ref[...], w_ref[...])


def _ret_fwd(log_g, q, k, v, gate, ob, x, ng, w_out, ts):
    b, s, d = x.shape
    tile = lambda width: pl.BlockSpec((1, ts, width), lambda bi, i: (bi, i, 0))
    return pl.pallas_call(
        functools.partial(_ret_fwd_kernel, ts=ts),
        grid=(b, s // ts),
        in_specs=[pl.BlockSpec(memory_space=pltpu.SMEM), tile(RET_QK), tile(RET_QK), tile(RET_V),
                  tile(RET_V), tile(RET_V), tile(d), _const_spec((1, RET_V)), _const_spec(w_out.shape)],
        out_specs=tile(d),
        out_shape=jax.ShapeDtypeStruct((b, s, d), F32),
        scratch_shapes=[pltpu.VMEM((RET_HEADS, RET_DK, RET_DV), F32), pltpu.VMEM((ts, RET_V), BF16)],
        compiler_params=_params(),
        name="ret_fwd",
    )(log_g, q, k, v, gate, ob, x, ng, w_out)


def _prepare(norm_mix, norm_ffn, norm_final, lru_w_in, lru_conv_w, lru_conv_b, lru_w_a, lru_b_a,
             lru_w_x, lru_b_x, lru_lambda, lru_w_out, ret_w_in, ret_decay_logit, ret_norm, ret_w_out,
             ffn_w_in, ffn_conv_w, ffn_conv_b, ffn_w_out, seq):
    ffn = []
    for i in range(2):
        w_in = ffn_w_in[i].astype(BF16)
        ffn.append(dict(
            g=norm_ffn[i][None, :],
            wu=w_in[:, :D_FF],
            wv=w_in[:, D_FF:],
            cw=ffn_conv_w[i],
            cb=ffn_conv_b[i][None, :],
            wo=ffn_w_out[i].astype(BF16),
        ))
    half = RET_DK // 2
    theta = ROPE_BASE ** (-jnp.arange(half, dtype=F32) / half)
    ang = jnp.arange(seq, dtype=F32)[:, None] * theta[None, :]
    return dict(
        ffn=ffn,
        norm_mix=[norm_mix[0][None, :], norm_mix[1][None, :]],
        norm_final=norm_final[None, :],
        lru_w_in=lru_w_in[0].astype(BF16),
        lru_cw=lru_conv_w[0],
        lru_cb=lru_conv_b[0][None, :],
        lru_wa=[lru_w_a[0, d].astype(BF16) for d in range(2)],
        lru_ba=[lru_b_a[0, d][None, :] for d in range(2)],
        lru_wx=[lru_w_x[0, d].astype(BF16) for d in range(2)],
        lru_bx=[lru_b_x[0, d][None, :] for d in range(2)],
        lru_lam=[lru_lambda[0, d][None, :] for d in range(2)],
        lru_w_out=lru_w_out[0].astype(BF16),
        ret_w_in=ret_w_in[0].astype(BF16),
        ret_log_g=jax.nn.log_sigmoid(ret_decay_logit[0].astype(F32)).reshape(2 * RET_HEADS),
        ret_norm=ret_norm[0][None, :],
        ret_w_out=ret_w_out[0].astype(BF16),
        cos=jnp.cos(ang),
        sin=jnp.sin(ang),
    )


def _encoder(x, p, ts):
    gate, rec = _lru_in(x, p["norm_mix"][0], p["lru_w_in"], ts)
    h = [_lru_scan(rec, p["lru_cw"], p["lru_cb"], p["lru_wa"][d], p["lru_ba"][d], p["lru_wx"][d],
                   p["lru_bx"][d], p["lru_lam"][d], ts, reverse=bool(d)) for d in range(2)]
    x = _lru_out(h[0], h[1], gate, x, p["lru_w_out"], ts)
    f = p["ffn"][0]
    x = _ffn(x, f["g"], f["wu"], f["wv"], f["cw"], f["cb"], f["wo"], ts)
    q, k, v, gate = _ret_proj(x, p["norm_mix"][1], p["ret_w_in"], p["cos"], p["sin"], ts)
    ob = _ret_bwd(p["ret_log_g"], q, k, v, ts)
    x = _ret_fwd(p["ret_log_g"], q, k, v, gate, ob, x, p["ret_norm"], p["ret_w_out"], ts)
    f = p["ffn"][1]
    return _ffn(x, f["g"], f["wu"], f["wv"], f["cw"], f["cb"], f["wo"], ts, final_g=p["norm_final"])


def kernel(x_prompt, x_sample, norm_mix, norm_ffn, norm_final, lru_w_in, lru_conv_w, lru_conv_b, lru_w_a, lru_b_a, lru_w_x, lru_b_x, lru_lambda, lru_w_out, ret_w_in, ret_decay_logit, ret_norm, ret_w_out, ffn_w_in, ffn_conv_w, ffn_conv_b, ffn_w_out):
    assert x_prompt.shape[1] == x_sample.shape[1] and x_prompt.shape[1] % SEQ_TILE == 0
    p = _prepare(norm_mix, norm_ffn, norm_final, lru_w_in, lru_conv_w, lru_conv_b, lru_w_a, lru_b_a,
                 lru_w_x, lru_b_x, lru_lambda, lru_w_out, ret_w_in, ret_decay_logit, ret_norm,
                 ret_w_out, ffn_w_in, ffn_conv_w, ffn_conv_b, ffn_w_out, x_prompt.shape[1])
    return (_encoder(x_prompt, p, SEQ_TILE), _encoder(x_sample, p, SEQ_TILE))
```

```python
import functools

import jax
import jax.numpy as jnp
from jax import lax
from jax.experimental import pallas as pl
from jax.experimental.pallas import tpu as pltpu

F32 = jnp.float32
BF16 = jnp.bfloat16

EPS = 1e-6
D_MODEL = 1024
LRU_BLOCKS = 4
LRU_BLOCK_W = D_MODEL // LRU_BLOCKS
LRU_C = 8.0
RET_HEADS = 4
RET_DK = 256
RET_DV = 512
RET_QK = RET_HEADS * RET_DK
RET_V = RET_HEADS * RET_DV
RET_CHUNK = 128
ROPE_BASE = 10000.0
D_FF = 2816

SUBLANES = 8
HALO = SUBLANES
SEQ_TILE = 512
FF_CHUNK = 256
VMEM_LIMIT_BYTES = 56 * 1024 * 1024


def _params(grid_rank):
    return pltpu.CompilerParams(
        dimension_semantics=("parallel",) * (grid_rank - 1) + ("arbitrary",),
        vmem_limit_bytes=VMEM_LIMIT_BYTES,
    )


def _const_spec(shape):
    zeros = (0,) * len(shape)
    return pl.BlockSpec(shape, lambda *_: zeros, pipeline_mode=pl.Buffered(1))


def _rms(x, g):
    return x * lax.rsqrt(jnp.mean(x * x, axis=-1, keepdims=True) + EPS) * g


def _gelu(x):
    return jax.nn.gelu(x, approximate=True)


def _dot(a, b):
    return jnp.dot(a, b, preferred_element_type=F32)


def _dot_tn(a, b):
    return lax.dot_general(a, b, (((0,), (0,)), ((), ())), preferred_element_type=F32)


def _dot_nt(a, b):
    return lax.dot_general(a, b, (((1,), (1,)), ((), ())), preferred_element_type=F32)


def _tile_and_halo_specs(s, ts, d):
    halo_per_tile = ts // HALO
    n_halo = s // HALO
    tile = pl.BlockSpec((1, ts, d), lambda bi, i: (bi, i, 0))
    prev = pl.BlockSpec((1, HALO, d), lambda bi, i: (bi, jnp.maximum(i * halo_per_tile - 1, 0), 0))
    nxt = pl.BlockSpec((1, HALO, d),
                       lambda bi, i: (bi, jnp.minimum((i + 1) * halo_per_tile, n_halo - 1), 0))
    return tile, prev, nxt


def _store_normed_tile_with_halo(x, xp_ref, xn_ref, g, xs_ref, n_tiles, ts):
    j = pl.program_id(1)
    xs_ref[0:ts] = _rms(x, g).astype(BF16)
    nxt = jnp.where(j == n_tiles - 1, 0.0, _rms(xn_ref[0], g))
    prv = jnp.where(j == 0, 0.0, _rms(xp_ref[0], g))
    xs_ref[ts:ts + 2 * HALO] = jnp.concatenate([nxt, prv], axis=0).astype(BF16)


def _time_shift(ext, k, ts):
    return pltpu.roll(ext, (-k) % ext.shape[0], 0)[0:ts]


def _lru_in_kernel(x_ref, xp_ref, xn_ref, g_ref, w_ref, cw_ref, cb_ref, gate_ref, xc_ref, xs_ref,
                   *, n_tiles, ts):
    w = D_MODEL
    _store_normed_tile_with_halo(x_ref[0], xp_ref, xn_ref, g_ref[...], xs_ref, n_tiles, ts)
    gate_ref[0] = _gelu(_dot(xs_ref[0:ts], w_ref[:, :w]))
    for nb in range(LRU_BLOCKS):
        sl = slice(nb * LRU_BLOCK_W, (nb + 1) * LRU_BLOCK_W)
        rec = _dot(xs_ref[...], w_ref[:, w + nb * LRU_BLOCK_W:w + (nb + 1) * LRU_BLOCK_W])
        xc_ref[:, sl] = (_time_shift(rec, -2, ts) * cw_ref[0:1, sl] + _time_shift(rec, -1, ts) * cw_ref[1:2, sl]
                         + rec[0:ts] * cw_ref[2:3, sl] + _time_shift(rec, 1, ts) * cw_ref[3:4, sl]
                         + cb_ref[:, sl])


def _lru_in(x, g, w_in, cw, cb, ts):
    b, s, d = x.shape
    n_tiles = s // ts
    tile, prev, nxt = _tile_and_halo_specs(s, ts, d)
    token_major = pl.BlockSpec((ts, d), lambda bi, i: (i, bi))
    return pl.pallas_call(
        functools.partial(_lru_in_kernel, n_tiles=n_tiles, ts=ts),
        grid=(b, n_tiles),
        in_specs=[tile, prev, nxt, _const_spec((1, d)), _const_spec((d, 2 * d)), _const_spec(cw.shape),
                  _const_spec((1, d))],
        out_specs=[tile, token_major],
        out_shape=[jax.ShapeDtypeStruct((b, s, d), F32), jax.ShapeDtypeStruct((s, b * d), F32)],
        scratch_shapes=[pltpu.VMEM((ts + 2 * HALO, d), BF16)],
        compiler_params=_params(2),
        name="lru_in",
    )(x, x, x, g, w_in, cw, cb)


def _lru_scan_kernel(xc_ref, wa_ref, ba_ref, wx_ref, bx_ref, lam_ref, h_ref, carry_ref, a_ref, u_ref,
                     *, reverse, rows, batch):
    @pl.when(pl.program_id(0) == 0)
    def _():
        carry_ref[...] = jnp.zeros_like(carry_ref)

    neg_lam = -lam_ref[...]
    softplus = jnp.maximum(neg_lam, 0.0) + jnp.log1p(jnp.exp(-jnp.abs(neg_lam)))
    scale = (-0.5 * LRU_C) * softplus
    for nb in range(LRU_BLOCKS):
        sl = slice(nb * LRU_BLOCK_W, (nb + 1) * LRU_BLOCK_W)
        xb = xc_ref[:, sl]
        xb16 = xb.astype(BF16)
        tr = jnp.tanh(0.5 * (_dot(xb16, wa_ref[nb]) + ba_ref[:, sl]))
        ti = jnp.tanh(0.5 * (_dot(xb16, wx_ref[nb]) + bx_ref[:, sl]))
        log_a = scale[:, sl] * (tr + 1.0)
        a = jnp.exp(log_a)
        one_minus_a2 = -jnp.tanh(log_a) * (a * a + 1.0)
        a_ref[:, sl] = a
        u_ref[:, sl] = jnp.sqrt(one_minus_a2) * ((0.5 * xb) * (ti + 1.0))

    groups = rows // SUBLANES
    substeps = SUBLANES // batch
    shift = (SUBLANES - batch) if reverse else batch % SUBLANES
    sub = lax.broadcasted_iota(jnp.int32, (SUBLANES, D_MODEL), 0)

    def step(gi, c):
        g = (groups - 1 - gi) if reverse else gi
        r = pl.ds(pl.multiple_of(g * SUBLANES, SUBLANES), SUBLANES)
        a8 = a_ref[r, :]
        u8 = u_ref[r, :]
        h = a8 * c + u8
        out = h
        for k in range(1, substeps):
            h = a8 * pltpu.roll(h, shift, 0) + u8
            if reverse:
                out = jnp.where(sub < (substeps - k) * batch, h, out)
            else:
                out = jnp.where(sub >= k * batch, h, out)
        h_ref[r, :] = out
        return pltpu.roll(h, shift, 0) if substeps > 1 else h

    carry_ref[...] = lax.fori_loop(0, groups, step, carry_ref[...], unroll=4)


def _lru_scan(xc, wa, ba, wx, bx, lam, batch, rows, reverse):
    n, w = xc.shape
    n_tiles = n // rows
    tile = pl.BlockSpec((rows, w), lambda i: ((n_tiles - 1 - i) if reverse else i, 0))
    row = _const_spec((1, w))
    gate_w = _const_spec((LRU_BLOCKS, LRU_BLOCK_W, LRU_BLOCK_W))
    kern = functools.partial(_lru_scan_kernel, reverse=reverse, rows=rows, batch=batch)
    return pl.pallas_call(
        kern,
        grid=(n_tiles,),
        in_specs=[tile, gate_w, row, gate_w, row, row],
        out_specs=tile,
        out_shape=jax.ShapeDtypeStruct((n, w), F32),
        scratch_shapes=[pltpu.VMEM((SUBLANES, w), F32), pltpu.VMEM((rows, w), F32),
                        pltpu.VMEM((rows, w), F32)],
        compiler_params=_params(1),
        name="lru_scan_bwd" if reverse else "lru_scan_fwd",
    )(xc, wa, ba, wx, bx, lam)


def _lru_out_kernel(hf_ref, hb_ref, gate_ref, x_ref, w_ref, o_ref):
    z = ((hf_ref[...] + hb_ref[...]) * gate_ref[0]).astype(BF16)
    o_ref[0] = x_ref[0] + _dot(z, w_ref[...])


def _lru_out(hf, hb, gate, x, w_out, ts):
    b, s, d = x.shape
    tile = pl.BlockSpec((1, ts, d), lambda bi, i: (bi, i, 0))
    token_major = pl.BlockSpec((ts, d), lambda bi, i: (i, bi))
    return pl.pallas_call(
        _lru_out_kernel,
        grid=(b, s // ts),
        in_specs=[token_major, token_major, tile, tile, _const_spec((d, d))],
        out_specs=tile,
        out_shape=jax.ShapeDtypeStruct((b, s, d), F32),
        compiler_params=_params(2),
        name="lru_out",
    )(hf, hb, gate, x, w_out)


def _ffn_kernel(*refs, n_tiles, ts, final):
    if final:
        (x_ref, xp_ref, xn_ref, g_ref, wu_ref, wv_ref, cw_ref, cb_ref, wo_ref, gf_ref,
         o_ref, xs_ref, act_ref) = refs
    else:
        (x_ref, xp_ref, xn_ref, g_ref, wu_ref, wv_ref, cw_ref, cb_ref, wo_ref,
         o_ref, xs_ref, act_ref) = refs
    x = x_ref[0]
    _store_normed_tile_with_halo(x, xp_ref, xn_ref, g_ref[...], xs_ref, n_tiles, ts)
    for c in range(D_FF // FF_CHUNK):
        cols = slice(c * FF_CHUNK, (c + 1) * FF_CHUNK)
        u = _dot(xs_ref[...], wu_ref[:, cols])
        v = _dot(xs_ref[0:ts], wv_ref[:, cols])
        y = (_time_shift(u, -1, ts) * cw_ref[0:1, cols] + u[0:ts] * cw_ref[1:2, cols]
             + _time_shift(u, 1, ts) * cw_ref[2:3, cols] + cb_ref[:, cols])
        act_ref[:, cols] = (_gelu(y) * v).astype(BF16)

    out = x + _dot(act_ref[...], wo_ref[...])
    if final:
        out = _rms(out, gf_ref[...])
    o_ref[0] = out


def _ffn(x, g, wu, wv, cw, cb, wo, ts, final_g=None):
    b, s, d = x.shape
    n_tiles = s // ts
    tile, prev, nxt = _tile_and_halo_specs(s, ts, d)
    final = final_g is not None
    in_specs = [tile, prev, nxt, _const_spec((1, d)), _const_spec(wu.shape), _const_spec(wv.shape),
                _const_spec(cw.shape), _const_spec(cb.shape), _const_spec(wo.shape)]
    args = [x, x, x, g, wu, wv, cw, cb, wo]
    if final:
        in_specs.append(_const_spec((1, d)))
        args.append(final_g)
    kern = functools.partial(_ffn_kernel, n_tiles=n_tiles, ts=ts, final=final)
    return pl.pallas_call(
        kern,
        grid=(b, n_tiles),
        in_specs=in_specs,
        out_specs=tile,
        out_shape=jax.ShapeDtypeStruct((b, s, d), F32),
        scratch_shapes=[pltpu.VMEM((ts + 2 * HALO, d), BF16), pltpu.VMEM((ts, D_FF), BF16)],
        compiler_params=_params(2),
        name="ffn_final" if final else "ffn",
    )(*args)


def _ret_proj_kernel(x_ref, g_ref, w_ref, cos_ref, sin_ref, q_ref, k_ref, v_ref, gate_ref, xs_ref):
    xs_ref[...] = _rms(x_ref[0], g_ref[...]).astype(BF16)
    cos = cos_ref[...]
    sin = sin_ref[...]
    half = RET_DK // 2
    for h in range(RET_HEADS):
        for base, out_ref, scale in ((0, q_ref, None), (RET_QK, k_ref, RET_DK ** -0.5)):
            lo = base + h * RET_DK
            t = _dot(xs_ref[...], w_ref[:, lo:lo + RET_DK])
            if scale is not None:
                t = t * scale
            t1 = t[:, :half]
            t2 = t[:, half:]
            col = h * RET_DK
            out_ref[0, :, col:col + half] = (t1 * cos - t2 * sin).astype(out_ref.dtype)
            out_ref[0, :, col + half:col + RET_DK] = (t2 * cos + t1 * sin).astype(out_ref.dtype)
    for h in range(RET_HEADS):
        lo = 2 * RET_QK + h * RET_DV
        v_ref[0, :, h * RET_DV:(h + 1) * RET_DV] = _dot(xs_ref[...], w_ref[:, lo:lo + RET_DV]).astype(BF16)
        lo = 2 * RET_QK + RET_V + h * RET_DV
        gate_ref[0, :, h * RET_DV:(h + 1) * RET_DV] = _dot(xs_ref[...], w_ref[:, lo:lo + RET_DV])


def _ret_proj(x, g, w_in, cos, sin, ts):
    b, s, d = x.shape
    tile = lambda width: pl.BlockSpec((1, ts, width), lambda bi, i: (bi, i, 0))
    rope = pl.BlockSpec((ts, RET_DK // 2), lambda bi, i: (i, 0))
    return pl.pallas_call(
        _ret_proj_kernel,
        grid=(b, s // ts),
        in_specs=[tile(d), _const_spec((1, d)), _const_spec(w_in.shape), rope, rope],
        out_specs=[tile(RET_QK), tile(RET_QK), tile(RET_V), tile(RET_V)],
        out_shape=[jax.ShapeDtypeStruct((b, s, RET_QK), BF16), jax.ShapeDtypeStruct((b, s, RET_QK), F32),
                   jax.ShapeDtypeStruct((b, s, RET_V), BF16), jax.ShapeDtypeStruct((b, s, RET_V), F32)],
        scratch_shapes=[pltpu.VMEM((ts, d), BF16)],
        compiler_params=_params(2),
        name="ret_proj",
    )(x, g, w_in, cos, sin)


def _chunk_pos():
    return lax.broadcasted_iota(jnp.int32, (RET_CHUNK, 1), 0).astype(F32)


def _ret_bwd_kernel(lg_ref, q_ref, k_ref, v_ref, ob_ref, s_ref, *, ts):
    i = pl.program_id(1)

    @pl.when(i == 0)
    def _():
        s_ref[...] = jnp.zeros_like(s_ref)

    pos = _chunk_pos()
    n_chunks = ts // RET_CHUNK

    def chunk(ci, carry):
        r0 = pl.multiple_of((n_chunks - 1 - ci) * RET_CHUNK, RET_CHUNK)
        rows = pl.ds(r0, RET_CHUNK)
        for h in range(RET_HEADS):
            lg = lg_ref[RET_HEADS + h]
            xi = jnp.exp(lg * (RET_CHUNK - pos))
            zeta = jnp.exp(lg * pos)
            g_chunk = jnp.exp(jnp.full((1, 1), lg * RET_CHUNK, F32))
            qk = slice(h * RET_DK, (h + 1) * RET_DK)
            vv = slice(h * RET_DV, (h + 1) * RET_DV)
            qc = q_ref[0, rows, qk]
            kc = k_ref[0, rows, qk]
            vc = v_ref[0, rows, vv]
            state = s_ref[h]
            ob_ref[0, rows, vv] = _dot(qc, state.astype(BF16)) * xi
            s_ref[h] = state * g_chunk + _dot_tn((kc * zeta).astype(BF16), vc)
        return carry

    lax.fori_loop(0, n_chunks, chunk, 0)


def _ret_bwd(log_g, q, k, v, ts):
    b, s, _ = q.shape
    n_tiles = s // ts
    tile = lambda width: pl.BlockSpec((1, ts, width), lambda bi, i: (bi, n_tiles - 1 - i, 0))
    return pl.pallas_call(
        functools.partial(_ret_bwd_kernel, ts=ts),
        grid=(b, n_tiles),
        in_specs=[pl.BlockSpec(memory_space=pltpu.SMEM), tile(RET_QK), tile(RET_QK), tile(RET_V)],
        out_specs=tile(RET_V),
        out_shape=jax.ShapeDtypeStruct((b, s, RET_V), F32),
        scratch_shapes=[pltpu.VMEM((RET_HEADS, RET_DK, RET_DV), F32)],
        compiler_params=_params(2),
        name="ret_bwd",
    )(log_g, q, k, v)


def _ret_fwd_kernel(lg_ref, q_ref, k_ref, v_ref, gate_ref, ob_ref, x_ref, ng_ref, w_ref, o_ref,
                    s_ref, z_ref, *, ts):
    i = pl.program_id(1)

    @pl.when(i == 0)
    def _():
        s_ref[...] = jnp.zeros_like(s_ref)

    pos = _chunk_pos()
    n_idx = lax.broadcasted_iota(jnp.int32, (RET_CHUNK, RET_CHUNK), 0)
    m_idx = lax.broadcasted_iota(jnp.int32, (RET_CHUNK, RET_CHUNK), 1)
    diff = (n_idx - m_idx).astype(F32)
    n_chunks = ts // RET_CHUNK

    def chunk(ci, carry):
        rows = pl.ds(pl.multiple_of(ci * RET_CHUNK, RET_CHUNK), RET_CHUNK)
        for h in range(RET_HEADS):
            lf = lg_ref[h]
            lb = lg_ref[RET_HEADS + h]
            decay = jnp.where(diff >= 0.0, jnp.exp(lf * jnp.maximum(diff, 0.0)),
                              jnp.exp(lb * jnp.maximum(-diff, 0.0)))
            xi = jnp.exp(lf * (pos + 1.0))
            zeta = jnp.exp(lf * (RET_CHUNK - 1.0 - pos))
            g_chunk = jnp.exp(jnp.full((1, 1), lf * RET_CHUNK, F32))
            qk = slice(h * RET_DK, (h + 1) * RET_DK)
            vv = slice(h * RET_DV, (h + 1) * RET_DV)
            qc = q_ref[0, rows, qk]
            kc = k_ref[0, rows, qk]
            vc = v_ref[0, rows, vv]
            state = s_ref[h]
            scores = _dot_nt(qc, kc.astype(BF16)) * decay
            y = (_dot(scores.astype(BF16), vc) + _dot(qc, state.astype(BF16)) * xi) + ob_ref[0, rows, vv]
            s_ref[h] = state * g_chunk + _dot_tn((kc * zeta).astype(BF16), vc)
            y = y * lax.rsqrt(jnp.mean(y * y, axis=-1, keepdims=True) + EPS)
            y = y * ng_ref[:, vv]
            z_ref[rows, vv] = (jax.nn.silu(gate_ref[0, rows, vv]) * y).astype(BF16)
        return carry

    lax.fori_loop(0, n_chunks, chunk, 0)
    o_ref[0] = x_ref[0] + _dot(z_ref[...], w_ref[...])


def _ret_fwd(log_g, q, k, v, gate, ob, x, ng, w_out, ts):
    b, s, d = x.shape
    tile = lambda width: pl.BlockSpec((1, ts, width), lambda bi, i: (bi, i, 0))
    return pl.pallas_call(
        functools.partial(_ret_fwd_kernel, ts=ts),
        grid=(b, s // ts),
        in_specs=[pl.BlockSpec(memory_space=pltpu.SMEM), tile(RET_QK), tile(RET_QK), tile(RET_V),
                  tile(RET_V), tile(RET_V), tile(d), _const_spec((1, RET_V)), _const_spec(w_out.shape)],
        out_specs=tile(d),
        out_shape=jax.ShapeDtypeStruct((b, s, d), F32),
        scratch_shapes=[pltpu.VMEM((RET_HEADS, RET_DK, RET_DV), F32), pltpu.VMEM((ts, RET_V), BF16)],
        compiler_params=_params(2),
        name="ret_fwd",
    )(log_g, q, k, v, gate, ob, x, ng, w_out)


def _prepare(norm_mix, norm_ffn, norm_final, lru_w_in, lru_conv_w, lru_conv_b, lru_w_a, lru_b_a,
             lru_w_x, lru_b_x, lru_lambda, lru_w_out, ret_w_in, ret_decay_logit, ret_norm, ret_w_out,
             ffn_w_in, ffn_conv_w, ffn_conv_b, ffn_w_out, seq):
    ffn = []
    for i in range(2):
        w_in = ffn_w_in[i].astype(BF16)
        ffn.append(dict(
            g=norm_ffn[i][None, :],
            wu=w_in[:, :D_FF],
            wv=w_in[:, D_FF:],
            cw=ffn_conv_w[i],
            cb=ffn_conv_b[i][None, :],
            wo=ffn_w_out[i].astype(BF16),
        ))
    half = RET_DK // 2
    theta = ROPE_BASE ** (-jnp.arange(half, dtype=F32) / half)
    ang = jnp.arange(seq, dtype=F32)[:, None] * theta[None, :]
    return dict(
        ffn=ffn,
        norm_mix=[norm_mix[0][None, :], norm_mix[1][None, :]],
        norm_final=norm_final[None, :],
        lru_w_in=lru_w_in[0].astype(BF16),
        lru_cw=lru_conv_w[0],
        lru_cb=lru_conv_b[0][None, :],
        lru_wa=[lru_w_a[0, d].astype(BF16) for d in range(2)],
        lru_ba=[lru_b_a[0, d][None, :] for d in range(2)],
        lru_wx=[lru_w_x[0, d].astype(BF16) for d in range(2)],
        lru_bx=[lru_b_x[0, d][None, :] for d in range(2)],
        lru_lam=[lru_lambda[0, d][None, :] for d in range(2)],
        lru_w_out=lru_w_out[0].astype(BF16),
        ret_w_in=ret_w_in[0].astype(BF16),
        ret_log_g=jax.nn.log_sigmoid(ret_decay_logit[0].astype(F32)).reshape(2 * RET_HEADS),
        ret_norm=ret_norm[0][None, :],
        ret_w_out=ret_w_out[0].astype(BF16),
        cos=jnp.cos(ang),
        sin=jnp.sin(ang),
    )


def _encoder(x, p, ts):
    b, s, d = x.shape
    gate, xc = _lru_in(x, p["norm_mix"][0], p["lru_w_in"], p["lru_cw"], p["lru_cb"], ts)
    xc = xc.reshape(s * b, d)
    h = [_lru_scan(xc, p["lru_wa"][di], p["lru_ba"][di], p["lru_wx"][di], p["lru_bx"][di], p["lru_lam"][di],
                   b, ts, reverse=bool(di)).reshape(s, b * d) for di in range(2)]
    x = _lru_out(h[0], h[1], gate, x, p["lru_w_out"], ts)
    f = p["ffn"][0]
    x = _ffn(x, f["g"], f["wu"], f["wv"], f["cw"], f["cb"], f["wo"], ts)
    q, k, v, gate = _ret_proj(x, p["norm_mix"][1], p["ret_w_in"], p["cos"], p["sin"], ts)
    ob = _ret_bwd(p["ret_log_g"], q, k, v, ts)
    x = _ret_fwd(p["ret_log_g"], q, k, v, gate, ob, x, p["ret_norm"], p["ret_w_out"], ts)
    f = p["ffn"][1]
    return _ffn(x, f["g"], f["wu"], f["wv"], f["cw"], f["cb"], f["wo"], ts, final_g=p["norm_final"])


def kernel(x_prompt, x_sample, norm_mix, norm_ffn, norm_final, lru_w_in, lru_conv_w, lru_conv_b, lru_w_a, lru_b_a, lru_w_x, lru_b_x, lru_lambda, lru_w_out, ret_w_in, ret_decay_logit, ret_norm, ret_w_out, ffn_w_in, ffn_conv_w, ffn_conv_b, ffn_w_out):
    assert x_prompt.shape[1] == x_sample.shape[1] and x_prompt.shape[1] % SEQ_TILE == 0
    assert all(SUBLANES % x.shape[0] == 0 for x in (x_prompt, x_sample))
    p = _prepare(norm_mix, norm_ffn, norm_final, lru_w_in, lru_conv_w, lru_conv_b, lru_w_a, lru_b_a,
                 lru_w_x, lru_b_x, lru_lambda, lru_w_out, ret_w_in, ret_decay_logit, ret_norm,
                 ret_w_out, ffn_w_in, ffn_conv_w, ffn_conv_b, ffn_w_out, x_prompt.shape[1])
    return (_encoder(x_prompt, p, SEQ_TILE), _encoder(x_sample, p, SEQ_TILE))
```

```python
import functools

import jax
import jax.numpy as jnp
from jax import lax
from jax.experimental import pallas as pl
from jax.experimental.pallas import tpu as pltpu

F32 = jnp.float32
BF16 = jnp.bfloat16

EPS = 1e-6
D_MODEL = 1024
LRU_BLOCKS = 4
LRU_BLOCK_W = D_MODEL // LRU_BLOCKS
LRU_C = 8.0
RET_HEADS = 4
RET_DK = 256
RET_DV = 512
RET_QK = RET_HEADS * RET_DK
RET_V = RET_HEADS * RET_DV
RET_CHUNK = 128
ROPE_BASE = 10000.0
D_FF = 2816

SUBLANES = 8
LANES = 128
LANE_SLABS = D_MODEL // LANES
HALO = SUBLANES
SEQ_TILE = 512
LRU_TILE = 256
SCAN_ROWS = 512
FF_CHUNK = 256
VMEM_LIMIT_BYTES = 56 * 1024 * 1024


def _params(semantics):
    return pltpu.CompilerParams(dimension_semantics=semantics, vmem_limit_bytes=VMEM_LIMIT_BYTES)


BATCH_THEN_TILES = ("parallel", "arbitrary")


def _const_spec(shape):
    zeros = (0,) * len(shape)
    return pl.BlockSpec(shape, lambda *_: zeros, pipeline_mode=pl.Buffered(1))


def _rms(x, g):
    return x * lax.rsqrt(jnp.mean(x * x, axis=-1, keepdims=True) + EPS) * g


def _gelu(x):
    return jax.nn.gelu(x, approximate=True)


def _dot(a, b):
    return jnp.dot(a, b, preferred_element_type=F32)


def _dot_tn(a, b):
    return lax.dot_general(a, b, (((0,), (0,)), ((), ())), preferred_element_type=F32)


def _dot_nt(a, b):
    return lax.dot_general(a, b, (((1,), (1,)), ((), ())), preferred_element_type=F32)


def _tile_and_halo_specs(s, ts, d, batch_first=True):
    halo_per_tile = ts // HALO
    n_halo = s // HALO

    def spec(rows, tile_to_block):
        if batch_first:
            return pl.BlockSpec((1, rows, d), lambda bi, i: (bi, tile_to_block(i), 0))
        return pl.BlockSpec((1, rows, d), lambda i, bi: (bi, tile_to_block(i), 0))

    tile = spec(ts, lambda i: i)
    prev = spec(HALO, lambda i: jnp.maximum(i * halo_per_tile - 1, 0))
    nxt = spec(HALO, lambda i: jnp.minimum((i + 1) * halo_per_tile, n_halo - 1))
    return tile, prev, nxt


def _store_normed_tile_with_halo(x, xp_ref, xn_ref, g, xs_ref, j, n_tiles, ts):
    xs_ref[0:ts] = _rms(x, g).astype(BF16)
    nxt = jnp.where(j == n_tiles - 1, 0.0, _rms(xn_ref[0], g))
    prv = jnp.where(j == 0, 0.0, _rms(xp_ref[0], g))
    xs_ref[ts:ts + 2 * HALO] = jnp.concatenate([nxt, prv], axis=0).astype(BF16)


def _time_shift(ext, k, ts):
    return pltpu.roll(ext, (-k) % ext.shape[0], 0)[0:ts]


def _token_major_spec(ts, batch):
    return pl.BlockSpec((LANE_SLABS, ts * batch, LANES), lambda i, bi: (0, i, 0))


def _lru_in_kernel(x_ref, xp_ref, xn_ref, g_ref, w_ref, cw_ref, cb_ref, gate_ref, xc_ref, xs_ref,
                   *, n_tiles, ts, batch):
    w = D_MODEL
    bi = pl.program_id(1)
    _store_normed_tile_with_halo(x_ref[0], xp_ref, xn_ref, g_ref[...], xs_ref, pl.program_id(0), n_tiles, ts)
    gate_ref[0] = _gelu(_dot(xs_ref[0:ts], w_ref[:, :w]))
    for nb in range(LRU_BLOCKS):
        sl = slice(nb * LRU_BLOCK_W, (nb + 1) * LRU_BLOCK_W)
        rec = _dot(xs_ref[...], w_ref[:, w + nb * LRU_BLOCK_W:w + (nb + 1) * LRU_BLOCK_W])
        xc = (_time_shift(rec, -2, ts) * cw_ref[0:1, sl] + _time_shift(rec, -1, ts) * cw_ref[1:2, sl]
              + rec[0:ts] * cw_ref[2:3, sl] + _time_shift(rec, 1, ts) * cw_ref[3:4, sl] + cb_ref[:, sl])
        for k in range(LRU_BLOCK_W // LANES):
            slab = nb * (LRU_BLOCK_W // LANES) + k
            xc_ref[slab, pl.ds(bi, ts, stride=batch), :] = xc[:, k * LANES:(k + 1) * LANES]


def _lru_in(x, g, w_in, cw, cb, ts):
    b, s, d = x.shape
    n_tiles = s // ts
    tile, prev, nxt = _tile_and_halo_specs(s, ts, d, batch_first=False)
    return pl.pallas_call(
        functools.partial(_lru_in_kernel, n_tiles=n_tiles, ts=ts, batch=b),
        grid=(n_tiles, b),
        in_specs=[tile, prev, nxt, _const_spec((1, d)), _const_spec((d, 2 * d)), _const_spec(cw.shape),
                  _const_spec((1, d))],
        out_specs=[tile, _token_major_spec(ts, b)],
        out_shape=[jax.ShapeDtypeStruct((b, s, d), F32), jax.ShapeDtypeStruct((LANE_SLABS, s * b, LANES), F32)],
        scratch_shapes=[pltpu.VMEM((ts + 2 * HALO, d), BF16)],
        compiler_params=_params(("arbitrary", "arbitrary")),
        name="lru_in",
    )(x, x, x, g, w_in, cw, cb)


def _lru_scan_kernel(xc_ref, wa_ref, ba_ref, wx_ref, bx_ref, lam_ref, h_ref, carry_ref, a_ref, u_ref,
                     *, reverse, rows, batch):
    @pl.when(pl.program_id(0) == 0)
    def _():
        carry_ref[...] = jnp.zeros_like(carry_ref)

    neg_lam = -lam_ref[...]
    softplus = jnp.maximum(neg_lam, 0.0) + jnp.log1p(jnp.exp(-jnp.abs(neg_lam)))
    scale = (-0.5 * LRU_C) * softplus
    slabs_per_block = LRU_BLOCK_W // LANES
    for nb in range(LRU_BLOCKS):
        sl = slice(nb * LRU_BLOCK_W, (nb + 1) * LRU_BLOCK_W)
        xb = jnp.concatenate([xc_ref[nb * slabs_per_block + k] for k in range(slabs_per_block)], axis=1)
        xb16 = xb.astype(BF16)
        tr = jnp.tanh(0.5 * (_dot(xb16, wa_ref[nb]) + ba_ref[:, sl]))
        ti = jnp.tanh(0.5 * (_dot(xb16, wx_ref[nb]) + bx_ref[:, sl]))
        log_a = scale[:, sl] * (tr + 1.0)
        a = jnp.exp(log_a)
        one_minus_a2 = -jnp.tanh(log_a) * (a * a + 1.0)
        a_ref[:, sl] = a
        u_ref[:, sl] = jnp.sqrt(one_minus_a2) * ((0.5 * xb) * (ti + 1.0))

    groups = rows // SUBLANES
    substeps = SUBLANES // batch
    shift = (SUBLANES - batch) if reverse else batch % SUBLANES
    sub = lax.broadcasted_iota(jnp.int32, (SUBLANES, D_MODEL), 0)

    def step(gi, c):
        g = (groups - 1 - gi) if reverse else gi
        r = pl.ds(pl.multiple_of(g * SUBLANES, SUBLANES), SUBLANES)
        a8 = a_ref[r, :]
        u8 = u_ref[r, :]
        h = a8 * c + u8
        out = h
        for k in range(1, substeps):
            h = a8 * pltpu.roll(h, shift, 0) + u8
            if reverse:
                out = jnp.where(sub < (substeps - k) * batch, h, out)
            else:
                out = jnp.where(sub >= k * batch, h, out)
        for slab in range(LANE_SLABS):
            h_ref[slab, r, :] = out[:, slab * LANES:(slab + 1) * LANES]
        return pltpu.roll(h, shift, 0) if substeps > 1 else h

    carry_ref[...] = lax.fori_loop(0, groups, step, carry_ref[...], unroll=4)


def _lru_scan(xc, wa, ba, wx, bx, lam, batch, rows, reverse):
    slabs, n, lanes = xc.shape
    w = slabs * lanes
    n_tiles = n // rows
    tile = pl.BlockSpec((slabs, rows, lanes), lambda i: (0, (n_tiles - 1 - i) if reverse else i, 0))
    row = _const_spec((1, w))
    gate_w = _const_spec((LRU_BLOCKS, LRU_BLOCK_W, LRU_BLOCK_W))
    kern = functools.partial(_lru_scan_kernel, reverse=reverse, rows=rows, batch=batch)
    return pl.pallas_call(
        kern,
        grid=(n_tiles,),
        in_specs=[tile, gate_w, row, gate_w, row, row],
        out_specs=tile,
        out_shape=jax.ShapeDtypeStruct(xc.shape, F32),
        scratch_shapes=[pltpu.VMEM((SUBLANES, w), F32), pltpu.VMEM((rows, w), F32),
                        pltpu.VMEM((rows, w), F32)],
        compiler_params=_params(("arbitrary",)),
        name="lru_scan_bwd" if reverse else "lru_scan_fwd",
    )(xc, wa, ba, wx, bx, lam)


def _lru_out_kernel(hf_ref, hb_ref, gate_ref, x_ref, w_ref, o_ref, z_ref, *, ts, batch):
    rows = pl.ds(pl.program_id(1), ts, stride=batch)
    for slab in range(LANE_SLABS):
        cols = slice(slab * LANES, (slab + 1) * LANES)
        h = hf_ref[slab, rows, :] + hb_ref[slab, rows, :]
        z_ref[:, cols] = (h * gate_ref[0, :, cols]).astype(BF16)
    o_ref[0] = x_ref[0] + _dot(z_ref[...], w_ref[...])


def _lru_out(hf, hb, gate, x, w_out, ts):
    b, s, d = x.shape
    tile = pl.BlockSpec((1, ts, d), lambda i, bi: (bi, i, 0))
    slabs = _token_major_spec(ts, b)
    return pl.pallas_call(
        functools.partial(_lru_out_kernel, ts=ts, batch=b),
        grid=(s // ts, b),
        in_specs=[slabs, slabs, tile, tile, _const_spec((d, d))],
        out_specs=tile,
        out_shape=jax.ShapeDtypeStruct((b, s, d), F32),
        scratch_shapes=[pltpu.VMEM((ts, d), BF16)],
        compiler_params=_params(("arbitrary", "arbitrary")),
        name="lru_out",
    )(hf, hb, gate, x, w_out)


def _ffn_kernel(*refs, n_tiles, ts, final):
    if final:
        (x_ref, xp_ref, xn_ref, g_ref, wu_ref, wv_ref, cw_ref, cb_ref, wo_ref, gf_ref,
         o_ref, xs_ref, act_ref) = refs
    else:
        (x_ref, xp_ref, xn_ref, g_ref, wu_ref, wv_ref, cw_ref, cb_ref, wo_ref,
         o_ref, xs_ref, act_ref) = refs
    x = x_ref[0]
    _store_normed_tile_with_halo(x, xp_ref, xn_ref, g_ref[...], xs_ref, pl.program_id(1), n_tiles, ts)
    for c in range(D_FF // FF_CHUNK):
        cols = slice(c * FF_CHUNK, (c + 1) * FF_CHUNK)
        u = _dot(xs_ref[...], wu_ref[:, cols])
        v = _dot(xs_ref[0:ts], wv_ref[:, cols])
        y = (_time_shift(u, -1, ts) * cw_ref[0:1, cols] + u[0:ts] * cw_ref[1:2, cols]
             + _time_shift(u, 1, ts) * cw_ref[2:3, cols] + cb_ref[:, cols])
        act_ref[:, cols] = (_gelu(y) * v).astype(BF16)

    out = x + _dot(act_ref[...], wo_ref[...])
    if final:
        out = _rms(out, gf_ref[...])
    o_ref[0] = out


def _ffn(x, g, wu, wv, cw, cb, wo, ts, final_g=None):
    b, s, d = x.shape
    n_tiles = s // ts
    tile, prev, nxt = _tile_and_halo_specs(s, ts, d)
    final = final_g is not None
    in_specs = [tile, prev, nxt, _const_spec((1, d)), _const_spec(wu.shape), _const_spec(wv.shape),
                _const_spec(cw.shape), _const_spec(cb.shape), _const_spec(wo.shape)]
    args = [x, x, x, g, wu, wv, cw, cb, wo]
    if final:
        in_specs.append(_const_spec((1, d)))
        args.append(final_g)
    kern = functools.partial(_ffn_kernel, n_tiles=n_tiles, ts=ts, final=final)
    return pl.pallas_call(
        kern,
        grid=(b, n_tiles),
        in_specs=in_specs,
        out_specs=tile,
        out_shape=jax.ShapeDtypeStruct((b, s, d), F32),
        scratch_shapes=[pltpu.VMEM((ts + 2 * HALO, d), BF16), pltpu.VMEM((ts, D_FF), BF16)],
        compiler_params=_params(BATCH_THEN_TILES),
        name="ffn_final" if final else "ffn",
    )(*args)


def _ret_proj_kernel(x_ref, g_ref, w_ref, cos_ref, sin_ref, q_ref, k_ref, v_ref, gate_ref, xs_ref):
    xs_ref[...] = _rms(x_ref[0], g_ref[...]).astype(BF16)
    cos = cos_ref[...]
    sin = sin_ref[...]
    half = RET_DK // 2
    for h in range(RET_HEADS):
        for base, out_ref, scale in ((0, q_ref, None), (RET_QK, k_ref, RET_DK ** -0.5)):
            lo = base + h * RET_DK
            t = _dot(xs_ref[...], w_ref[:, lo:lo + RET_DK])
            if scale is not None:
                t = t * scale
            t1 = t[:, :half]
            t2 = t[:, half:]
            col = h * RET_DK
            out_ref[0, :, col:col + half] = (t1 * cos - t2 * sin).astype(out_ref.dtype)
            out_ref[0, :, col + half:col + RET_DK] = (t2 * cos + t1 * sin).astype(out_ref.dtype)
    for h in range(RET_HEADS):
        lo = 2 * RET_QK + h * RET_DV
        v_ref[0, :, h * RET_DV:(h + 1) * RET_DV] = _dot(xs_ref[...], w_ref[:, lo:lo + RET_DV]).astype(BF16)
        lo = 2 * RET_QK + RET_V + h * RET_DV
        gate_ref[0, :, h * RET_DV:(h + 1) * RET_DV] = _dot(xs_ref[...], w_ref[:, lo:lo + RET_DV])


def _ret_proj(x, g, w_in, cos, sin, ts):
    b, s, d = x.shape
    tile = lambda width: pl.BlockSpec((1, ts, width), lambda bi, i: (bi, i, 0))
    rope = pl.BlockSpec((ts, RET_DK // 2), lambda bi, i: (i, 0))
    return pl.pallas_call(
        _ret_proj_kernel,
        grid=(b, s // ts),
        in_specs=[tile(d), _const_spec((1, d)), _const_spec(w_in.shape), rope, rope],
        out_specs=[tile(RET_QK), tile(RET_QK), tile(RET_V), tile(RET_V)],
        out_shape=[jax.ShapeDtypeStruct((b, s, RET_QK), BF16), jax.ShapeDtypeStruct((b, s, RET_QK), F32),
                   jax.ShapeDtypeStruct((b, s, RET_V), BF16), jax.ShapeDtypeStruct((b, s, RET_V), F32)],
        scratch_shapes=[pltpu.VMEM((ts, d), BF16)],
        compiler_params=_params(BATCH_THEN_TILES),
        name="ret_proj",
    )(x, g, w_in, cos, sin)


def _chunk_pos():
    return lax.broadcasted_iota(jnp.int32, (RET_CHUNK, 1), 0).astype(F32)


def _ret_bwd_kernel(lg_ref, q_ref, k_ref, v_ref, ob_ref, s_ref, *, ts):
    i = pl.program_id(1)

    @pl.when(i == 0)
    def _():
        s_ref[...] = jnp.zeros_like(s_ref)

    pos = _chunk_pos()
    n_chunks = ts // RET_CHUNK

    def chunk(ci, carry):
        r0 = pl.multiple_of((n_chunks - 1 - ci) * RET_CHUNK, RET_CHUNK)
        rows = pl.ds(r0, RET_CHUNK)
        for h in range(RET_HEADS):
            lg = lg_ref[RET_HEADS + h]
            xi = jnp.exp(lg * (RET_CHUNK - pos))
            zeta = jnp.exp(lg * pos)
            g_chunk = jnp.exp(jnp.full((1, 1), lg * RET_CHUNK, F32))
            qk = slice(h * RET_DK, (h + 1) * RET_DK)
            vv = slice(h * RET_DV, (h + 1) * RET_DV)
            qc = q_ref[0, rows, qk]
            kc = k_ref[0, rows, qk]
            vc = v_ref[0, rows, vv]
            state = s_ref[h]
            ob_ref[0, rows, vv] = _dot(qc, state.astype(BF16)) * xi
            s_ref[h] = state * g_chunk + _dot_tn((kc * zeta).astype(BF16), vc)
        return carry

    lax.fori_loop(0, n_chunks, chunk, 0)


def _ret_bwd(log_g, q, k, v, ts):
    b, s, _ = q.shape
    n_tiles = s // ts
    tile = lambda width: pl.BlockSpec((1, ts, width), lambda bi, i: (bi, n_tiles - 1 - i, 0))
    return pl.pallas_call(
        functools.partial(_ret_bwd_kernel, ts=ts),
        grid=(b, n_tiles),
        in_specs=[pl.BlockSpec(memory_space=pltpu.SMEM), tile(RET_QK), tile(RET_QK), tile(RET_V)],
        out_specs=tile(RET_V),
        out_shape=jax.ShapeDtypeStruct((b, s, RET_V), F32),
        scratch_shapes=[pltpu.VMEM((RET_HEADS, RET_DK, RET_DV), F32)],
        compiler_params=_params(BATCH_THEN_TILES),
        name="ret_bwd",
    )(log_g, q, k, v)


def _ret_fwd_kernel(lg_ref, q_ref, k_ref, v_ref, gate_ref, ob_ref, x_ref, ng_ref, w_ref, o_ref,
                    s_ref, z_ref, *, ts):
    i = pl.program_id(1)

    @pl.when(i == 0)
    def _():
        s_ref[...] = jnp.zeros_like(s_ref)

    pos = _chunk_pos()
    n_idx = lax.broadcasted_iota(jnp.int32, (RET_CHUNK, RET_CHUNK), 0)
    m_idx = lax.broadcasted_iota(jnp.int32, (RET_CHUNK, RET_CHUNK), 1)
    diff = (n_idx - m_idx).astype(F32)
    n_chunks = ts // RET_CHUNK

    def chunk(ci, carry):
        rows = pl.ds(pl.multiple_of(ci * RET_CHUNK, RET_CHUNK), RET_CHUNK)
        for h in range(RET_HEADS):
            lf = lg_ref[h]
            lb = lg_ref[RET_HEADS + h]
            decay = jnp.where(diff >= 0.0, jnp.exp(lf * jnp.maximum(diff, 0.0)),
                              jnp.exp(lb * jnp.maximum(-diff, 0.0)))
            xi = jnp.exp(lf * (pos + 1.0))
            zeta = jnp.exp(lf * (RET_CHUNK - 1.0 - pos))
            g_chunk = jnp.exp(jnp.full((1, 1), lf * RET_CHUNK, F32))
            qk = slice(h * RET_DK, (h + 1) * RET_DK)
            vv = slice(h * RET_DV, (h + 1) * RET_DV)
            qc = q_ref[0, rows, qk]
            kc = k_ref[0, rows, qk]
            vc = v_ref[0, rows, vv]
            state = s_ref[h]
            scores = _dot_nt(qc, kc.astype(BF16)) * decay
            y = (_dot(scores.astype(BF16), vc) + _dot(qc, state.astype(BF16)) * xi) + ob_ref[0, rows, vv]
            s_ref[h] = state * g_chunk + _dot_tn((kc * zeta).astype(BF16), vc)
            y = y * lax.rsqrt(jnp.mean(y * y, axis=-1, keepdims=True) + EPS)
            y = y * ng_ref[:, vv]
            z_ref[rows, vv] = (jax.nn.silu(gate_ref[0, rows, vv]) * y).astype(BF16)
        return carry

    lax.fori_loop(0, n_chunks, chunk, 0)
    o_ref[0] = x_ref[0] + _dot(z_ref[...], w_ref[...])


def _ret_fwd(log_g, q, k, v, gate, ob, x, ng, w_out, ts):
    b, s, d = x.shape
    tile = lambda width: pl.BlockSpec((1, ts, width), lambda bi, i: (bi, i, 0))
    return pl.pallas_call(
        functools.partial(_ret_fwd_kernel, ts=ts),
        grid=(b, s // ts),
        in_specs=[pl.BlockSpec(memory_space=pltpu.SMEM), tile(RET_QK), tile(RET_QK), tile(RET_V),
                  tile(RET_V), tile(RET_V), tile(d), _const_spec((1, RET_V)), _const_spec(w_out.shape)],
        out_specs=tile(d),
        out_shape=jax.ShapeDtypeStruct((b, s, d), F32),
        scratch_shapes=[pltpu.VMEM((RET_HEADS, RET_DK, RET_DV), F32), pltpu.VMEM((ts, RET_V), BF16)],
        compiler_params=_params(BATCH_THEN_TILES),
        name="ret_fwd",
    )(log_g, q, k, v, gate, ob, x, ng, w_out)


def _prepare(norm_mix, norm_ffn, norm_final, lru_w_in, lru_conv_w, lru_conv_b, lru_w_a, lru_b_a,
             lru_w_x, lru_b_x, lru_lambda, lru_w_out, ret_w_in, ret_decay_logit, ret_norm, ret_w_out,
             ffn_w_in, ffn_conv_w, ffn_conv_b, ffn_w_out, seq):
    ffn = []
    for i in range(2):
        w_in = ffn_w_in[i].astype(BF16)
        ffn.append(dict(
            g=norm_ffn[i][None, :],
            wu=w_in[:, :D_FF],
            wv=w_in[:, D_FF:],
            cw=ffn_conv_w[i],
            cb=ffn_conv_b[i][None, :],
            wo=ffn_w_out[i].astype(BF16),
        ))
    half = RET_DK // 2
    theta = ROPE_BASE ** (-jnp.arange(half, dtype=F32) / half)
    ang = jnp.arange(seq, dtype=F32)[:, None] * theta[None, :]
    return dict(
        ffn=ffn,
        norm_mix=[norm_mix[0][None, :], norm_mix[1][None, :]],
        norm_final=norm_final[None, :],
        lru_w_in=lru_w_in[0].astype(BF16),
        lru_cw=lru_conv_w[0],
        lru_cb=lru_conv_b[0][None, :],
        lru_wa=[lru_w_a[0, d].astype(BF16) for d in range(2)],
        lru_ba=[lru_b_a[0, d][None, :] for d in range(2)],
        lru_wx=[lru_w_x[0, d].astype(BF16) for d in range(2)],
        lru_bx=[lru_b_x[0, d][None, :] for d in range(2)],
        lru_lam=[lru_lambda[0, d][None, :] for d in range(2)],
        lru_w_out=lru_w_out[0].astype(BF16),
        ret_w_in=ret_w_in[0].astype(BF16),
        ret_log_g=jax.nn.log_sigmoid(ret_decay_logit[0].astype(F32)).reshape(2 * RET_HEADS),
        ret_norm=ret_norm[0][None, :],
        ret_w_out=ret_w_out[0].astype(BF16),
        cos=jnp.cos(ang),
        sin=jnp.sin(ang),
    )


def _encoder(x, p, ts):
    b, s, d = x.shape
    lru_ts = min(ts, LRU_TILE)
    gate, xc = _lru_in(x, p["norm_mix"][0], p["lru_w_in"], p["lru_cw"], p["lru_cb"], lru_ts)
    h = [_lru_scan(xc, p["lru_wa"][di], p["lru_ba"][di], p["lru_wx"][di], p["lru_bx"][di], p["lru_lam"][di],
                   b, min(SCAN_ROWS, lru_ts * b), reverse=bool(di)) for di in range(2)]
    x = _lru_out(h[0], h[1], gate, x, p["lru_w_out"], lru_ts)
    f = p["ffn"][0]
    x = _ffn(x, f["g"], f["wu"], f["wv"], f["cw"], f["cb"], f["wo"], ts)
    q, k, v, gate = _ret_proj(x, p["norm_mix"][1], p["ret_w_in"], p["cos"], p["sin"], ts)
    ob = _ret_bwd(p["ret_log_g"], q, k, v, ts)
    x = _ret_fwd(p["ret_log_g"], q, k, v, gate, ob, x, p["ret_norm"], p["ret_w_out"], ts)
    f = p["ffn"][1]
    return _ffn(x, f["g"], f["wu"], f["wv"], f["cw"], f["cb"], f["wo"], ts, final_g=p["norm_final"])


def kernel(x_prompt, x_sample, norm_mix, norm_ffn, norm_final, lru_w_in, lru_conv_w, lru_conv_b, lru_w_a, lru_b_a, lru_w_x, lru_b_x, lru_lambda, lru_w_out, ret_w_in, ret_decay_logit, ret_norm, ret_w_out, ffn_w_in, ffn_conv_w, ffn_conv_b, ffn_w_out):
    assert x_prompt.shape[1] == x_sample.shape[1] and x_prompt.shape[1] % SEQ_TILE == 0
    assert all(SUBLANES % x.shape[0] == 0 for x in (x_prompt, x_sample))
    p = _prepare(norm_mix, norm_ffn, norm_final, lru_w_in, lru_conv_w, lru_conv_b, lru_w_a, lru_b_a,
                 lru_w_x, lru_b_x, lru_lambda, lru_w_out, ret_w_in, ret_decay_logit, ret_norm,
                 ret_w_out, ffn_w_in, ffn_conv_w, ffn_conv_b, ffn_w_out, x_prompt.shape[1])
    return (_encoder(x_prompt, p, SEQ_TILE), _encoder(x_sample, p, SEQ_TILE))
```

```python
import functools

import jax
import jax.numpy as jnp
from jax import lax
from jax.experimental import pallas as pl
from jax.experimental.pallas import tpu as pltpu

F32 = jnp.float32
BF16 = jnp.bfloat16

EPS = 1e-6
D_MODEL = 1024
LRU_BLOCKS = 4
LRU_BLOCK_W = D_MODEL // LRU_BLOCKS
LRU_C = 8.0
RET_HEADS = 4
RET_DK = 256
RET_DV = 512
RET_QK = RET_HEADS * RET_DK
RET_V = RET_HEADS * RET_DV
RET_CHUNK = 128
ROPE_BASE = 10000.0
D_FF = 2816

SUBLANES = 8
LANES = 128
LANE_SLABS = D_MODEL // LANES
HALO = SUBLANES
SEQ_TILE = 512
LRU_TILE = 256
SCAN_ROWS = 512
FF_CHUNK = 256
VMEM_LIMIT_BYTES = 56 * 1024 * 1024


def _params(semantics):
    return pltpu.CompilerParams(dimension_semantics=semantics, vmem_limit_bytes=VMEM_LIMIT_BYTES)


BATCH_THEN_TILES = ("parallel", "arbitrary")


def _const_spec(shape):
    zeros = (0,) * len(shape)
    return pl.BlockSpec(shape, lambda *_: zeros, pipeline_mode=pl.Buffered(1))


def _rms(x, g):
    return x * lax.rsqrt(jnp.mean(x * x, axis=-1, keepdims=True) + EPS) * g


def _gelu(x):
    return jax.nn.gelu(x, approximate=True)


def _dot(a, b):
    return jnp.dot(a, b, preferred_element_type=F32)


def _dot_tn(a, b):
    return lax.dot_general(a, b, (((0,), (0,)), ((), ())), preferred_element_type=F32)


def _dot_nt(a, b):
    return lax.dot_general(a, b, (((1,), (1,)), ((), ())), preferred_element_type=F32)


def _tile_and_halo_specs(s, ts, d, batch_first=True):
    halo_per_tile = ts // HALO
    n_halo = s // HALO

    def spec(rows, tile_to_block):
        if batch_first:
            return pl.BlockSpec((1, rows, d), lambda bi, i: (bi, tile_to_block(i), 0))
        return pl.BlockSpec((1, rows, d), lambda i, bi: (bi, tile_to_block(i), 0))

    tile = spec(ts, lambda i: i)
    prev = spec(HALO, lambda i: jnp.maximum(i * halo_per_tile - 1, 0))
    nxt = spec(HALO, lambda i: jnp.minimum((i + 1) * halo_per_tile, n_halo - 1))
    return tile, prev, nxt


def _store_normed_tile_with_halo(x, xp_ref, xn_ref, g, xs_ref, j, n_tiles, ts):
    xs_ref[0:ts] = _rms(x, g).astype(BF16)
    nxt = jnp.where(j == n_tiles - 1, 0.0, _rms(xn_ref[0], g))
    prv = jnp.where(j == 0, 0.0, _rms(xp_ref[0], g))
    xs_ref[ts:ts + 2 * HALO] = jnp.concatenate([nxt, prv], axis=0).astype(BF16)


def _time_shift(ext, k, ts):
    return pltpu.roll(ext, (-k) % ext.shape[0], 0)[0:ts]


def _token_major_spec(ts, batch):
    return pl.BlockSpec((LANE_SLABS, ts * batch, LANES), lambda i, bi: (0, i, 0))


def _lru_in_kernel(x_ref, xp_ref, xn_ref, g_ref, w_ref, cw_ref, cb_ref, gate_ref, xc_ref, xs_ref,
                   *, n_tiles, ts, batch):
    w = D_MODEL
    bi = pl.program_id(1)
    _store_normed_tile_with_halo(x_ref[0], xp_ref, xn_ref, g_ref[...], xs_ref, pl.program_id(0), n_tiles, ts)
    gate_ref[0] = _gelu(_dot(xs_ref[0:ts], w_ref[:, :w]))
    for nb in range(LRU_BLOCKS):
        sl = slice(nb * LRU_BLOCK_W, (nb + 1) * LRU_BLOCK_W)
        rec = _dot(xs_ref[...], w_ref[:, w + nb * LRU_BLOCK_W:w + (nb + 1) * LRU_BLOCK_W])
        xc = (_time_shift(rec, -2, ts) * cw_ref[0:1, sl] + _time_shift(rec, -1, ts) * cw_ref[1:2, sl]
              + rec[0:ts] * cw_ref[2:3, sl] + _time_shift(rec, 1, ts) * cw_ref[3:4, sl] + cb_ref[:, sl])
        for k in range(LRU_BLOCK_W // LANES):
            slab = nb * (LRU_BLOCK_W // LANES) + k
            xc_ref[slab, pl.ds(bi, ts, stride=batch), :] = xc[:, k * LANES:(k + 1) * LANES]


def _lru_in(x, g, w_in, cw, cb, ts):
    b, s, d = x.shape
    n_tiles = s // ts
    tile, prev, nxt = _tile_and_halo_specs(s, ts, d, batch_first=False)
    return pl.pallas_call(
        functools.partial(_lru_in_kernel, n_tiles=n_tiles, ts=ts, batch=b),
        grid=(n_tiles, b),
        in_specs=[tile, prev, nxt, _const_spec((1, d)), _const_spec((d, 2 * d)), _const_spec(cw.shape),
                  _const_spec((1, d))],
        out_specs=[tile, _token_major_spec(ts, b)],
        out_shape=[jax.ShapeDtypeStruct((b, s, d), F32), jax.ShapeDtypeStruct((LANE_SLABS, s * b, LANES), F32)],
        scratch_shapes=[pltpu.VMEM((ts + 2 * HALO, d), BF16)],
        compiler_params=_params(("arbitrary", "arbitrary")),
        name="lru_in",
    )(x, x, x, g, w_in, cw, cb)


def _lru_scan_kernel(xc_ref, wa_ref, ba_ref, wx_ref, bx_ref, lam_ref, h_ref, carry_ref, a_ref, u_ref,
                     *, reverse, rows, batch):
    @pl.when(pl.program_id(0) == 0)
    def _():
        carry_ref[...] = jnp.zeros_like(carry_ref)

    neg_lam = -lam_ref[...]
    softplus = jnp.maximum(neg_lam, 0.0) + jnp.log1p(jnp.exp(-jnp.abs(neg_lam)))
    scale = (-0.5 * LRU_C) * softplus
    slabs_per_block = LRU_BLOCK_W // LANES
    for nb in range(LRU_BLOCKS):
        sl = slice(nb * LRU_BLOCK_W, (nb + 1) * LRU_BLOCK_W)
        xb = jnp.concatenate([xc_ref[nb * slabs_per_block + k] for k in range(slabs_per_block)], axis=1)
        xb16 = xb.astype(BF16)
        tr = jnp.tanh(0.5 * (_dot(xb16, wa_ref[nb]) + ba_ref[:, sl]))
        ti = jnp.tanh(0.5 * (_dot(xb16, wx_ref[nb]) + bx_ref[:, sl]))
        log_a = scale[:, sl] * (tr + 1.0)
        a = jnp.exp(log_a)
        one_minus_a2 = -jnp.tanh(log_a) * (a * a + 1.0)
        a_ref[:, sl] = a
        u_ref[:, sl] = jnp.sqrt(one_minus_a2) * ((0.5 * xb) * (ti + 1.0))

    groups = rows // SUBLANES
    substeps = SUBLANES // batch
    shift = (SUBLANES - batch) if reverse else batch % SUBLANES
    sub = lax.broadcasted_iota(jnp.int32, (SUBLANES, D_MODEL), 0)

    def step(gi, c):
        g = (groups - 1 - gi) if reverse else gi
        r = pl.ds(pl.multiple_of(g * SUBLANES, SUBLANES), SUBLANES)
        a8 = a_ref[r, :]
        u8 = u_ref[r, :]
        h = a8 * c + u8
        out = h
        for k in range(1, substeps):
            h = a8 * pltpu.roll(h, shift, 0) + u8
            if reverse:
                out = jnp.where(sub < (substeps - k) * batch, h, out)
            else:
                out = jnp.where(sub >= k * batch, h, out)
        for slab in range(LANE_SLABS):
            h_ref[slab, r, :] = out[:, slab * LANES:(slab + 1) * LANES]
        return pltpu.roll(h, shift, 0) if substeps > 1 else h

    carry_ref[...] = lax.fori_loop(0, groups, step, carry_ref[...], unroll=4)


def _lru_scan(xc, wa, ba, wx, bx, lam, batch, rows, reverse):
    slabs, n, lanes = xc.shape
    w = slabs * lanes
    n_tiles = n // rows
    tile = pl.BlockSpec((slabs, rows, lanes), lambda i: (0, (n_tiles - 1 - i) if reverse else i, 0))
    row = _const_spec((1, w))
    gate_w = _const_spec((LRU_BLOCKS, LRU_BLOCK_W, LRU_BLOCK_W))
    kern = functools.partial(_lru_scan_kernel, reverse=reverse, rows=rows, batch=batch)
    return pl.pallas_call(
        kern,
        grid=(n_tiles,),
        in_specs=[tile, gate_w, row, gate_w, row, row],
        out_specs=tile,
        out_shape=jax.ShapeDtypeStruct(xc.shape, F32),
        scratch_shapes=[pltpu.VMEM((SUBLANES, w), F32), pltpu.VMEM((rows, w), F32),
                        pltpu.VMEM((rows, w), F32)],
        compiler_params=_params(("arbitrary",)),
        name="lru_scan_bwd" if reverse else "lru_scan_fwd",
    )(xc, wa, ba, wx, bx, lam)


def _lru_out_kernel(hf_ref, hb_ref, gate_ref, x_ref, w_ref, o_ref, z_ref, *, ts, batch):
    rows = pl.ds(pl.program_id(1), ts, stride=batch)
    for slab in range(LANE_SLABS):
        cols = slice(slab * LANES, (slab + 1) * LANES)
        h = hf_ref[slab, rows, :] + hb_ref[slab, rows, :]
        z_ref[:, cols] = (h * gate_ref[0, :, cols]).astype(BF16)
    o_ref[0] = x_ref[0] + _dot(z_ref[...], w_ref[...])


def _lru_out(hf, hb, gate, x, w_out, ts):
    b, s, d = x.shape
    tile = pl.BlockSpec((1, ts, d), lambda i, bi: (bi, i, 0))
    slabs = _token_major_spec(ts, b)
    return pl.pallas_call(
        functools.partial(_lru_out_kernel, ts=ts, batch=b),
        grid=(s // ts, b),
        in_specs=[slabs, slabs, tile, tile, _const_spec((d, d))],
        out_specs=tile,
        out_shape=jax.ShapeDtypeStruct((b, s, d), F32),
        scratch_shapes=[pltpu.VMEM((ts, d), BF16)],
        compiler_params=_params(("arbitrary", "arbitrary")),
        name="lru_out",
    )(hf, hb, gate, x, w_out)


def _ffn_kernel(*refs, n_tiles, ts, final):
    if final:
        (x_ref, xp_ref, xn_ref, g_ref, wu_ref, wv_ref, cw_ref, cb_ref, wo_ref, gf_ref,
         o_ref, xs_ref, act_ref) = refs
    else:
        (x_ref, xp_ref, xn_ref, g_ref, wu_ref, wv_ref, cw_ref, cb_ref, wo_ref,
         o_ref, xs_ref, act_ref) = refs
    x = x_ref[0]
    _store_normed_tile_with_halo(x, xp_ref, xn_ref, g_ref[...], xs_ref, pl.program_id(1), n_tiles, ts)
    for c in range(D_FF // FF_CHUNK):
        cols = slice(c * FF_CHUNK, (c + 1) * FF_CHUNK)
        u = _dot(xs_ref[...], wu_ref[:, cols])
        v = _dot(xs_ref[0:ts], wv_ref[:, cols])
        y = (_time_shift(u, -1, ts) * cw_ref[0:1, cols] + u[0:ts] * cw_ref[1:2, cols]
             + _time_shift(u, 1, ts) * cw_ref[2:3, cols] + cb_ref[:, cols])
        act_ref[:, cols] = (_gelu(y) * v).astype(BF16)

    out = x + _dot(act_ref[...], wo_ref[...])
    if final:
        out = _rms(out, gf_ref[...])
    o_ref[0] = out


def _ffn(x, g, wu, wv, cw, cb, wo, ts, final_g=None):
    b, s, d = x.shape
    n_tiles = s // ts
    tile, prev, nxt = _tile_and_halo_specs(s, ts, d)
    final = final_g is not None
    in_specs = [tile, prev, nxt, _const_spec((1, d)), _const_spec(wu.shape), _const_spec(wv.shape),
                _const_spec(cw.shape), _const_spec(cb.shape), _const_spec(wo.shape)]
    args = [x, x, x, g, wu, wv, cw, cb, wo]
    if final:
        in_specs.append(_const_spec((1, d)))
        args.append(final_g)
    kern = functools.partial(_ffn_kernel, n_tiles=n_tiles, ts=ts, final=final)
    return pl.pallas_call(
        kern,
        grid=(b, n_tiles),
        in_specs=in_specs,
        out_specs=tile,
        out_shape=jax.ShapeDtypeStruct((b, s, d), F32),
        scratch_shapes=[pltpu.VMEM((ts + 2 * HALO, d), BF16), pltpu.VMEM((ts, D_FF), BF16)],
        compiler_params=_params(BATCH_THEN_TILES),
        name="ffn_final" if final else "ffn",
    )(*args)


def _chunk_pos(rows):
    return (lax.broadcasted_iota(jnp.int32, (rows, 1), 0) % RET_CHUNK).astype(F32)


def _ret_proj_kernel(lg_ref, x_ref, g_ref, w_ref, cos_ref, sin_ref,
                     q_ref, k_ref, kz_ref, v_ref, sg_ref, ob_ref, xs_ref, kzb_ref, s_ref, *, ts):
    @pl.when(pl.program_id(1) == 0)
    def _():
        s_ref[...] = jnp.zeros_like(s_ref)

    xs_ref[...] = _rms(x_ref[0], g_ref[...]).astype(BF16)
    cos = cos_ref[...]
    sin = sin_ref[...]
    half = RET_DK // 2
    pos = _chunk_pos(ts)

    def rotary(t):
        t1 = t[:, :half]
        t2 = t[:, half:]
        return t1 * cos - t2 * sin, t2 * cos + t1 * sin

    for h in range(RET_HEADS):
        lo = h * RET_DK
        mid = lo + half
        hi = lo + RET_DK
        q1, q2 = rotary(_dot(xs_ref[...], w_ref[:, lo:hi]))
        q_ref[0, :, lo:mid] = q1.astype(BF16)
        q_ref[0, :, mid:hi] = q2.astype(BF16)
        k1, k2 = rotary(_dot(xs_ref[...], w_ref[:, RET_QK + lo:RET_QK + hi]) * (RET_DK ** -0.5))
        k_ref[0, :, lo:mid] = k1.astype(BF16)
        k_ref[0, :, mid:hi] = k2.astype(BF16)
        zeta_f = jnp.exp(lg_ref[h] * (RET_CHUNK - 1.0 - pos))
        kz_ref[0, :, lo:mid] = (k1 * zeta_f).astype(BF16)
        kz_ref[0, :, mid:hi] = (k2 * zeta_f).astype(BF16)
        zeta_b = jnp.exp(lg_ref[RET_HEADS + h] * pos)
        kzb_ref[:, lo:mid] = (k1 * zeta_b).astype(BF16)
        kzb_ref[:, mid:hi] = (k2 * zeta_b).astype(BF16)
    for h in range(RET_HEADS):
        vv = slice(h * RET_DV, (h + 1) * RET_DV)
        lo = 2 * RET_QK + h * RET_DV
        v_ref[0, :, vv] = _dot(xs_ref[...], w_ref[:, lo:lo + RET_DV]).astype(BF16)
        lo = 2 * RET_QK + RET_V + h * RET_DV
        sg_ref[0, :, vv] = jax.nn.silu(_dot(xs_ref[...], w_ref[:, lo:lo + RET_DV]))

    cpos = _chunk_pos(RET_CHUNK)
    for c in reversed(range(ts // RET_CHUNK)):
        rows = slice(c * RET_CHUNK, (c + 1) * RET_CHUNK)
        for h in range(RET_HEADS):
            lg = lg_ref[RET_HEADS + h]
            xi = jnp.exp(lg * (RET_CHUNK - cpos))
            g_chunk = jnp.exp(jnp.full((1, 1), lg * RET_CHUNK, F32))
            qk = slice(h * RET_DK, (h + 1) * RET_DK)
            vv = slice(h * RET_DV, (h + 1) * RET_DV)
            state = s_ref[h]
            ob_ref[0, rows, vv] = _dot(q_ref[0, rows, qk], state.astype(BF16)) * xi
            s_ref[h] = state * g_chunk + _dot_tn(kzb_ref[rows, qk], v_ref[0, rows, vv])


def _ret_proj(log_g, x, g, w_in, cos, sin, ts):
    b, s, d = x.shape
    n_tiles = s // ts
    tile = lambda width: pl.BlockSpec((1, ts, width), lambda bi, i: (bi, n_tiles - 1 - i, 0))
    rope = pl.BlockSpec((ts, RET_DK // 2), lambda bi, i: (n_tiles - 1 - i, 0))
    act = lambda width, dtype: jax.ShapeDtypeStruct((b, s, width), dtype)
    return pl.pallas_call(
        functools.partial(_ret_proj_kernel, ts=ts),
        grid=(b, n_tiles),
        in_specs=[pl.BlockSpec(memory_space=pltpu.SMEM), tile(d), _const_spec((1, d)), _const_spec(w_in.shape),
                  rope, rope],
        out_specs=[tile(RET_QK), tile(RET_QK), tile(RET_QK), tile(RET_V), tile(RET_V), tile(RET_V)],
        out_shape=[act(RET_QK, BF16), act(RET_QK, BF16), act(RET_QK, BF16), act(RET_V, BF16),
                   act(RET_V, F32), act(RET_V, F32)],
        scratch_shapes=[pltpu.VMEM((ts, d), BF16), pltpu.VMEM((ts, RET_QK), BF16),
                        pltpu.VMEM((RET_HEADS, RET_DK, RET_DV), F32)],
        compiler_params=_params(BATCH_THEN_TILES),
        name="ret_proj",
    )(log_g, x, g, w_in, cos, sin)


def _ret_fwd_kernel(lg_ref, q_ref, k_ref, kz_ref, v_ref, sg_ref, ob_ref, x_ref, ng_ref, w_ref, o_ref,
                    s_ref, z_ref, *, ts):
    @pl.when(pl.program_id(1) == 0)
    def _():
        s_ref[...] = jnp.zeros_like(s_ref)

    cpos = _chunk_pos(RET_CHUNK)
    n_idx = lax.broadcasted_iota(jnp.int32, (RET_CHUNK, RET_CHUNK), 0)
    m_idx = lax.broadcasted_iota(jnp.int32, (RET_CHUNK, RET_CHUNK), 1)
    diff = (n_idx - m_idx).astype(F32)
    for h in range(RET_HEADS):
        lf = lg_ref[h]
        lb = lg_ref[RET_HEADS + h]
        decay = jnp.where(diff >= 0.0, jnp.exp(lf * jnp.maximum(diff, 0.0)),
                          jnp.exp(lb * jnp.maximum(-diff, 0.0)))
        xi = jnp.exp(lf * (cpos + 1.0))
        g_chunk = jnp.exp(jnp.full((1, 1), lf * RET_CHUNK, F32))
        qk = slice(h * RET_DK, (h + 1) * RET_DK)
        vv = slice(h * RET_DV, (h + 1) * RET_DV)
        for c in range(ts // RET_CHUNK):
            rows = slice(c * RET_CHUNK, (c + 1) * RET_CHUNK)
            qc = q_ref[0, rows, qk]
            vc = v_ref[0, rows, vv]
            state = s_ref[h]
            scores = _dot_nt(qc, k_ref[0, rows, qk]) * decay
            y = (_dot(scores.astype(BF16), vc) + _dot(qc, state.astype(BF16)) * xi) + ob_ref[0, rows, vv]
            s_ref[h] = state * g_chunk + _dot_tn(kz_ref[0, rows, qk], vc)
            y = y * lax.rsqrt(jnp.mean(y * y, axis=-1, keepdims=True) + EPS)
            y = y * ng_ref[:, vv]
            z_ref[rows, vv] = (sg_ref[0, rows, vv] * y).astype(BF16)

    o_ref[0] = x_ref[0] + _dot(z_ref[...], w_ref[...])


def _ret_fwd(log_g, q, k, kz, v, sg, ob, x, ng, w_out, ts):
    b, s, d = x.shape
    tile = lambda width: pl.BlockSpec((1, ts, width), lambda bi, i: (bi, i, 0))
    return pl.pallas_call(
        functools.partial(_ret_fwd_kernel, ts=ts),
        grid=(b, s // ts),
        in_specs=[pl.BlockSpec(memory_space=pltpu.SMEM), tile(RET_QK), tile(RET_QK), tile(RET_QK), tile(RET_V),
                  tile(RET_V), tile(RET_V), tile(d), _const_spec((1, RET_V)), _const_spec(w_out.shape)],
        out_specs=tile(d),
        out_shape=jax.ShapeDtypeStruct((b, s, d), F32),
        scratch_shapes=[pltpu.VMEM((RET_HEADS, RET_DK, RET_DV), F32), pltpu.VMEM((ts, RET_V), BF16)],
        compiler_params=_params(BATCH_THEN_TILES),
        name="ret_fwd",
    )(log_g, q, k, kz, v, sg, ob, x, ng, w_out)


def _prepare(norm_mix, norm_ffn, norm_final, lru_w_in, lru_conv_w, lru_conv_b, lru_w_a, lru_b_a,
             lru_w_x, lru_b_x, lru_lambda, lru_w_out, ret_w_in, ret_decay_logit, ret_norm, ret_w_out,
             ffn_w_in, ffn_conv_w, ffn_conv_b, ffn_w_out, seq):
    ffn = []
    for i in range(2):
        w_in = ffn_w_in[i].astype(BF16)
        ffn.append(dict(
            g=norm_ffn[i][None, :],
            wu=w_in[:, :D_FF],
            wv=w_in[:, D_FF:],
            cw=ffn_conv_w[i],
            cb=ffn_conv_b[i][None, :],
            wo=ffn_w_out[i].astype(BF16),
        ))
    half = RET_DK // 2
    theta = ROPE_BASE ** (-jnp.arange(half, dtype=F32) / half)
    ang = jnp.arange(seq, dtype=F32)[:, None] * theta[None, :]
    return dict(
        ffn=ffn,
        norm_mix=[norm_mix[0][None, :], norm_mix[1][None, :]],
        norm_final=norm_final[None, :],
        lru_w_in=lru_w_in[0].astype(BF16),
        lru_cw=lru_conv_w[0],
        lru_cb=lru_conv_b[0][None, :],
        lru_wa=[lru_w_a[0, d].astype(BF16) for d in range(2)],
        lru_ba=[lru_b_a[0, d][None, :] for d in range(2)],
        lru_wx=[lru_w_x[0, d].astype(BF16) for d in range(2)],
        lru_bx=[lru_b_x[0, d][None, :] for d in range(2)],
        lru_lam=[lru_lambda[0, d][None, :] for d in range(2)],
        lru_w_out=lru_w_out[0].astype(BF16),
        ret_w_in=ret_w_in[0].astype(BF16),
        ret_log_g=jax.nn.log_sigmoid(ret_decay_logit[0].astype(F32)).reshape(2 * RET_HEADS),
        ret_norm=ret_norm[0][None, :],
        ret_w_out=ret_w_out[0].astype(BF16),
        cos=jnp.cos(ang),
        sin=jnp.sin(ang),
    )


def _encoder(x, p, ts):
    b, s, d = x.shape
    lru_ts = min(ts, LRU_TILE)
    gate, xc = _lru_in(x, p["norm_mix"][0], p["lru_w_in"], p["lru_cw"], p["lru_cb"], lru_ts)
    h = [_lru_scan(xc, p["lru_wa"][di], p["lru_ba"][di], p["lru_wx"][di], p["lru_bx"][di], p["lru_lam"][di],
                   b, min(SCAN_ROWS, lru_ts * b), reverse=bool(di)) for di in range(2)]
    x = _lru_out(h[0], h[1], gate, x, p["lru_w_out"], lru_ts)
    f = p["ffn"][0]
    x = _ffn(x, f["g"], f["wu"], f["wv"], f["cw"], f["cb"], f["wo"], ts)
    q, k, kz, v, sg, ob = _ret_proj(p["ret_log_g"], x, p["norm_mix"][1], p["ret_w_in"], p["cos"], p["sin"], ts)
    x = _ret_fwd(p["ret_log_g"], q, k, kz, v, sg, ob, x, p["ret_norm"], p["ret_w_out"], ts)
    f = p["ffn"][1]
    return _ffn(x, f["g"], f["wu"], f["wv"], f["cw"], f["cb"], f["wo"], ts, final_g=p["norm_final"])


def kernel(x_prompt, x_sample, norm_mix, norm_ffn, norm_final, lru_w_in, lru_conv_w, lru_conv_b, lru_w_a, lru_b_a, lru_w_x, lru_b_x, lru_lambda, lru_w_out, ret_w_in, ret_decay_logit, ret_norm, ret_w_out, ffn_w_in, ffn_conv_w, ffn_conv_b, ffn_w_out):
    assert x_prompt.shape[1] == x_sample.shape[1] and x_prompt.shape[1] % SEQ_TILE == 0
    assert all(SUBLANES % x.shape[0] == 0 for x in (x_prompt, x_sample))
    p = _prepare(norm_mix, norm_ffn, norm_final, lru_w_in, lru_conv_w, lru_conv_b, lru_w_a, lru_b_a,
                 lru_w_x, lru_b_x, lru_lambda, lru_w_out, ret_w_in, ret_decay_logit, ret_norm,
                 ret_w_out, ffn_w_in, ffn_conv_w, ffn_conv_b, ffn_w_out, x_prompt.shape[1])
    return (_encoder(x_prompt, p, SEQ_TILE), _encoder(x_sample, p, SEQ_TILE))
```

```python
import functools

import jax
import jax.numpy as jnp
from jax import lax
from jax.experimental import pallas as pl
from jax.experimental.pallas import tpu as pltpu

F32 = jnp.float32
BF16 = jnp.bfloat16

EPS = 1e-6
D_MODEL = 1024
LRU_BLOCKS = 4
LRU_BLOCK_W = D_MODEL // LRU_BLOCKS
LRU_C = 8.0
RET_HEADS = 4
RET_DK = 256
RET_DV = 512
RET_QK = RET_HEADS * RET_DK
RET_V = RET_HEADS * RET_DV
RET_CHUNK = 128
ROPE_BASE = 10000.0
D_FF = 2816

SUBLANES = 8
LANES = 128
LANE_SLABS = D_MODEL // LANES
HALO = SUBLANES
SEQ_TILE = 512
LRU_IN_ROWS = 1024
SCAN_ROWS = 512
FF_CHUNK = 256
VMEM_LIMIT_BYTES = 56 * 1024 * 1024


def _params(semantics):
    return pltpu.CompilerParams(dimension_semantics=semantics, vmem_limit_bytes=VMEM_LIMIT_BYTES)


BATCH_THEN_TILES = ("parallel", "arbitrary")


def _const_spec(shape):
    zeros = (0,) * len(shape)
    return pl.BlockSpec(shape, lambda *_: zeros, pipeline_mode=pl.Buffered(1))


def _rms(x, g):
    return x * lax.rsqrt(jnp.mean(x * x, axis=-1, keepdims=True) + EPS) * g


def _gelu(x):
    return jax.nn.gelu(x, approximate=True)


def _dot(a, b):
    return jnp.dot(a, b, preferred_element_type=F32)


def _dot_tn(a, b):
    return lax.dot_general(a, b, (((0,), (0,)), ((), ())), preferred_element_type=F32)


def _dot_nt(a, b):
    return lax.dot_general(a, b, (((1,), (1,)), ((), ())), preferred_element_type=F32)


def _tile_and_halo_specs(s, ts, d, batch_first=True):
    halo_per_tile = ts // HALO
    n_halo = s // HALO

    def spec(rows, tile_to_block):
        if batch_first:
            return pl.BlockSpec((1, rows, d), lambda bi, i: (bi, tile_to_block(i), 0))
        return pl.BlockSpec((1, rows, d), lambda i, bi: (bi, tile_to_block(i), 0))

    tile = spec(ts, lambda i: i)
    prev = spec(HALO, lambda i: jnp.maximum(i * halo_per_tile - 1, 0))
    nxt = spec(HALO, lambda i: jnp.minimum((i + 1) * halo_per_tile, n_halo - 1))
    return tile, prev, nxt


def _store_normed_tile_with_halo(x, xp_ref, xn_ref, g, xs_ref, j, n_tiles, ts):
    xs_ref[0:ts] = _rms(x, g).astype(BF16)
    nxt = jnp.where(j == n_tiles - 1, 0.0, _rms(xn_ref[0], g))
    prv = jnp.where(j == 0, 0.0, _rms(xp_ref[0], g))
    xs_ref[ts:ts + 2 * HALO] = jnp.concatenate([nxt, prv], axis=0).astype(BF16)


def _time_shift(ext, k, ts):
    return pltpu.roll(ext, (-k) % ext.shape[0], 0)[0:ts]


def _batch_tile_specs(batch, s, ts, d, reverse=False):
    n_tiles = s // ts
    halo_per_tile = ts // HALO
    n_halo = s // HALO

    def tile_of(i):
        return (n_tiles - 1 - i) if reverse else i

    tile = pl.BlockSpec((batch, ts, d), lambda i: (0, tile_of(i), 0))
    prev = pl.BlockSpec((batch, HALO, d), lambda i: (0, jnp.maximum(tile_of(i) * halo_per_tile - 1, 0), 0))
    nxt = pl.BlockSpec((batch, HALO, d),
                       lambda i: (0, jnp.minimum((tile_of(i) + 1) * halo_per_tile, n_halo - 1), 0))
    return tile, prev, nxt


def _slab_spec(rows, n_tiles, reverse=False):
    return pl.BlockSpec((LANE_SLABS, rows, LANES), lambda i: (0, (n_tiles - 1 - i) if reverse else i, 0))


def _lru_in_kernel(x_ref, xp_ref, xn_ref, g_ref, w_ref, cw_ref, cb_ref, gate_ref, xc_ref, xs_ref,
                   *, n_tiles, ts, batch):
    w = D_MODEL
    j = pl.program_id(0)
    g = g_ref[...]
    body = batch * ts
    halo = 2 * HALO
    for b in range(batch):
        xs_ref[b * ts:(b + 1) * ts] = _rms(x_ref[b], g).astype(BF16)
        nxt = jnp.where(j == n_tiles - 1, 0.0, _rms(xn_ref[b], g))
        prv = jnp.where(j == 0, 0.0, _rms(xp_ref[b], g))
        xs_ref[body + b * halo:body + (b + 1) * halo] = jnp.concatenate([nxt, prv], axis=0).astype(BF16)

    slabs_per_block = LRU_BLOCK_W // LANES
    for nb in range(LRU_BLOCKS):
        sl = slice(nb * LRU_BLOCK_W, (nb + 1) * LRU_BLOCK_W)
        gate = _gelu(_dot(xs_ref[0:body], w_ref[:, sl]))
        rec = _dot(xs_ref[...], w_ref[:, w + nb * LRU_BLOCK_W:w + (nb + 1) * LRU_BLOCK_W])
        for b in range(batch):
            ext = jnp.concatenate([rec[b * ts:(b + 1) * ts], rec[body + b * halo:body + (b + 1) * halo]], axis=0)
            xc = (_time_shift(ext, -2, ts) * cw_ref[0:1, sl] + _time_shift(ext, -1, ts) * cw_ref[1:2, sl]
                  + ext[0:ts] * cw_ref[2:3, sl] + _time_shift(ext, 1, ts) * cw_ref[3:4, sl] + cb_ref[:, sl])
            rows = pl.ds(b, ts, stride=batch)
            for k in range(slabs_per_block):
                slab = nb * slabs_per_block + k
                lanes = slice(k * LANES, (k + 1) * LANES)
                xc_ref[slab, rows, :] = xc[:, lanes]
                gate_ref[slab, rows, :] = gate[b * ts:(b + 1) * ts, lanes]


def _lru_in(x, g, w_in, cw, cb, ts):
    b, s, d = x.shape
    n_tiles = s // ts
    tile, prev, nxt = _batch_tile_specs(b, s, ts, d)
    slabs = _slab_spec(ts * b, n_tiles)
    out = jax.ShapeDtypeStruct((LANE_SLABS, s * b, LANES), F32)
    return pl.pallas_call(
        functools.partial(_lru_in_kernel, n_tiles=n_tiles, ts=ts, batch=b),
        grid=(n_tiles,),
        in_specs=[tile, prev, nxt, _const_spec((1, d)), _const_spec((d, 2 * d)), _const_spec(cw.shape),
                  _const_spec((1, d))],
        out_specs=[slabs, slabs],
        out_shape=[out, out],
        scratch_shapes=[pltpu.VMEM((b * (ts + 2 * HALO), d), BF16)],
        compiler_params=_params(("arbitrary",)),
        name="lru_in",
    )(x, x, x, g, w_in, cw, cb)


def _lru_scan_kernel(*refs, reverse, rows, batch, fused_out):
    if fused_out:
        (xc_ref, wa_ref, ba_ref, wx_ref, bx_ref, lam_ref, hf_ref, gate_ref, x_ref, wo_ref,
         o_ref, carry_ref, a_ref, u_ref, h_ref, z_ref, res_ref) = refs
    else:
        xc_ref, wa_ref, ba_ref, wx_ref, bx_ref, lam_ref, h_ref, carry_ref, a_ref, u_ref = refs

    @pl.when(pl.program_id(0) == 0)
    def _():
        carry_ref[...] = jnp.zeros_like(carry_ref)

    neg_lam = -lam_ref[...]
    softplus = jnp.maximum(neg_lam, 0.0) + jnp.log1p(jnp.exp(-jnp.abs(neg_lam)))
    scale = (-0.5 * LRU_C) * softplus
    slabs_per_block = LRU_BLOCK_W // LANES
    for nb in range(LRU_BLOCKS):
        sl = slice(nb * LRU_BLOCK_W, (nb + 1) * LRU_BLOCK_W)
        xb = jnp.concatenate([xc_ref[nb * slabs_per_block + k] for k in range(slabs_per_block)], axis=1)
        xb16 = xb.astype(BF16)
        tr = jnp.tanh(0.5 * (_dot(xb16, wa_ref[nb]) + ba_ref[:, sl]))
        ti = jnp.tanh(0.5 * (_dot(xb16, wx_ref[nb]) + bx_ref[:, sl]))
        log_a = scale[:, sl] * (tr + 1.0)
        a = jnp.exp(log_a)
        one_minus_a2 = -jnp.tanh(log_a) * (a * a + 1.0)
        a_ref[:, sl] = a
        u_ref[:, sl] = jnp.sqrt(one_minus_a2) * ((0.5 * xb) * (ti + 1.0))

    groups = rows // SUBLANES
    substeps = SUBLANES // batch
    shift = (SUBLANES - batch) if reverse else batch % SUBLANES
    sub = lax.broadcasted_iota(jnp.int32, (SUBLANES, D_MODEL), 0)

    def step(gi, c):
        g = (groups - 1 - gi) if reverse else gi
        r = pl.ds(pl.multiple_of(g * SUBLANES, SUBLANES), SUBLANES)
        a8 = a_ref[r, :]
        u8 = u_ref[r, :]
        h = a8 * c + u8
        out = h
        for k in range(1, substeps):
            h = a8 * pltpu.roll(h, shift, 0) + u8
            if reverse:
                out = jnp.where(sub < (substeps - k) * batch, h, out)
            else:
                out = jnp.where(sub >= k * batch, h, out)
        for slab in range(LANE_SLABS):
            h_ref[slab, r, :] = out[:, slab * LANES:(slab + 1) * LANES]
        return pltpu.roll(h, shift, 0) if substeps > 1 else h

    carry_ref[...] = lax.fori_loop(0, groups, step, carry_ref[...], unroll=4)

    if fused_out:
        for slab in range(LANE_SLABS):
            lanes = slice(slab * LANES, (slab + 1) * LANES)
            z_ref[:, lanes] = ((hf_ref[slab] + h_ref[slab]) * gate_ref[slab]).astype(BF16)
        res = _dot(z_ref[...], wo_ref[...])
        for slab in range(LANE_SLABS):
            res_ref[slab] = res[:, slab * LANES:(slab + 1) * LANES]
        ts = rows // batch
        for b in range(batch):
            for slab in range(LANE_SLABS):
                lanes = slice(slab * LANES, (slab + 1) * LANES)
                o_ref[b, :, lanes] = x_ref[b, :, lanes] + res_ref[slab, pl.ds(b, ts, stride=batch), :]


def _lru_scan(xc, wa, ba, wx, bx, lam, batch, rows, reverse, fused=None):
    slabs, n, lanes = xc.shape
    w = slabs * lanes
    n_tiles = n // rows
    tile = _slab_spec(rows, n_tiles, reverse)
    row = _const_spec((1, w))
    gate_w = _const_spec((LRU_BLOCKS, LRU_BLOCK_W, LRU_BLOCK_W))
    in_specs = [tile, gate_w, row, gate_w, row, row]
    args = [xc, wa, ba, wx, bx, lam]
    scratch = [pltpu.VMEM((SUBLANES, w), F32), pltpu.VMEM((rows, w), F32), pltpu.VMEM((rows, w), F32)]
    if fused is None:
        out_specs = tile
        out_shape = jax.ShapeDtypeStruct(xc.shape, F32)
    else:
        h_other, gate, x, w_out = fused
        _, s, d = x.shape
        x_tile, _, _ = _batch_tile_specs(batch, s, rows // batch, d, reverse)
        in_specs += [tile, tile, x_tile, _const_spec(w_out.shape)]
        args += [h_other, gate, x, w_out]
        out_specs = x_tile
        out_shape = jax.ShapeDtypeStruct(x.shape, F32)
        scratch += [pltpu.VMEM((slabs, rows, lanes), F32), pltpu.VMEM((rows, w), BF16),
                    pltpu.VMEM((slabs, rows, lanes), F32)]
    kern = functools.partial(_lru_scan_kernel, reverse=reverse, rows=rows, batch=batch,
                             fused_out=fused is not None)
    return pl.pallas_call(
        kern,
        grid=(n_tiles,),
        in_specs=in_specs,
        out_specs=out_specs,
        out_shape=out_shape,
        scratch_shapes=scratch,
        compiler_params=_params(("arbitrary",)),
        name=("lru_scan_bwd_out" if fused is not None else "lru_scan_bwd") if reverse else "lru_scan_fwd",
    )(*args)


def _ffn_kernel(*refs, n_tiles, ts, final):
    if final:
        (x_ref, xp_ref, xn_ref, g_ref, wu_ref, wv_ref, cw_ref, cb_ref, wo_ref, gf_ref,
         o_ref, xs_ref, act_ref) = refs
    else:
        (x_ref, xp_ref, xn_ref, g_ref, wu_ref, wv_ref, cw_ref, cb_ref, wo_ref,
         o_ref, xs_ref, act_ref) = refs
    x = x_ref[0]
    _store_normed_tile_with_halo(x, xp_ref, xn_ref, g_ref[...], xs_ref, pl.program_id(1), n_tiles, ts)
    for c in range(D_FF // FF_CHUNK):
        cols = slice(c * FF_CHUNK, (c + 1) * FF_CHUNK)
        u = _dot(xs_ref[...], wu_ref[:, cols])
        v = _dot(xs_ref[0:ts], wv_ref[:, cols])
        y = (_time_shift(u, -1, ts) * cw_ref[0:1, cols] + u[0:ts] * cw_ref[1:2, cols]
             + _time_shift(u, 1, ts) * cw_ref[2:3, cols] + cb_ref[:, cols])
        act_ref[:, cols] = (_gelu(y) * v).astype(BF16)

    out = x + _dot(act_ref[...], wo_ref[...])
    if final:
        out = _rms(out, gf_ref[...])
    o_ref[0] = out


def _ffn(x, g, wu, wv, cw, cb, wo, ts, final_g=None):
    b, s, d = x.shape
    n_tiles = s // ts
    tile, prev, nxt = _tile_and_halo_specs(s, ts, d)
    final = final_g is not None
    in_specs = [tile, prev, nxt, _const_spec((1, d)), _const_spec(wu.shape), _const_spec(wv.shape),
                _const_spec(cw.shape), _const_spec(cb.shape), _const_spec(wo.shape)]
    args = [x, x, x, g, wu, wv, cw, cb, wo]
    if final:
        in_specs.append(_const_spec((1, d)))
        args.append(final_g)
    kern = functools.partial(_ffn_kernel, n_tiles=n_tiles, ts=ts, final=final)
    return pl.pallas_call(
        kern,
        grid=(b, n_tiles),
        in_specs=in_specs,
        out_specs=tile,
        out_shape=jax.ShapeDtypeStruct((b, s, d), F32),
        scratch_shapes=[pltpu.VMEM((ts + 2 * HALO, d), BF16), pltpu.VMEM((ts, D_FF), BF16)],
        compiler_params=_params(BATCH_THEN_TILES),
        name="ffn_final" if final else "ffn",
    )(*args)


def _chunk_pos(rows):
    return (lax.broadcasted_iota(jnp.int32, (rows, 1), 0) % RET_CHUNK).astype(F32)


def _ret_proj_kernel(lg_ref, x_ref, g_ref, w_ref, cos_ref, sin_ref,
                     q_ref, k_ref, kz_ref, v_ref, sg_ref, ob_ref, xs_ref, kzb_ref, s_ref, *, ts):
    @pl.when(pl.program_id(1) == 0)
    def _():
        s_ref[...] = jnp.zeros_like(s_ref)

    xs_ref[...] = _rms(x_ref[0], g_ref[...]).astype(BF16)
    cos = cos_ref[...]
    sin = sin_ref[...]
    half = RET_DK // 2
    pos = _chunk_pos(ts)

    def rotary(t):
        t1 = t[:, :half]
        t2 = t[:, half:]
        return t1 * cos - t2 * sin, t2 * cos + t1 * sin

    for h in range(RET_HEADS):
        lo = h * RET_DK
        mid = lo + half
        hi = lo + RET_DK
        q1, q2 = rotary(_dot(xs_ref[...], w_ref[:, lo:hi]))
        q_ref[0, :, lo:mid] = q1.astype(BF16)
        q_ref[0, :, mid:hi] = q2.astype(BF16)
        k1, k2 = rotary(_dot(xs_ref[...], w_ref[:, RET_QK + lo:RET_QK + hi]) * (RET_DK ** -0.5))
        k_ref[0, :, lo:mid] = k1.astype(BF16)
        k_ref[0, :, mid:hi] = k2.astype(BF16)
        zeta_f = jnp.exp(lg_ref[h] * (RET_CHUNK - 1.0 - pos))
        kz_ref[0, :, lo:mid] = (k1 * zeta_f).astype(BF16)
        kz_ref[0, :, mid:hi] = (k2 * zeta_f).astype(BF16)
        zeta_b = jnp.exp(lg_ref[RET_HEADS + h] * pos)
        kzb_ref[:, lo:mid] = (k1 * zeta_b).astype(BF16)
        kzb_ref[:, mid:hi] = (k2 * zeta_b).astype(BF16)
    for h in range(RET_HEADS):
        vv = slice(h * RET_DV, (h + 1) * RET_DV)
        lo = 2 * RET_QK + h * RET_DV
        v_ref[0, :, vv] = _dot(xs_ref[...], w_ref[:, lo:lo + RET_DV]).astype(BF16)
        lo = 2 * RET_QK + RET_V + h * RET_DV
        sg_ref[0, :, vv] = jax.nn.silu(_dot(xs_ref[...], w_ref[:, lo:lo + RET_DV]))

    cpos = _chunk_pos(RET_CHUNK)
    for c in reversed(range(ts // RET_CHUNK)):
        rows = slice(c * RET_CHUNK, (c + 1) * RET_CHUNK)
        for h in range(RET_HEADS):
            lg = lg_ref[RET_HEADS + h]
            xi = jnp.exp(lg * (RET_CHUNK - cpos))
            g_chunk = jnp.exp(jnp.full((1, 1), lg * RET_CHUNK, F32))
            qk = slice(h * RET_DK, (h + 1) * RET_DK)
            vv = slice(h * RET_DV, (h + 1) * RET_DV)
            state = s_ref[h]
            ob_ref[0, rows, vv] = _dot(q_ref[0, rows, qk], state.astype(BF16)) * xi
            s_ref[h] = state * g_chunk + _dot_tn(kzb_ref[rows, qk], v_ref[0, rows, vv])


def _ret_proj(log_g, x, g, w_in, cos, sin, ts):
    b, s, d = x.shape
    n_tiles = s // ts
    tile = lambda width: pl.BlockSpec((1, ts, width), lambda bi, i: (bi, n_tiles - 1 - i, 0))
    rope = pl.BlockSpec((ts, RET_DK // 2), lambda bi, i: (n_tiles - 1 - i, 0))
    act = lambda width, dtype: jax.ShapeDtypeStruct((b, s, width), dtype)
    return pl.pallas_call(
        functools.partial(_ret_proj_kernel, ts=ts),
        grid=(b, n_tiles),
        in_specs=[pl.BlockSpec(memory_space=pltpu.SMEM), tile(d), _const_spec((1, d)), _const_spec(w_in.shape),
                  rope, rope],
        out_specs=[tile(RET_QK), tile(RET_QK), tile(RET_QK), tile(RET_V), tile(RET_V), tile(RET_V)],
        out_shape=[act(RET_QK, BF16), act(RET_QK, BF16), act(RET_QK, BF16), act(RET_V, BF16),
                   act(RET_V, F32), act(RET_V, F32)],
        scratch_shapes=[pltpu.VMEM((ts, d), BF16), pltpu.VMEM((ts, RET_QK), BF16),
                        pltpu.VMEM((RET_HEADS, RET_DK, RET_DV), F32)],
        compiler_params=_params(BATCH_THEN_TILES),
        name="ret_proj",
    )(log_g, x, g, w_in, cos, sin)


def _ret_fwd_kernel(lg_ref, q_ref, k_ref, kz_ref, v_ref, sg_ref, ob_ref, x_ref, ng_ref, w_ref, o_ref,
                    s_ref, z_ref, *, ts):
    @pl.when(pl.program_id(1) == 0)
    def _():
        s_ref[...] = jnp.zeros_like(s_ref)

    cpos = _chunk_pos(RET_CHUNK)
    n_idx = lax.broadcasted_iota(jnp.int32, (RET_CHUNK, RET_CHUNK), 0)
    m_idx = lax.broadcasted_iota(jnp.int32, (RET_CHUNK, RET_CHUNK), 1)
    diff = (n_idx - m_idx).astype(F32)
    for h in range(RET_HEADS):
        lf = lg_ref[h]
        lb = lg_ref[RET_HEADS + h]
        decay = jnp.where(diff >= 0.0, jnp.exp(lf * jnp.maximum(diff, 0.0)),
                          jnp.exp(lb * jnp.maximum(-diff, 0.0)))
        xi = jnp.exp(lf * (cpos + 1.0))
        g_chunk = jnp.exp(jnp.full((1, 1), lf * RET_CHUNK, F32))
        qk = slice(h * RET_DK, (h + 1) * RET_DK)
        vv = slice(h * RET_DV, (h + 1) * RET_DV)
        for c in range(ts // RET_CHUNK):
            rows = slice(c * RET_CHUNK, (c + 1) * RET_CHUNK)
            qc = q_ref[0, rows, qk]
            vc = v_ref[0, rows, vv]
            state = s_ref[h]
            scores = _dot_nt(qc, k_ref[0, rows, qk]) * decay
            y = (_dot(scores.astype(BF16), vc) + _dot(qc, state.astype(BF16)) * xi) + ob_ref[0, rows, vv]
            s_ref[h] = state * g_chunk + _dot_tn(kz_ref[0, rows, qk], vc)
            y = y * lax.rsqrt(jnp.mean(y * y, axis=-1, keepdims=True) + EPS)
            y = y * ng_ref[:, vv]
            z_ref[rows, vv] = (sg_ref[0, rows, vv] * y).astype(BF16)

    o_ref[0] = x_ref[0] + _dot(z_ref[...], w_ref[...])


def _ret_fwd(log_g, q, k, kz, v, sg, ob, x, ng, w_out, ts):
    b, s, d = x.shape
    tile = lambda width: pl.BlockSpec((1, ts, width), lambda bi, i: (bi, i, 0))
    return pl.pallas_call(
        functools.partial(_ret_fwd_kernel, ts=ts),
        grid=(b, s // ts),
        in_specs=[pl.BlockSpec(memory_space=pltpu.SMEM), tile(RET_QK), tile(RET_QK), tile(RET_QK), tile(RET_V),
                  tile(RET_V), tile(RET_V), tile(d), _const_spec((1, RET_V)), _const_spec(w_out.shape)],
        out_specs=tile(d),
        out_shape=jax.ShapeDtypeStruct((b, s, d), F32),
        scratch_shapes=[pltpu.VMEM((RET_HEADS, RET_DK, RET_DV), F32), pltpu.VMEM((ts, RET_V), BF16)],
        compiler_params=_params(BATCH_THEN_TILES),
        name="ret_fwd",
    )(log_g, q, k, kz, v, sg, ob, x, ng, w_out)


def _prepare(norm_mix, norm_ffn, norm_final, lru_w_in, lru_conv_w, lru_conv_b, lru_w_a, lru_b_a,
             lru_w_x, lru_b_x, lru_lambda, lru_w_out, ret_w_in, ret_decay_logit, ret_norm, ret_w_out,
             ffn_w_in, ffn_conv_w, ffn_conv_b, ffn_w_out, seq):
    ffn = []
    for i in range(2):
        w_in = ffn_w_in[i].astype(BF16)
        ffn.append(dict(
            g=norm_ffn[i][None, :],
            wu=w_in[:, :D_FF],
            wv=w_in[:, D_FF:],
            cw=ffn_conv_w[i],
            cb=ffn_conv_b[i][None, :],
            wo=ffn_w_out[i].astype(BF16),
        ))
    half = RET_DK // 2
    theta = ROPE_BASE ** (-jnp.arange(half, dtype=F32) / half)
    ang = jnp.arange(seq, dtype=F32)[:, None] * theta[None, :]
    return dict(
        ffn=ffn,
        norm_mix=[norm_mix[0][None, :], norm_mix[1][None, :]],
        norm_final=norm_final[None, :],
        lru_w_in=lru_w_in[0].astype(BF16),
        lru_cw=lru_conv_w[0],
        lru_cb=lru_conv_b[0][None, :],
        lru_wa=[lru_w_a[0, d].astype(BF16) for d in range(2)],
        lru_ba=[lru_b_a[0, d][None, :] for d in range(2)],
        lru_wx=[lru_w_x[0, d].astype(BF16) for d in range(2)],
        lru_bx=[lru_b_x[0, d][None, :] for d in range(2)],
        lru_lam=[lru_lambda[0, d][None, :] for d in range(2)],
        lru_w_out=lru_w_out[0].astype(BF16),
        ret_w_in=ret_w_in[0].astype(BF16),
        ret_log_g=jax.nn.log_sigmoid(ret_decay_logit[0].astype(F32)).reshape(2 * RET_HEADS),
        ret_norm=ret_norm[0][None, :],
        ret_w_out=ret_w_out[0].astype(BF16),
        cos=jnp.cos(ang),
        sin=jnp.sin(ang),
    )


def _encoder(x, p, ts):
    b, s, d = x.shape
    in_ts = min(ts, LRU_IN_ROWS // b)
    scan_rows = min(SCAN_ROWS, ts * b)
    gate, xc = _lru_in(x, p["norm_mix"][0], p["lru_w_in"], p["lru_cw"], p["lru_cb"], in_ts)
    scan = lambda di, fused: _lru_scan(xc, p["lru_wa"][di], p["lru_ba"][di], p["lru_wx"][di], p["lru_bx"][di],
                                       p["lru_lam"][di], b, scan_rows, reverse=bool(di), fused=fused)
    hf = scan(0, None)
    x = scan(1, (hf, gate, x, p["lru_w_out"]))
    f = p["ffn"][0]
    x = _ffn(x, f["g"], f["wu"], f["wv"], f["cw"], f["cb"], f["wo"], ts)
    q, k, kz, v, sg, ob = _ret_proj(p["ret_log_g"], x, p["norm_mix"][1], p["ret_w_in"], p["cos"], p["sin"], ts)
    x = _ret_fwd(p["ret_log_g"], q, k, kz, v, sg, ob, x, p["ret_norm"], p["ret_w_out"], ts)
    f = p["ffn"][1]
    return _ffn(x, f["g"], f["wu"], f["wv"], f["cw"], f["cb"], f["wo"], ts, final_g=p["norm_final"])


def kernel(x_prompt, x_sample, norm_mix, norm_ffn, norm_final, lru_w_in, lru_conv_w, lru_conv_b, lru_w_a, lru_b_a, lru_w_x, lru_b_x, lru_lambda, lru_w_out, ret_w_in, ret_decay_logit, ret_norm, ret_w_out, ffn_w_in, ffn_conv_w, ffn_conv_b, ffn_w_out):
    assert x_prompt.shape[1] == x_sample.shape[1] and x_prompt.shape[1] % SEQ_TILE == 0
    assert all(SUBLANES % x.shape[0] == 0 for x in (x_prompt, x_sample))
    p = _prepare(norm_mix, norm_ffn, norm_final, lru_w_in, lru_conv_w, lru_conv_b, lru_w_a, lru_b_a,
                 lru_w_x, lru_b_x, lru_lambda, lru_w_out, ret_w_in, ret_decay_logit, ret_norm,
                 ret_w_out, ffn_w_in, ffn_conv_w, ffn_conv_b, ffn_w_out, x_prompt.shape[1])
    return (_encoder(x_prompt, p, SEQ_TILE), _encoder(x_sample, p, SEQ_TILE))
```

```python
import functools

import jax
import jax.numpy as jnp
from jax import lax
from jax.experimental import pallas as pl
from jax.experimental.pallas import tpu as pltpu

F32 = jnp.float32
BF16 = jnp.bfloat16

EPS = 1e-6
D_MODEL = 1024
LRU_BLOCKS = 4
LRU_BLOCK_W = D_MODEL // LRU_BLOCKS
LRU_C = 8.0
RET_HEADS = 4
RET_DK = 256
RET_DV = 512
RET_QK = RET_HEADS * RET_DK
RET_V = RET_HEADS * RET_DV
RET_CHUNK = 256
ROPE_BASE = 10000.0
LOG2_E = 1.4426950408889634
D_FF = 2816

SUBLANES = 8
LANES = 128
LANE_SLABS = D_MODEL // LANES
HALO = SUBLANES
SEQ_TILE = 512
LRU_IN_ROWS = 1024
SCAN_ROWS = 512
FF_CHUNK = 256
VMEM_LIMIT_BYTES = 56 * 1024 * 1024


def _params(semantics):
    return pltpu.CompilerParams(dimension_semantics=semantics, vmem_limit_bytes=VMEM_LIMIT_BYTES)


BATCH_THEN_TILES = ("parallel", "arbitrary")


def _const_spec(shape):
    zeros = (0,) * len(shape)
    return pl.BlockSpec(shape, lambda *_: zeros, pipeline_mode=pl.Buffered(1))


def _rms(x, g):
    return x * lax.rsqrt(jnp.mean(x * x, axis=-1, keepdims=True) + EPS) * g


def _gelu(x):
    return jax.nn.gelu(x, approximate=True)


def _sqrt_nonneg(x):
    return jnp.where(x > 0.0, x * lax.rsqrt(x), 0.0)


def _dot(a, b):
    return jnp.dot(a, b, preferred_element_type=F32)


def _dot_tn(a, b):
    return lax.dot_general(a, b, (((0,), (0,)), ((), ())), preferred_element_type=F32)


def _dot_nt(a, b):
    return lax.dot_general(a, b, (((1,), (1,)), ((), ())), preferred_element_type=F32)


def _tile_and_halo_specs(s, ts, d, batch_first=True):
    halo_per_tile = ts // HALO
    n_halo = s // HALO

    def spec(rows, tile_to_block):
        if batch_first:
            return pl.BlockSpec((1, rows, d), lambda bi, i: (bi, tile_to_block(i), 0))
        return pl.BlockSpec((1, rows, d), lambda i, bi: (bi, tile_to_block(i), 0))

    tile = spec(ts, lambda i: i)
    prev = spec(HALO, lambda i: jnp.maximum(i * halo_per_tile - 1, 0))
    nxt = spec(HALO, lambda i: jnp.minimum((i + 1) * halo_per_tile, n_halo - 1))
    return tile, prev, nxt


def _store_normed_tile_with_halo(x, xp_ref, xn_ref, g, xs_ref, j, n_tiles, ts):
    xs_ref[0:ts] = _rms(x, g).astype(BF16)
    nxt = jnp.where(j == n_tiles - 1, 0.0, _rms(xn_ref[0], g))
    prv = jnp.where(j == 0, 0.0, _rms(xp_ref[0], g))
    xs_ref[ts:ts + 2 * HALO] = jnp.concatenate([nxt, prv], axis=0).astype(BF16)


def _time_shift(ext, k, ts):
    return pltpu.roll(ext, (-k) % ext.shape[0], 0)[0:ts]


def _batch_tile_specs(batch, s, ts, d, reverse=False):
    n_tiles = s // ts
    halo_per_tile = ts // HALO
    n_halo = s // HALO

    def tile_of(i):
        return (n_tiles - 1 - i) if reverse else i

    tile = pl.BlockSpec((batch, ts, d), lambda i: (0, tile_of(i), 0))
    prev = pl.BlockSpec((batch, HALO, d), lambda i: (0, jnp.maximum(tile_of(i) * halo_per_tile - 1, 0), 0))
    nxt = pl.BlockSpec((batch, HALO, d),
                       lambda i: (0, jnp.minimum((tile_of(i) + 1) * halo_per_tile, n_halo - 1), 0))
    return tile, prev, nxt


def _slab_spec(rows, n_tiles, reverse=False):
    return pl.BlockSpec((LANE_SLABS, rows, LANES), lambda i: (0, (n_tiles - 1 - i) if reverse else i, 0))


def _lru_in_kernel(x_ref, xp_ref, xn_ref, g_ref, w_ref, cw_ref, cb_ref, gate_ref, xc_ref, xs_ref,
                   *, n_tiles, ts, batch):
    w = D_MODEL
    j = pl.program_id(0)
    g = g_ref[...]
    body = batch * ts
    halo = 2 * HALO
    for b in range(batch):
        xs_ref[b * ts:(b + 1) * ts] = _rms(x_ref[b], g).astype(BF16)
        nxt = jnp.where(j == n_tiles - 1, 0.0, _rms(xn_ref[b], g))
        prv = jnp.where(j == 0, 0.0, _rms(xp_ref[b], g))
        xs_ref[body + b * halo:body + (b + 1) * halo] = jnp.concatenate([nxt, prv], axis=0).astype(BF16)

    slabs_per_block = LRU_BLOCK_W // LANES
    for nb in range(LRU_BLOCKS):
        sl = slice(nb * LRU_BLOCK_W, (nb + 1) * LRU_BLOCK_W)
        gate = _gelu(_dot(xs_ref[0:body], w_ref[:, sl]))
        rec = _dot(xs_ref[...], w_ref[:, w + nb * LRU_BLOCK_W:w + (nb + 1) * LRU_BLOCK_W])
        for b in range(batch):
            ext = jnp.concatenate([rec[b * ts:(b + 1) * ts], rec[body + b * halo:body + (b + 1) * halo]], axis=0)
            xc = (_time_shift(ext, -2, ts) * cw_ref[0:1, sl] + _time_shift(ext, -1, ts) * cw_ref[1:2, sl]
                  + ext[0:ts] * cw_ref[2:3, sl] + _time_shift(ext, 1, ts) * cw_ref[3:4, sl] + cb_ref[:, sl])
            rows = pl.ds(b, ts, stride=batch)
            for k in range(slabs_per_block):
                slab = nb * slabs_per_block + k
                lanes = slice(k * LANES, (k + 1) * LANES)
                xc_ref[slab, rows, :] = xc[:, lanes]
                gate_ref[slab, rows, :] = gate[b * ts:(b + 1) * ts, lanes]


def _lru_in(x, g, w_in, cw, cb, ts):
    b, s, d = x.shape
    n_tiles = s // ts
    tile, prev, nxt = _batch_tile_specs(b, s, ts, d)
    slabs = _slab_spec(ts * b, n_tiles)
    out = jax.ShapeDtypeStruct((LANE_SLABS, s * b, LANES), F32)
    return pl.pallas_call(
        functools.partial(_lru_in_kernel, n_tiles=n_tiles, ts=ts, batch=b),
        grid=(n_tiles,),
        in_specs=[tile, prev, nxt, _const_spec((1, d)), _const_spec((d, 2 * d)), _const_spec(cw.shape),
                  _const_spec((1, d))],
        out_specs=[slabs, slabs],
        out_shape=[out, out],
        scratch_shapes=[pltpu.VMEM((b * (ts + 2 * HALO), d), BF16)],
        compiler_params=_params(("arbitrary",)),
        name="lru_in",
    )(x, x, x, g, w_in, cw, cb)


def _lru_scan_kernel(*refs, reverse, rows, batch, fused_out):
    if fused_out:
        (xc_ref, wa_ref, ba_ref, wx_ref, bx_ref, lam_ref, hf_ref, gate_ref, x_ref, wo_ref,
         o_ref, carry_ref, a_ref, u_ref, h_ref, z_ref, res_ref) = refs
    else:
        xc_ref, wa_ref, ba_ref, wx_ref, bx_ref, lam_ref, h_ref, carry_ref, a_ref, u_ref = refs

    @pl.when(pl.program_id(0) == 0)
    def _():
        carry_ref[...] = jnp.zeros_like(carry_ref)

    neg_lam = -lam_ref[...]
    softplus = jnp.maximum(neg_lam, 0.0) + jnp.log1p(jnp.exp(-jnp.abs(neg_lam)))
    decay = (0.5 * LRU_C) * softplus
    slabs_per_block = LRU_BLOCK_W // LANES
    for nb in range(LRU_BLOCKS):
        sl = slice(nb * LRU_BLOCK_W, (nb + 1) * LRU_BLOCK_W)
        xh = jnp.concatenate([xc_ref[nb * slabs_per_block + k] for k in range(slabs_per_block)], axis=1)
        xh16 = xh.astype(BF16)
        tr = jnp.tanh(_dot(xh16, wa_ref[nb]) + ba_ref[:, sl])
        ti = jnp.tanh(_dot(xh16, wx_ref[nb]) + bx_ref[:, sl])
        neg_log_a = decay[:, sl] * (tr + 1.0)
        a = jnp.exp2(neg_log_a * (-LOG2_E))
        one_minus_a2 = jnp.tanh(neg_log_a) * (a * a + 1.0)
        a_ref[:, sl] = a
        u_ref[:, sl] = _sqrt_nonneg(one_minus_a2) * (xh * (ti + 1.0))

    groups = rows // SUBLANES
    substeps = SUBLANES // batch
    shift = (SUBLANES - batch) if reverse else batch % SUBLANES
    sub = lax.broadcasted_iota(jnp.int32, (SUBLANES, D_MODEL), 0)

    def step(gi, c):
        g = (groups - 1 - gi) if reverse else gi
        r = pl.ds(pl.multiple_of(g * SUBLANES, SUBLANES), SUBLANES)
        a8 = a_ref[r, :]
        u8 = u_ref[r, :]
        h = a8 * c + u8
        out = h
        for k in range(1, substeps):
            h = a8 * pltpu.roll(h, shift, 0) + u8
            if reverse:
                out = jnp.where(sub < (substeps - k) * batch, h, out)
            else:
                out = jnp.where(sub >= k * batch, h, out)
        for slab in range(LANE_SLABS):
            h_ref[slab, r, :] = out[:, slab * LANES:(slab + 1) * LANES]
        return pltpu.roll(h, shift, 0) if substeps > 1 else h

    carry_ref[...] = lax.fori_loop(0, groups, step, carry_ref[...], unroll=4)

    if fused_out:
        for slab in range(LANE_SLABS):
            lanes = slice(slab * LANES, (slab + 1) * LANES)
            z_ref[:, lanes] = ((hf_ref[slab] + h_ref[slab]) * gate_ref[slab]).astype(BF16)
        res = _dot(z_ref[...], wo_ref[...])
        for slab in range(LANE_SLABS):
            res_ref[slab] = res[:, slab * LANES:(slab + 1) * LANES]
        ts = rows // batch
        for b in range(batch):
            for slab in range(LANE_SLABS):
                lanes = slice(slab * LANES, (slab + 1) * LANES)
                o_ref[b, :, lanes] = x_ref[b, :, lanes] + res_ref[slab, pl.ds(b, ts, stride=batch), :]


def _lru_scan(xc, wa, ba, wx, bx, lam, batch, rows, reverse, fused=None):
    slabs, n, lanes = xc.shape
    w = slabs * lanes
    n_tiles = n // rows
    tile = _slab_spec(rows, n_tiles, reverse)
    row = _const_spec((1, w))
    gate_w = _const_spec((LRU_BLOCKS, LRU_BLOCK_W, LRU_BLOCK_W))
    in_specs = [tile, gate_w, row, gate_w, row, row]
    args = [xc, wa, ba, wx, bx, lam]
    scratch = [pltpu.VMEM((SUBLANES, w), F32), pltpu.VMEM((rows, w), F32), pltpu.VMEM((rows, w), F32)]
    if fused is None:
        out_specs = tile
        out_shape = jax.ShapeDtypeStruct(xc.shape, F32)
    else:
        h_other, gate, x, w_out = fused
        _, s, d = x.shape
        x_tile, _, _ = _batch_tile_specs(batch, s, rows // batch, d, reverse)
        in_specs += [tile, tile, x_tile, _const_spec(w_out.shape)]
        args += [h_other, gate, x, w_out]
        out_specs = x_tile
        out_shape = jax.ShapeDtypeStruct(x.shape, F32)
        scratch += [pltpu.VMEM((slabs, rows, lanes), F32), pltpu.VMEM((rows, w), BF16),
                    pltpu.VMEM((slabs, rows, lanes), F32)]
    kern = functools.partial(_lru_scan_kernel, reverse=reverse, rows=rows, batch=batch,
                             fused_out=fused is not None)
    return pl.pallas_call(
        kern,
        grid=(n_tiles,),
        in_specs=in_specs,
        out_specs=out_specs,
        out_shape=out_shape,
        scratch_shapes=scratch,
        compiler_params=_params(("arbitrary",)),
        name=("lru_scan_bwd_out" if fused is not None else "lru_scan_bwd") if reverse else "lru_scan_fwd",
    )(*args)


def _ffn_kernel(*refs, n_tiles, ts, final):
    if final:
        (x_ref, xp_ref, xn_ref, g_ref, wu_ref, wv_ref, cw_ref, cb_ref, wo_ref, gf_ref,
         o_ref, xs_ref, act_ref) = refs
    else:
        (x_ref, xp_ref, xn_ref, g_ref, wu_ref, wv_ref, cw_ref, cb_ref, wo_ref,
         o_ref, xs_ref, act_ref) = refs
    x = x_ref[0]
    _store_normed_tile_with_halo(x, xp_ref, xn_ref, g_ref[...], xs_ref, pl.program_id(1), n_tiles, ts)
    for c in range(D_FF // FF_CHUNK):
        cols = slice(c * FF_CHUNK, (c + 1) * FF_CHUNK)
        u = _dot(xs_ref[...], wu_ref[:, cols])
        v = _dot(xs_ref[0:ts], wv_ref[:, cols])
        y = (_time_shift(u, -1, ts) * cw_ref[0:1, cols] + u[0:ts] * cw_ref[1:2, cols]
             + _time_shift(u, 1, ts) * cw_ref[2:3, cols] + cb_ref[:, cols])
        act_ref[:, cols] = (_gelu(y) * v).astype(BF16)

    out = x + _dot(act_ref[...], wo_ref[...])
    if final:
        out = _rms(out, gf_ref[...])
    o_ref[0] = out


def _ffn(x, g, wu, wv, cw, cb, wo, ts, final_g=None):
    b, s, d = x.shape
    n_tiles = s // ts
    tile, prev, nxt = _tile_and_halo_specs(s, ts, d)
    final = final_g is not None
    in_specs = [tile, prev, nxt, _const_spec((1, d)), _const_spec(wu.shape), _const_spec(wv.shape),
                _const_spec(cw.shape), _const_spec(cb.shape), _const_spec(wo.shape)]
    args = [x, x, x, g, wu, wv, cw, cb, wo]
    if final:
        in_specs.append(_const_spec((1, d)))
        args.append(final_g)
    kern = functools.partial(_ffn_kernel, n_tiles=n_tiles, ts=ts, final=final)
    return pl.pallas_call(
        kern,
        grid=(b, n_tiles),
        in_specs=in_specs,
        out_specs=tile,
        out_shape=jax.ShapeDtypeStruct((b, s, d), F32),
        scratch_shapes=[pltpu.VMEM((ts + 2 * HALO, d), BF16), pltpu.VMEM((ts, D_FF), BF16)],
        compiler_params=_params(BATCH_THEN_TILES),
        name="ffn_final" if final else "ffn",
    )(*args)


def _chunk_pos(rows):
    return (lax.broadcasted_iota(jnp.int32, (rows, 1), 0) % RET_CHUNK).astype(F32)


def _ret_proj_kernel(lg_ref, x_ref, g_ref, w_ref, cos_ref, sin_ref,
                     q_ref, k_ref, kz_ref, v_ref, sg_ref, ob_ref, xs_ref, kzb_ref, s_ref, *, ts):
    @pl.when(pl.program_id(1) == 0)
    def _():
        s_ref[...] = jnp.zeros_like(s_ref)

    xs_ref[...] = _rms(x_ref[0], g_ref[...]).astype(BF16)
    cos = cos_ref[...]
    sin = sin_ref[...]
    half = RET_DK // 2
    pos = _chunk_pos(ts)

    def rotary(t):
        t1 = t[:, :half]
        t2 = t[:, half:]
        return t1 * cos - t2 * sin, t2 * cos + t1 * sin

    for h in range(RET_HEADS):
        lo = h * RET_DK
        mid = lo + half
        hi = lo + RET_DK
        q1, q2 = rotary(_dot(xs_ref[...], w_ref[:, lo:hi]))
        q_ref[0, :, lo:mid] = q1.astype(BF16)
        q_ref[0, :, mid:hi] = q2.astype(BF16)
        k1, k2 = rotary(_dot(xs_ref[...], w_ref[:, RET_QK + lo:RET_QK + hi]) * (RET_DK ** -0.5))
        k_ref[0, :, lo:mid] = k1.astype(BF16)
        k_ref[0, :, mid:hi] = k2.astype(BF16)
        zeta_f = jnp.exp(lg_ref[h] * (RET_CHUNK - 1.0 - pos))
        kz_ref[0, :, lo:mid] = (k1 * zeta_f).astype(BF16)
        kz_ref[0, :, mid:hi] = (k2 * zeta_f).astype(BF16)
        zeta_b = jnp.exp(lg_ref[RET_HEADS + h] * pos)
        kzb_ref[:, lo:mid] = (k1 * zeta_b).astype(BF16)
        kzb_ref[:, mid:hi] = (k2 * zeta_b).astype(BF16)
    for h in range(RET_HEADS):
        vv = slice(h * RET_DV, (h + 1) * RET_DV)
        lo = 2 * RET_QK + h * RET_DV
        v_ref[0, :, vv] = _dot(xs_ref[...], w_ref[:, lo:lo + RET_DV]).astype(BF16)
        lo = 2 * RET_QK + RET_V + h * RET_DV
        sg_ref[0, :, vv] = jax.nn.silu(_dot(xs_ref[...], w_ref[:, lo:lo + RET_DV]))

    cpos = _chunk_pos(RET_CHUNK)
    for c in reversed(range(ts // RET_CHUNK)):
        rows = slice(c * RET_CHUNK, (c + 1) * RET_CHUNK)
        for h in range(RET_HEADS):
            lg = lg_ref[RET_HEADS + h]
            xi = jnp.exp(lg * (RET_CHUNK - cpos))
            g_chunk = jnp.exp(jnp.full((1, 1), lg * RET_CHUNK, F32))
            qk = slice(h * RET_DK, (h + 1) * RET_DK)
            vv = slice(h * RET_DV, (h + 1) * RET_DV)
            state = s_ref[h]
            ob_ref[0, rows, vv] = _dot(q_ref[0, rows, qk], state.astype(BF16)) * xi
            s_ref[h] = state * g_chunk + _dot_tn(kzb_ref[rows, qk], v_ref[0, rows, vv])


def _ret_proj(log_g, x, g, w_in, cos, sin, ts):
    b, s, d = x.shape
    n_tiles = s // ts
    tile = lambda width: pl.BlockSpec((1, ts, width), lambda bi, i: (bi, n_tiles - 1 - i, 0))
    rope = pl.BlockSpec((ts, RET_DK // 2), lambda bi, i: (n_tiles - 1 - i, 0))
    act = lambda width, dtype: jax.ShapeDtypeStruct((b, s, width), dtype)
    return pl.pallas_call(
        functools.partial(_ret_proj_kernel, ts=ts),
        grid=(b, n_tiles),
        in_specs=[pl.BlockSpec(memory_space=pltpu.SMEM), tile(d), _const_spec((1, d)), _const_spec(w_in.shape),
                  rope, rope],
        out_specs=[tile(RET_QK), tile(RET_QK), tile(RET_QK), tile(RET_V), tile(RET_V), tile(RET_V)],
        out_shape=[act(RET_QK, BF16), act(RET_QK, BF16), act(RET_QK, BF16), act(RET_V, BF16),
                   act(RET_V, F32), act(RET_V, F32)],
        scratch_shapes=[pltpu.VMEM((ts, d), BF16), pltpu.VMEM((ts, RET_QK), BF16),
                        pltpu.VMEM((RET_HEADS, RET_DK, RET_DV), F32)],
        compiler_params=_params(BATCH_THEN_TILES),
        name="ret_proj",
    )(log_g, x, g, w_in, cos, sin)


def _ret_fwd_kernel(lg_ref, q_ref, k_ref, kz_ref, v_ref, sg_ref, ob_ref, x_ref, ng_ref, w_ref, o_ref,
                    s_ref, z_ref, *, ts):
    @pl.when(pl.program_id(1) == 0)
    def _():
        s_ref[...] = jnp.zeros_like(s_ref)

    cpos = _chunk_pos(RET_CHUNK)
    n_idx = lax.broadcasted_iota(jnp.int32, (RET_CHUNK, RET_CHUNK), 0)
    m_idx = lax.broadcasted_iota(jnp.int32, (RET_CHUNK, RET_CHUNK), 1)
    diff = (n_idx - m_idx).astype(F32)
    for h in range(RET_HEADS):
        lf = lg_ref[h]
        lb = lg_ref[RET_HEADS + h]
        decay = jnp.where(diff >= 0.0, jnp.exp(lf * jnp.maximum(diff, 0.0)),
                          jnp.exp(lb * jnp.maximum(-diff, 0.0)))
        xi = jnp.exp(lf * (cpos + 1.0))
        g_chunk = jnp.exp(jnp.full((1, 1), lf * RET_CHUNK, F32))
        qk = slice(h * RET_DK, (h + 1) * RET_DK)
        vv = slice(h * RET_DV, (h + 1) * RET_DV)
        for c in range(ts // RET_CHUNK):
            rows = slice(c * RET_CHUNK, (c + 1) * RET_CHUNK)
            qc = q_ref[0, rows, qk]
            vc = v_ref[0, rows, vv]
            state = s_ref[h]
            scores = _dot_nt(qc, k_ref[0, rows, qk]) * decay
            y = (_dot(scores.astype(BF16), vc) + _dot(qc, state.astype(BF16)) * xi) + ob_ref[0, rows, vv]
            s_ref[h] = state * g_chunk + _dot_tn(kz_ref[0, rows, qk], vc)
            y = y * lax.rsqrt(jnp.mean(y * y, axis=-1, keepdims=True) + EPS)
            y = y * ng_ref[:, vv]
            z_ref[rows, vv] = (sg_ref[0, rows, vv] * y).astype(BF16)

    o_ref[0] = x_ref[0] + _dot(z_ref[...], w_ref[...])


def _ret_fwd(log_g, q, k, kz, v, sg, ob, x, ng, w_out, ts):
    b, s, d = x.shape
    tile = lambda width: pl.BlockSpec((1, ts, width), lambda bi, i: (bi, i, 0))
    return pl.pallas_call(
        functools.partial(_ret_fwd_kernel, ts=ts),
        grid=(b, s // ts),
        in_specs=[pl.BlockSpec(memory_space=pltpu.SMEM), tile(RET_QK), tile(RET_QK), tile(RET_QK), tile(RET_V),
                  tile(RET_V), tile(RET_V), tile(d), _const_spec((1, RET_V)), _const_spec(w_out.shape)],
        out_specs=tile(d),
        out_shape=jax.ShapeDtypeStruct((b, s, d), F32),
        scratch_shapes=[pltpu.VMEM((RET_HEADS, RET_DK, RET_DV), F32), pltpu.VMEM((ts, RET_V), BF16)],
        compiler_params=_params(BATCH_THEN_TILES),
        name="ret_fwd",
    )(log_g, q, k, kz, v, sg, ob, x, ng, w_out)


def _prepare(norm_mix, norm_ffn, norm_final, lru_w_in, lru_conv_w, lru_conv_b, lru_w_a, lru_b_a,
             lru_w_x, lru_b_x, lru_lambda, lru_w_out, ret_w_in, ret_decay_logit, ret_norm, ret_w_out,
             ffn_w_in, ffn_conv_w, ffn_conv_b, ffn_w_out, seq):
    ffn = []
    for i in range(2):
        w_in = ffn_w_in[i].astype(BF16)
        ffn.append(dict(
            g=norm_ffn[i][None, :],
            wu=w_in[:, :D_FF],
            wv=w_in[:, D_FF:],
            cw=ffn_conv_w[i],
            cb=ffn_conv_b[i][None, :],
            wo=ffn_w_out[i].astype(BF16),
        ))
    half = RET_DK // 2
    theta = ROPE_BASE ** (-jnp.arange(half, dtype=F32) / half)
    ang = jnp.arange(seq, dtype=F32)[:, None] * theta[None, :]
    return dict(
        ffn=ffn,
        norm_mix=[norm_mix[0][None, :], norm_mix[1][None, :]],
        norm_final=norm_final[None, :],
        lru_w_in=lru_w_in[0].astype(BF16),
        lru_cw=0.5 * lru_conv_w[0],
        lru_cb=0.5 * lru_conv_b[0][None, :],
        lru_wa=[lru_w_a[0, d].astype(BF16) for d in range(2)],
        lru_ba=[0.5 * lru_b_a[0, d][None, :] for d in range(2)],
        lru_wx=[lru_w_x[0, d].astype(BF16) for d in range(2)],
        lru_bx=[0.5 * lru_b_x[0, d][None, :] for d in range(2)],
        lru_lam=[lru_lambda[0, d][None, :] for d in range(2)],
        lru_w_out=lru_w_out[0].astype(BF16),
        ret_w_in=ret_w_in[0].astype(BF16),
        ret_log_g=jax.nn.log_sigmoid(ret_decay_logit[0].astype(F32)).reshape(2 * RET_HEADS),
        ret_norm=ret_norm[0][None, :],
        ret_w_out=ret_w_out[0].astype(BF16),
        cos=jnp.cos(ang),
        sin=jnp.sin(ang),
    )


def _encoder(x, p, ts):
    b, s, d = x.shape
    in_ts = min(ts, LRU_IN_ROWS // b)
    scan_rows = min(SCAN_ROWS, ts * b)
    gate, xc = _lru_in(x, p["norm_mix"][0], p["lru_w_in"], p["lru_cw"], p["lru_cb"], in_ts)
    scan = lambda di, fused: _lru_scan(xc, p["lru_wa"][di], p["lru_ba"][di], p["lru_wx"][di], p["lru_bx"][di],
                                       p["lru_lam"][di], b, scan_rows, reverse=bool(di), fused=fused)
    hf = scan(0, None)
    x = scan(1, (hf, gate, x, p["lru_w_out"]))
    f = p["ffn"][0]
    x = _ffn(x, f["g"], f["wu"], f["wv"], f["cw"], f["cb"], f["wo"], ts)
    q, k, kz, v, sg, ob = _ret_proj(p["ret_log_g"], x, p["norm_mix"][1], p["ret_w_in"], p["cos"], p["sin"], ts)
    x = _ret_fwd(p["ret_log_g"], q, k, kz, v, sg, ob, x, p["ret_norm"], p["ret_w_out"], ts)
    f = p["ffn"][1]
    return _ffn(x, f["g"], f["wu"], f["wv"], f["cw"], f["cb"], f["wo"], ts, final_g=p["norm_final"])


def kernel(x_prompt, x_sample, norm_mix, norm_ffn, norm_final, lru_w_in, lru_conv_w, lru_conv_b, lru_w_a, lru_b_a, lru_w_x, lru_b_x, lru_lambda, lru_w_out, ret_w_in, ret_decay_logit, ret_norm, ret_w_out, ffn_w_in, ffn_conv_w, ffn_conv_b, ffn_w_out):
    assert x_prompt.shape[1] == x_sample.shape[1] and x_prompt.shape[1] % SEQ_TILE == 0
    assert all(SUBLANES % x.shape[0] == 0 for x in (x_prompt, x_sample))
    p = _prepare(norm_mix, norm_ffn, norm_final, lru_w_in, lru_conv_w, lru_conv_b, lru_w_a, lru_b_a,
                 lru_w_x, lru_b_x, lru_lambda, lru_w_out, ret_w_in, ret_decay_logit, ret_norm,
                 ret_w_out, ffn_w_in, ffn_conv_w, ffn_conv_b, ffn_w_out, x_prompt.shape[1])
    return (_encoder(x_prompt, p, SEQ_TILE), _encoder(x_sample, p, SEQ_TILE))
```

```python
import functools

import jax
import jax.numpy as jnp
from jax import lax
from jax.experimental import pallas as pl
from jax.experimental.pallas import tpu as pltpu

F32 = jnp.float32
BF16 = jnp.bfloat16
STORE_DTYPE = jnp.bfloat16

EPS = 1e-6
D_MODEL = 1024
LRU_BLOCKS = 4
LRU_BLOCK_W = D_MODEL // LRU_BLOCKS
LRU_C = 8.0
RET_HEADS = 4
RET_DK = 256
RET_DV = 512
RET_QK = RET_HEADS * RET_DK
RET_V = RET_HEADS * RET_DV
RET_CHUNK = 256
ROPE_BASE = 10000.0
LOG2_E = 1.4426950408889634
D_FF = 2816

SUBLANES = 8
LANES = 128
LANE_SLABS = D_MODEL // LANES
HALO = SUBLANES
SEQ_TILE = 512
LRU_IN_ROWS = 1024
SCAN_ROWS = 512
FF_CHUNK = 256
VMEM_LIMIT_BYTES = 56 * 1024 * 1024


def _params(semantics):
    return pltpu.CompilerParams(dimension_semantics=semantics, vmem_limit_bytes=VMEM_LIMIT_BYTES)


BATCH_THEN_TILES = ("parallel", "arbitrary")


def _const_spec(shape):
    zeros = (0,) * len(shape)
    return pl.BlockSpec(shape, lambda *_: zeros, pipeline_mode=pl.Buffered(1))


def _rms(x, g):
    return x * lax.rsqrt(jnp.mean(x * x, axis=-1, keepdims=True) + EPS) * g


def _gelu(x):
    return jax.nn.gelu(x, approximate=True)


def _sqrt_nonneg(x):
    return jnp.where(x > 0.0, x * lax.rsqrt(x), 0.0)


def _dot(a, b):
    return jnp.dot(a, b, preferred_element_type=F32)


def _dot_tn(a, b):
    return lax.dot_general(a, b, (((0,), (0,)), ((), ())), preferred_element_type=F32)


def _dot_nt(a, b):
    return lax.dot_general(a, b, (((1,), (1,)), ((), ())), preferred_element_type=F32)


def _tile_and_halo_specs(s, ts, d, batch_first=True):
    halo_per_tile = ts // HALO
    n_halo = s // HALO

    def spec(rows, tile_to_block):
        if batch_first:
            return pl.BlockSpec((1, rows, d), lambda bi, i: (bi, tile_to_block(i), 0))
        return pl.BlockSpec((1, rows, d), lambda i, bi: (bi, tile_to_block(i), 0))

    tile = spec(ts, lambda i: i)
    prev = spec(HALO, lambda i: jnp.maximum(i * halo_per_tile - 1, 0))
    nxt = spec(HALO, lambda i: jnp.minimum((i + 1) * halo_per_tile, n_halo - 1))
    return tile, prev, nxt


def _store_normed_tile_with_halo(x, xp_ref, xn_ref, g, xs_ref, j, n_tiles, ts):
    xs_ref[0:ts] = _rms(x, g).astype(BF16)
    nxt = jnp.where(j == n_tiles - 1, 0.0, _rms(xn_ref[0], g))
    prv = jnp.where(j == 0, 0.0, _rms(xp_ref[0], g))
    xs_ref[ts:ts + 2 * HALO] = jnp.concatenate([nxt, prv], axis=0).astype(BF16)


def _time_shift(ext, k, ts):
    return pltpu.roll(ext, (-k) % ext.shape[0], 0)[0:ts]


def _batch_tile_specs(batch, s, ts, d, reverse=False):
    n_tiles = s // ts
    halo_per_tile = ts // HALO
    n_halo = s // HALO

    def tile_of(i):
        return (n_tiles - 1 - i) if reverse else i

    tile = pl.BlockSpec((batch, ts, d), lambda i: (0, tile_of(i), 0))
    prev = pl.BlockSpec((batch, HALO, d), lambda i: (0, jnp.maximum(tile_of(i) * halo_per_tile - 1, 0), 0))
    nxt = pl.BlockSpec((batch, HALO, d),
                       lambda i: (0, jnp.minimum((tile_of(i) + 1) * halo_per_tile, n_halo - 1), 0))
    return tile, prev, nxt


def _slab_spec(rows, n_tiles, reverse=False):
    return pl.BlockSpec((LANE_SLABS, rows, LANES), lambda i: (0, (n_tiles - 1 - i) if reverse else i, 0))


def _lru_in_kernel(x_ref, xp_ref, xn_ref, g_ref, w_ref, cw_ref, cb_ref, gate_ref, xc_ref, xs_ref,
                   *, n_tiles, ts, batch):
    w = D_MODEL
    j = pl.program_id(0)
    g = g_ref[...]
    body = batch * ts
    halo = 2 * HALO
    for b in range(batch):
        xs_ref[b * ts:(b + 1) * ts] = _rms(x_ref[b], g).astype(BF16)
        nxt = jnp.where(j == n_tiles - 1, 0.0, _rms(xn_ref[b], g))
        prv = jnp.where(j == 0, 0.0, _rms(xp_ref[b], g))
        xs_ref[body + b * halo:body + (b + 1) * halo] = jnp.concatenate([nxt, prv], axis=0).astype(BF16)

    slabs_per_block = LRU_BLOCK_W // LANES
    for nb in range(LRU_BLOCKS):
        sl = slice(nb * LRU_BLOCK_W, (nb + 1) * LRU_BLOCK_W)
        gate = _gelu(_dot(xs_ref[0:body], w_ref[:, sl]))
        rec = _dot(xs_ref[...], w_ref[:, w + nb * LRU_BLOCK_W:w + (nb + 1) * LRU_BLOCK_W])
        for b in range(batch):
            ext = jnp.concatenate([rec[b * ts:(b + 1) * ts], rec[body + b * halo:body + (b + 1) * halo]], axis=0)
            xc = (_time_shift(ext, -2, ts) * cw_ref[0:1, sl] + _time_shift(ext, -1, ts) * cw_ref[1:2, sl]
                  + ext[0:ts] * cw_ref[2:3, sl] + _time_shift(ext, 1, ts) * cw_ref[3:4, sl] + cb_ref[:, sl])
            rows = pl.ds(b, ts, stride=batch)
            for k in range(slabs_per_block):
                slab = nb * slabs_per_block + k
                lanes = slice(k * LANES, (k + 1) * LANES)
                xc_ref[slab, rows, :] = xc[:, lanes]
                gate_ref[slab, rows, :] = gate[b * ts:(b + 1) * ts, lanes]


def _lru_in(x, g, w_in, cw, cb, ts):
    b, s, d = x.shape
    n_tiles = s // ts
    tile, prev, nxt = _batch_tile_specs(b, s, ts, d)
    slabs = _slab_spec(ts * b, n_tiles)
    out = jax.ShapeDtypeStruct((LANE_SLABS, s * b, LANES), F32)
    return pl.pallas_call(
        functools.partial(_lru_in_kernel, n_tiles=n_tiles, ts=ts, batch=b),
        grid=(n_tiles,),
        in_specs=[tile, prev, nxt, _const_spec((1, d)), _const_spec((d, 2 * d)), _const_spec(cw.shape),
                  _const_spec((1, d))],
        out_specs=[slabs, slabs],
        out_shape=[out, out],
        scratch_shapes=[pltpu.VMEM((b * (ts + 2 * HALO), d), BF16)],
        compiler_params=_params(("arbitrary",)),
        name="lru_in",
    )(x, x, x, g, w_in, cw, cb)


def _lru_scan_kernel(*refs, reverse, rows, batch, fused_out):
    if fused_out:
        (xc_ref, wa_ref, ba_ref, wx_ref, bx_ref, lam_ref, hf_ref, gate_ref, x_ref, wo_ref,
         o_ref, carry_ref, a_ref, u_ref, h_ref, z_ref, res_ref) = refs
    else:
        xc_ref, wa_ref, ba_ref, wx_ref, bx_ref, lam_ref, h_ref, carry_ref, a_ref, u_ref = refs

    @pl.when(pl.program_id(0) == 0)
    def _():
        carry_ref[...] = jnp.zeros_like(carry_ref)

    neg_lam = -lam_ref[...]
    softplus = jnp.maximum(neg_lam, 0.0) + jnp.log1p(jnp.exp(-jnp.abs(neg_lam)))
    decay = (0.5 * LRU_C) * softplus
    slabs_per_block = LRU_BLOCK_W // LANES
    for nb in range(LRU_BLOCKS):
        sl = slice(nb * LRU_BLOCK_W, (nb + 1) * LRU_BLOCK_W)
        xh = jnp.concatenate([xc_ref[nb * slabs_per_block + k] for k in range(slabs_per_block)], axis=1)
        xh16 = xh.astype(BF16)
        tr = jnp.tanh(_dot(xh16, wa_ref[nb]) + ba_ref[:, sl])
        ti = jnp.tanh(_dot(xh16, wx_ref[nb]) + bx_ref[:, sl])
        neg_log_a = decay[:, sl] * (tr + 1.0)
        a = jnp.exp2(neg_log_a * (-LOG2_E))
        one_minus_a2 = jnp.tanh(neg_log_a) * (a * a + 1.0)
        a_ref[:, sl] = a
        u_ref[:, sl] = _sqrt_nonneg(one_minus_a2) * (xh * (ti + 1.0))

    groups = rows // SUBLANES
    substeps = SUBLANES // batch
    shift = (SUBLANES - batch) if reverse else batch % SUBLANES
    sub = lax.broadcasted_iota(jnp.int32, (SUBLANES, D_MODEL), 0)

    def step(gi, c):
        g = (groups - 1 - gi) if reverse else gi
        r = pl.ds(pl.multiple_of(g * SUBLANES, SUBLANES), SUBLANES)
        a8 = a_ref[r, :]
        u8 = u_ref[r, :]
        h = a8 * c + u8
        out = h
        for k in range(1, substeps):
            h = a8 * pltpu.roll(h, shift, 0) + u8
            if reverse:
                out = jnp.where(sub < (substeps - k) * batch, h, out)
            else:
                out = jnp.where(sub >= k * batch, h, out)
        for slab in range(LANE_SLABS):
            h_ref[slab, r, :] = out[:, slab * LANES:(slab + 1) * LANES]
        return pltpu.roll(h, shift, 0) if substeps > 1 else h

    carry_ref[...] = lax.fori_loop(0, groups, step, carry_ref[...], unroll=4)

    if fused_out:
        for slab in range(LANE_SLABS):
            lanes = slice(slab * LANES, (slab + 1) * LANES)
            z_ref[:, lanes] = ((hf_ref[slab] + h_ref[slab]) * gate_ref[slab]).astype(BF16)
        res = _dot(z_ref[...], wo_ref[...])
        for slab in range(LANE_SLABS):
            res_ref[slab] = res[:, slab * LANES:(slab + 1) * LANES]
        ts = rows // batch
        for b in range(batch):
            for slab in range(LANE_SLABS):
                lanes = slice(slab * LANES, (slab + 1) * LANES)
                o_ref[b, :, lanes] = x_ref[b, :, lanes] + res_ref[slab, pl.ds(b, ts, stride=batch), :]


def _lru_scan(xc, wa, ba, wx, bx, lam, batch, rows, reverse, fused=None):
    slabs, n, lanes = xc.shape
    w = slabs * lanes
    n_tiles = n // rows
    tile = _slab_spec(rows, n_tiles, reverse)
    row = _const_spec((1, w))
    gate_w = _const_spec((LRU_BLOCKS, LRU_BLOCK_W, LRU_BLOCK_W))
    in_specs = [tile, gate_w, row, gate_w, row, row]
    args = [xc, wa, ba, wx, bx, lam]
    scratch = [pltpu.VMEM((SUBLANES, w), F32), pltpu.VMEM((rows, w), F32), pltpu.VMEM((rows, w), F32)]
    if fused is None:
        out_specs = tile
        out_shape = jax.ShapeDtypeStruct(xc.shape, F32)
    else:
        h_other, gate, x, w_out = fused
        _, s, d = x.shape
        x_tile, _, _ = _batch_tile_specs(batch, s, rows // batch, d, reverse)
        in_specs += [tile, tile, x_tile, _const_spec(w_out.shape)]
        args += [h_other, gate, x, w_out]
        out_specs = x_tile
        out_shape = jax.ShapeDtypeStruct(x.shape, F32)
        scratch += [pltpu.VMEM((slabs, rows, lanes), F32), pltpu.VMEM((rows, w), BF16),
                    pltpu.VMEM((slabs, rows, lanes), F32)]
    kern = functools.partial(_lru_scan_kernel, reverse=reverse, rows=rows, batch=batch,
                             fused_out=fused is not None)
    return pl.pallas_call(
        kern,
        grid=(n_tiles,),
        in_specs=in_specs,
        out_specs=out_specs,
        out_shape=out_shape,
        scratch_shapes=scratch,
        compiler_params=_params(("arbitrary",)),
        name=("lru_scan_bwd_out" if fused is not None else "lru_scan_bwd") if reverse else "lru_scan_fwd",
    )(*args)


def _ffn_kernel(*refs, n_tiles, ts, final):
    if final:
        (x_ref, xp_ref, xn_ref, g_ref, wu_ref, wv_ref, cw_ref, cb_ref, wo_ref, gf_ref,
         o_ref, xs_ref, act_ref) = refs
    else:
        (x_ref, xp_ref, xn_ref, g_ref, wu_ref, wv_ref, cw_ref, cb_ref, wo_ref,
         o_ref, xs_ref, act_ref) = refs
    x = x_ref[0]
    _store_normed_tile_with_halo(x, xp_ref, xn_ref, g_ref[...], xs_ref, pl.program_id(1), n_tiles, ts)
    for c in range(D_FF // FF_CHUNK):
        cols = slice(c * FF_CHUNK, (c + 1) * FF_CHUNK)
        u = _dot(xs_ref[...], wu_ref[:, cols])
        v = _dot(xs_ref[0:ts], wv_ref[:, cols])
        y = (_time_shift(u, -1, ts) * cw_ref[0:1, cols] + u[0:ts] * cw_ref[1:2, cols]
             + _time_shift(u, 1, ts) * cw_ref[2:3, cols] + cb_ref[:, cols])
        act_ref[:, cols] = (_gelu(y) * v).astype(BF16)

    out = x + _dot(act_ref[...], wo_ref[...])
    if final:
        out = _rms(out, gf_ref[...])
    o_ref[0] = out


def _ffn(x, g, wu, wv, cw, cb, wo, ts, final_g=None):
    b, s, d = x.shape
    n_tiles = s // ts
    tile, prev, nxt = _tile_and_halo_specs(s, ts, d)
    final = final_g is not None
    in_specs = [tile, prev, nxt, _const_spec((1, d)), _const_spec(wu.shape), _const_spec(wv.shape),
                _const_spec(cw.shape), _const_spec(cb.shape), _const_spec(wo.shape)]
    args = [x, x, x, g, wu, wv, cw, cb, wo]
    if final:
        in_specs.append(_const_spec((1, d)))
        args.append(final_g)
    kern = functools.partial(_ffn_kernel, n_tiles=n_tiles, ts=ts, final=final)
    return pl.pallas_call(
        kern,
        grid=(b, n_tiles),
        in_specs=in_specs,
        out_specs=tile,
        out_shape=jax.ShapeDtypeStruct((b, s, d), F32),
        scratch_shapes=[pltpu.VMEM((ts + 2 * HALO, d), BF16), pltpu.VMEM((ts, D_FF), BF16)],
        compiler_params=_params(BATCH_THEN_TILES),
        name="ffn_final" if final else "ffn",
    )(*args)


def _chunk_pos(rows):
    return (lax.broadcasted_iota(jnp.int32, (rows, 1), 0) % RET_CHUNK).astype(F32)


def _ret_proj_kernel(lg_ref, x_ref, g_ref, w_ref, cos_ref, sin_ref,
                     q_ref, k_ref, kz_ref, v_ref, sg_ref, ob_ref, xs_ref, kzb_ref, s_ref, *, ts):
    @pl.when(pl.program_id(1) == 0)
    def _():
        s_ref[...] = jnp.zeros_like(s_ref)

    xs_ref[...] = _rms(x_ref[0], g_ref[...]).astype(BF16)
    cos = cos_ref[...]
    sin = sin_ref[...]
    half = RET_DK // 2
    pos = _chunk_pos(ts)

    def rotary(t):
        t1 = t[:, :half]
        t2 = t[:, half:]
        return t1 * cos - t2 * sin, t2 * cos + t1 * sin

    for h in range(RET_HEADS):
        lo = h * RET_DK
        mid = lo + half
        hi = lo + RET_DK
        q1, q2 = rotary(_dot(xs_ref[...], w_ref[:, lo:hi]))
        q_ref[0, :, lo:mid] = q1.astype(BF16)
        q_ref[0, :, mid:hi] = q2.astype(BF16)
        k1, k2 = rotary(_dot(xs_ref[...], w_ref[:, RET_QK + lo:RET_QK + hi]) * (RET_DK ** -0.5))
        k_ref[0, :, lo:mid] = k1.astype(BF16)
        k_ref[0, :, mid:hi] = k2.astype(BF16)
        zeta_f = jnp.exp(lg_ref[h] * (RET_CHUNK - 1.0 - pos))
        kz_ref[0, :, lo:mid] = (k1 * zeta_f).astype(BF16)
        kz_ref[0, :, mid:hi] = (k2 * zeta_f).astype(BF16)
        zeta_b = jnp.exp(lg_ref[RET_HEADS + h] * pos)
        kzb_ref[:, lo:mid] = (k1 * zeta_b).astype(BF16)
        kzb_ref[:, mid:hi] = (k2 * zeta_b).astype(BF16)
    for h in range(RET_HEADS):
        vv = slice(h * RET_DV, (h + 1) * RET_DV)
        lo = 2 * RET_QK + h * RET_DV
        v_ref[0, :, vv] = _dot(xs_ref[...], w_ref[:, lo:lo + RET_DV]).astype(BF16)
        lo = 2 * RET_QK + RET_V + h * RET_DV
        sg_ref[0, :, vv] = jax.nn.silu(_dot(xs_ref[...], w_ref[:, lo:lo + RET_DV])).astype(sg_ref.dtype)

    cpos = _chunk_pos(RET_CHUNK)
    for c in reversed(range(ts // RET_CHUNK)):
        rows = slice(c * RET_CHUNK, (c + 1) * RET_CHUNK)
        for h in range(RET_HEADS):
            lg = lg_ref[RET_HEADS + h]
            xi = jnp.exp(lg * (RET_CHUNK - cpos))
            g_chunk = jnp.exp(jnp.full((1, 1), lg * RET_CHUNK, F32))
            qk = slice(h * RET_DK, (h + 1) * RET_DK)
            vv = slice(h * RET_DV, (h + 1) * RET_DV)
            state = s_ref[h]
            ob_ref[0, rows, vv] = (_dot(q_ref[0, rows, qk], state.astype(BF16)) * xi).astype(ob_ref.dtype)
            s_ref[h] = state * g_chunk + _dot_tn(kzb_ref[rows, qk], v_ref[0, rows, vv])


def _ret_proj(log_g, x, g, w_in, cos, sin, ts):
    b, s, d = x.shape
    n_tiles = s // ts
    tile = lambda width: pl.BlockSpec((1, ts, width), lambda bi, i: (bi, n_tiles - 1 - i, 0))
    rope = pl.BlockSpec((ts, RET_DK // 2), lambda bi, i: (n_tiles - 1 - i, 0))
    act = lambda width, dtype: jax.ShapeDtypeStruct((b, s, width), dtype)
    return pl.pallas_call(
        functools.partial(_ret_proj_kernel, ts=ts),
        grid=(b, n_tiles),
        in_specs=[pl.BlockSpec(memory_space=pltpu.SMEM), tile(d), _const_spec((1, d)), _const_spec(w_in.shape),
                  rope, rope],
        out_specs=[tile(RET_QK), tile(RET_QK), tile(RET_QK), tile(RET_V), tile(RET_V), tile(RET_V)],
        out_shape=[act(RET_QK, BF16), act(RET_QK, BF16), act(RET_QK, BF16), act(RET_V, BF16),
                   act(RET_V, STORE_DTYPE), act(RET_V, STORE_DTYPE)],
        scratch_shapes=[pltpu.VMEM((ts, d), BF16), pltpu.VMEM((ts, RET_QK), BF16),
                        pltpu.VMEM((RET_HEADS, RET_DK, RET_DV), F32)],
        compiler_params=_params(BATCH_THEN_TILES),
        name="ret_proj",
    )(log_g, x, g, w_in, cos, sin)


def _ret_fwd_kernel(lg_ref, q_ref, k_ref, kz_ref, v_ref, sg_ref, ob_ref, x_ref, ng_ref, w_ref, o_ref,
                    s_ref, z_ref, *, ts):
    @pl.when(pl.program_id(1) == 0)
    def _():
        s_ref[...] = jnp.zeros_like(s_ref)

    cpos = _chunk_pos(RET_CHUNK)
    n_idx = lax.broadcasted_iota(jnp.int32, (RET_CHUNK, RET_CHUNK), 0)
    m_idx = lax.broadcasted_iota(jnp.int32, (RET_CHUNK, RET_CHUNK), 1)
    diff = (n_idx - m_idx).astype(F32)
    for h in range(RET_HEADS):
        lf = lg_ref[h]
        lb = lg_ref[RET_HEADS + h]
        decay = jnp.where(diff >= 0.0, jnp.exp(lf * jnp.maximum(diff, 0.0)),
                          jnp.exp(lb * jnp.maximum(-diff, 0.0)))
        xi = jnp.exp(lf * (cpos + 1.0))
        g_chunk = jnp.exp(jnp.full((1, 1), lf * RET_CHUNK, F32))
        qk = slice(h * RET_DK, (h + 1) * RET_DK)
        vv = slice(h * RET_DV, (h + 1) * RET_DV)
        for c in range(ts // RET_CHUNK):
            rows = slice(c * RET_CHUNK, (c + 1) * RET_CHUNK)
            qc = q_ref[0, rows, qk]
            vc = v_ref[0, rows, vv]
            state = s_ref[h]
            scores = _dot_nt(qc, k_ref[0, rows, qk]) * decay
            y = ((_dot(scores.astype(BF16), vc) + _dot(qc, state.astype(BF16)) * xi)
                 + ob_ref[0, rows, vv].astype(F32))
            s_ref[h] = state * g_chunk + _dot_tn(kz_ref[0, rows, qk], vc)
            y = y * lax.rsqrt(jnp.mean(y * y, axis=-1, keepdims=True) + EPS)
            y = y * ng_ref[:, vv]
            z_ref[rows, vv] = (sg_ref[0, rows, vv].astype(F32) * y).astype(BF16)

    o_ref[0] = x_ref[0] + _dot(z_ref[...], w_ref[...])


def _ret_fwd(log_g, q, k, kz, v, sg, ob, x, ng, w_out, ts):
    b, s, d = x.shape
    tile = lambda width: pl.BlockSpec((1, ts, width), lambda bi, i: (bi, i, 0))
    return pl.pallas_call(
        functools.partial(_ret_fwd_kernel, ts=ts),
        grid=(b, s // ts),
        in_specs=[pl.BlockSpec(memory_space=pltpu.SMEM), tile(RET_QK), tile(RET_QK), tile(RET_QK), tile(RET_V),
                  tile(RET_V), tile(RET_V), tile(d), _const_spec((1, RET_V)), _const_spec(w_out.shape)],
        out_specs=tile(d),
        out_shape=jax.ShapeDtypeStruct((b, s, d), F32),
        scratch_shapes=[pltpu.VMEM((RET_HEADS, RET_DK, RET_DV), F32), pltpu.VMEM((ts, RET_V), BF16)],
        compiler_params=_params(BATCH_THEN_TILES),
        name="ret_fwd",
    )(log_g, q, k, kz, v, sg, ob, x, ng, w_out)


def _prepare(norm_mix, norm_ffn, norm_final, lru_w_in, lru_conv_w, lru_conv_b, lru_w_a, lru_b_a,
             lru_w_x, lru_b_x, lru_lambda, lru_w_out, ret_w_in, ret_decay_logit, ret_norm, ret_w_out,
             ffn_w_in, ffn_conv_w, ffn_conv_b, ffn_w_out, seq):
    ffn = []
    for i in range(2):
        w_in = ffn_w_in[i].astype(BF16)
        ffn.append(dict(
            g=norm_ffn[i][None, :],
            wu=w_in[:, :D_FF],
            wv=w_in[:, D_FF:],
            cw=ffn_conv_w[i],
            cb=ffn_conv_b[i][None, :],
            wo=ffn_w_out[i].astype(BF16),
        ))
    half = RET_DK // 2
    theta = ROPE_BASE ** (-jnp.arange(half, dtype=F32) / half)
    ang = jnp.arange(seq, dtype=F32)[:, None] * theta[None, :]
    return dict(
        ffn=ffn,
        norm_mix=[norm_mix[0][None, :], norm_mix[1][None, :]],
        norm_final=norm_final[None, :],
        lru_w_in=lru_w_in[0].astype(BF16),
        lru_cw=0.5 * lru_conv_w[0],
        lru_cb=0.5 * lru_conv_b[0][None, :],
        lru_wa=[lru_w_a[0, d].astype(BF16) for d in range(2)],
        lru_ba=[0.5 * lru_b_a[0, d][None, :] for d in range(2)],
        lru_wx=[lru_w_x[0, d].astype(BF16) for d in range(2)],
        lru_bx=[0.5 * lru_b_x[0, d][None, :] for d in range(2)],
        lru_lam=[lru_lambda[0, d][None, :] for d in range(2)],
        lru_w_out=lru_w_out[0].astype(BF16),
        ret_w_in=ret_w_in[0].astype(BF16),
        ret_log_g=jax.nn.log_sigmoid(ret_decay_logit[0].astype(F32)).reshape(2 * RET_HEADS),
        ret_norm=ret_norm[0][None, :],
        ret_w_out=ret_w_out[0].astype(BF16),
        cos=jnp.cos(ang),
        sin=jnp.sin(ang),
    )


def _encoder(x, p, ts):
    b, s, d = x.shape
    in_ts = min(ts, LRU_IN_ROWS // b)
    scan_rows = min(SCAN_ROWS, ts * b)
    gate, xc = _lru_in(x, p["norm_mix"][0], p["lru_w_in"], p["lru_cw"], p["lru_cb"], in_ts)
    scan = lambda di, fused: _lru_scan(xc, p["lru_wa"][di], p["lru_ba"][di], p["lru_wx"][di], p["lru_bx"][di],
                                       p["lru_lam"][di], b, scan_rows, reverse=bool(di), fused=fused)
    hf = scan(0, None)
    x = scan(1, (hf, gate, x, p["lru_w_out"]))
    f = p["ffn"][0]
    x = _ffn(x, f["g"], f["wu"], f["wv"], f["cw"], f["cb"], f["wo"], ts)
    q, k, kz, v, sg, ob = _ret_proj(p["ret_log_g"], x, p["norm_mix"][1], p["ret_w_in"], p["cos"], p["sin"], ts)
    x = _ret_fwd(p["ret_log_g"], q, k, kz, v, sg, ob, x, p["ret_norm"], p["ret_w_out"], ts)
    f = p["ffn"][1]
    return _ffn(x, f["g"], f["wu"], f["wv"], f["cw"], f["cb"], f["wo"], ts, final_g=p["norm_final"])


def kernel(x_prompt, x_sample, norm_mix, norm_ffn, norm_final, lru_w_in, lru_conv_w, lru_conv_b, lru_w_a, lru_b_a, lru_w_x, lru_b_x, lru_lambda, lru_w_out, ret_w_in, ret_decay_logit, ret_norm, ret_w_out, ffn_w_in, ffn_conv_w, ffn_conv_b, ffn_w_out):
    assert x_prompt.shape[1] == x_sample.shape[1] and x_prompt.shape[1] % SEQ_TILE == 0
    assert all(SUBLANES % x.shape[0] == 0 for x in (x_prompt, x_sample))
    p = _prepare(norm_mix, norm_ffn, norm_final, lru_w_in, lru_conv_w, lru_conv_b, lru_w_a, lru_b_a,
                 lru_w_x, lru_b_x, lru_lambda, lru_w_out, ret_w_in, ret_decay_logit, ret_norm,
                 ret_w_out, ffn_w_in, ffn_conv_w, ffn_conv_b, ffn_w_out, x_prompt.shape[1])
    return (_encoder(x_prompt, p, SEQ_TILE), _encoder(x_sample, p, SEQ_TILE))
```

```python
import functools

import jax
import jax.numpy as jnp
from jax import lax
from jax.experimental import pallas as pl
from jax.experimental.pallas import tpu as pltpu

F32 = jnp.float32
BF16 = jnp.bfloat16

EPS = 1e-6
D_MODEL = 1024
LRU_BLOCKS = 4
LRU_BLOCK_W = D_MODEL // LRU_BLOCKS
LRU_C = 8.0
RET_HEADS = 4
RET_DK = 256
RET_DV = 512
RET_QK = RET_HEADS * RET_DK
RET_V = RET_HEADS * RET_DV
RET_CHUNK = 256
ROPE_BASE = 10000.0
LOG2_E = 1.4426950408889634
D_FF = 2816

SUBLANES = 8
LANES = 128
LANE_SLABS = D_MODEL // LANES
HALO = SUBLANES
SEQ_TILE = 512
FFN_TILE = 1024
LRU_IN_ROWS = 1024
SCAN_ROWS_FWD = 1024
SCAN_ROWS_BWD = 512
FF_CHUNK = 256
VMEM_LIMIT_BYTES = 56 * 1024 * 1024


def _params(semantics):
    return pltpu.CompilerParams(dimension_semantics=semantics, vmem_limit_bytes=VMEM_LIMIT_BYTES)


BATCH_THEN_TILES = ("parallel", "arbitrary")


def _const_spec(shape):
    zeros = (0,) * len(shape)
    return pl.BlockSpec(shape, lambda *_: zeros, pipeline_mode=pl.Buffered(1))


def _rms(x, g):
    return x * lax.rsqrt(jnp.mean(x * x, axis=-1, keepdims=True) + EPS) * g


def _gelu(x):
    return jax.nn.gelu(x, approximate=True)


def _sqrt_nonneg(x):
    return jnp.where(x > 0.0, x * lax.rsqrt(x), 0.0)


def _dot(a, b):
    return jnp.dot(a, b, preferred_element_type=F32)


def _dot_tn(a, b):
    return lax.dot_general(a, b, (((0,), (0,)), ((), ())), preferred_element_type=F32)


def _dot_nt(a, b):
    return lax.dot_general(a, b, (((1,), (1,)), ((), ())), preferred_element_type=F32)


def _tile_and_halo_specs(s, ts, d, batch_first=True):
    halo_per_tile = ts // HALO
    n_halo = s // HALO

    def spec(rows, tile_to_block):
        if batch_first:
            return pl.BlockSpec((1, rows, d), lambda bi, i: (bi, tile_to_block(i), 0))
        return pl.BlockSpec((1, rows, d), lambda i, bi: (bi, tile_to_block(i), 0))

    tile = spec(ts, lambda i: i)
    prev = spec(HALO, lambda i: jnp.maximum(i * halo_per_tile - 1, 0))
    nxt = spec(HALO, lambda i: jnp.minimum((i + 1) * halo_per_tile, n_halo - 1))
    return tile, prev, nxt


def _store_normed_tile_with_halo(x, xp_ref, xn_ref, g, xs_ref, j, n_tiles, ts):
    xs_ref[0:ts] = _rms(x, g).astype(BF16)
    nxt = jnp.where(j == n_tiles - 1, 0.0, _rms(xn_ref[0], g))
    prv = jnp.where(j == 0, 0.0, _rms(xp_ref[0], g))
    xs_ref[ts:ts + 2 * HALO] = jnp.concatenate([nxt, prv], axis=0).astype(BF16)


def _time_shift(ext, k, ts):
    return pltpu.roll(ext, (-k) % ext.shape[0], 0)[0:ts]


def _batch_tile_specs(batch, s, ts, d, reverse=False):
    n_tiles = s // ts
    halo_per_tile = ts // HALO
    n_halo = s // HALO

    def tile_of(i):
        return (n_tiles - 1 - i) if reverse else i

    tile = pl.BlockSpec((batch, ts, d), lambda i: (0, tile_of(i), 0))
    prev = pl.BlockSpec((batch, HALO, d), lambda i: (0, jnp.maximum(tile_of(i) * halo_per_tile - 1, 0), 0))
    nxt = pl.BlockSpec((batch, HALO, d),
                       lambda i: (0, jnp.minimum((tile_of(i) + 1) * halo_per_tile, n_halo - 1), 0))
    return tile, prev, nxt


def _slab_spec(rows, n_tiles, reverse=False):
    return pl.BlockSpec((LANE_SLABS, rows, LANES), lambda i: (0, (n_tiles - 1 - i) if reverse else i, 0))


def _lru_in_kernel(x_ref, xp_ref, xn_ref, g_ref, w_ref, cw_ref, cb_ref, gate_ref, xc_ref, xs_ref,
                   *, n_tiles, ts, batch):
    w = D_MODEL
    j = pl.program_id(0)
    g = g_ref[...]
    body = batch * ts
    halo = 2 * HALO
    for b in range(batch):
        xs_ref[b * ts:(b + 1) * ts] = _rms(x_ref[b], g).astype(BF16)
        nxt = jnp.where(j == n_tiles - 1, 0.0, _rms(xn_ref[b], g))
        prv = jnp.where(j == 0, 0.0, _rms(xp_ref[b], g))
        xs_ref[body + b * halo:body + (b + 1) * halo] = jnp.concatenate([nxt, prv], axis=0).astype(BF16)

    slabs_per_block = LRU_BLOCK_W // LANES
    for nb in range(LRU_BLOCKS):
        sl = slice(nb * LRU_BLOCK_W, (nb + 1) * LRU_BLOCK_W)
        gate = _gelu(_dot(xs_ref[0:body], w_ref[:, sl]))
        rec = _dot(xs_ref[...], w_ref[:, w + nb * LRU_BLOCK_W:w + (nb + 1) * LRU_BLOCK_W])
        for b in range(batch):
            ext = jnp.concatenate([rec[b * ts:(b + 1) * ts], rec[body + b * halo:body + (b + 1) * halo]], axis=0)
            xc = (_time_shift(ext, -2, ts) * cw_ref[0:1, sl] + _time_shift(ext, -1, ts) * cw_ref[1:2, sl]
                  + ext[0:ts] * cw_ref[2:3, sl] + _time_shift(ext, 1, ts) * cw_ref[3:4, sl] + cb_ref[:, sl])
            rows = pl.ds(b, ts, stride=batch)
            for k in range(slabs_per_block):
                slab = nb * slabs_per_block + k
                lanes = slice(k * LANES, (k + 1) * LANES)
                xc_ref[slab, rows, :] = xc[:, lanes]
                gate_ref[slab, rows, :] = gate[b * ts:(b + 1) * ts, lanes]


def _lru_in(x, g, w_in, cw, cb, ts):
    b, s, d = x.shape
    n_tiles = s // ts
    tile, prev, nxt = _batch_tile_specs(b, s, ts, d)
    slabs = _slab_spec(ts * b, n_tiles)
    out = jax.ShapeDtypeStruct((LANE_SLABS, s * b, LANES), F32)
    return pl.pallas_call(
        functools.partial(_lru_in_kernel, n_tiles=n_tiles, ts=ts, batch=b),
        grid=(n_tiles,),
        in_specs=[tile, prev, nxt, _const_spec((1, d)), _const_spec((d, 2 * d)), _const_spec(cw.shape),
                  _const_spec((1, d))],
        out_specs=[slabs, slabs],
        out_shape=[out, out],
        scratch_shapes=[pltpu.VMEM((b * (ts + 2 * HALO), d), BF16)],
        compiler_params=_params(("arbitrary",)),
        name="lru_in",
    )(x, x, x, g, w_in, cw, cb)


def _lru_scan_kernel(*refs, reverse, rows, batch, fused_out):
    if fused_out:
        (xc_ref, wa_ref, ba_ref, wx_ref, bx_ref, lam_ref, hf_ref, gate_ref, x_ref, wo_ref,
         o_ref, carry_ref, a_ref, u_ref, h_ref, z_ref, res_ref) = refs
    else:
        xc_ref, wa_ref, ba_ref, wx_ref, bx_ref, lam_ref, h_ref, carry_ref, a_ref, u_ref = refs

    @pl.when(pl.program_id(0) == 0)
    def _():
        carry_ref[...] = jnp.zeros_like(carry_ref)

    neg_lam = -lam_ref[...]
    softplus = jnp.maximum(neg_lam, 0.0) + jnp.log1p(jnp.exp(-jnp.abs(neg_lam)))
    decay = (0.5 * LRU_C) * softplus
    slabs_per_block = LRU_BLOCK_W // LANES
    for nb in range(LRU_BLOCKS):
        sl = slice(nb * LRU_BLOCK_W, (nb + 1) * LRU_BLOCK_W)
        xh = jnp.concatenate([xc_ref[nb * slabs_per_block + k] for k in range(slabs_per_block)], axis=1)
        xh16 = xh.astype(BF16)
        tr = jnp.tanh(_dot(xh16, wa_ref[nb]) + ba_ref[:, sl])
        ti = jnp.tanh(_dot(xh16, wx_ref[nb]) + bx_ref[:, sl])
        neg_log_a = decay[:, sl] * (tr + 1.0)
        a = jnp.exp2(neg_log_a * (-LOG2_E))
        one_minus_a2 = jnp.tanh(neg_log_a) * (a * a + 1.0)
        a_ref[:, sl] = a
        u_ref[:, sl] = _sqrt_nonneg(one_minus_a2) * (xh * (ti + 1.0))

    groups = rows // SUBLANES
    substeps = SUBLANES // batch
    shift = (SUBLANES - batch) if reverse else batch % SUBLANES
    sub = lax.broadcasted_iota(jnp.int32, (SUBLANES, D_MODEL), 0)

    def step(gi, c):
        g = (groups - 1 - gi) if reverse else gi
        r = pl.ds(pl.multiple_of(g * SUBLANES, SUBLANES), SUBLANES)
        a8 = a_ref[r, :]
        u8 = u_ref[r, :]
        h = a8 * c + u8
        out = h
        for k in range(1, substeps):
            h = a8 * pltpu.roll(h, shift, 0) + u8
            if reverse:
                out = jnp.where(sub < (substeps - k) * batch, h, out)
            else:
                out = jnp.where(sub >= k * batch, h, out)
        for slab in range(LANE_SLABS):
            h_ref[slab, r, :] = out[:, slab * LANES:(slab + 1) * LANES]
        return pltpu.roll(h, shift, 0) if substeps > 1 else h

    carry_ref[...] = lax.fori_loop(0, groups, step, carry_ref[...], unroll=4)

    if fused_out:
        for slab in range(LANE_SLABS):
            lanes = slice(slab * LANES, (slab + 1) * LANES)
            z_ref[:, lanes] = ((hf_ref[slab] + h_ref[slab]) * gate_ref[slab]).astype(BF16)
        res = _dot(z_ref[...], wo_ref[...])
        for slab in range(LANE_SLABS):
            res_ref[slab] = res[:, slab * LANES:(slab + 1) * LANES]
        ts = rows // batch
        for b in range(batch):
            for slab in range(LANE_SLABS):
                lanes = slice(slab * LANES, (slab + 1) * LANES)
                o_ref[b, :, lanes] = x_ref[b, :, lanes] + res_ref[slab, pl.ds(b, ts, stride=batch), :]


def _lru_scan(xc, wa, ba, wx, bx, lam, batch, rows, reverse, fused=None):
    slabs, n, lanes = xc.shape
    w = slabs * lanes
    n_tiles = n // rows
    tile = _slab_spec(rows, n_tiles, reverse)
    row = _const_spec((1, w))
    gate_w = _const_spec((LRU_BLOCKS, LRU_BLOCK_W, LRU_BLOCK_W))
    in_specs = [tile, gate_w, row, gate_w, row, row]
    args = [xc, wa, ba, wx, bx, lam]
    scratch = [pltpu.VMEM((SUBLANES, w), F32), pltpu.VMEM((rows, w), F32), pltpu.VMEM((rows, w), F32)]
    if fused is None:
        out_specs = tile
        out_shape = jax.ShapeDtypeStruct(xc.shape, F32)
    else:
        h_other, gate, x, w_out = fused
        _, s, d = x.shape
        x_tile, _, _ = _batch_tile_specs(batch, s, rows // batch, d, reverse)
        in_specs += [tile, tile, x_tile, _const_spec(w_out.shape)]
        args += [h_other, gate, x, w_out]
        out_specs = x_tile
        out_shape = jax.ShapeDtypeStruct(x.shape, F32)
        scratch += [pltpu.VMEM((slabs, rows, lanes), F32), pltpu.VMEM((rows, w), BF16),
                    pltpu.VMEM((slabs, rows, lanes), F32)]
    kern = functools.partial(_lru_scan_kernel, reverse=reverse, rows=rows, batch=batch,
                             fused_out=fused is not None)
    return pl.pallas_call(
        kern,
        grid=(n_tiles,),
        in_specs=in_specs,
        out_specs=out_specs,
        out_shape=out_shape,
        scratch_shapes=scratch,
        compiler_params=_params(("arbitrary",)),
        name=("lru_scan_bwd_out" if fused is not None else "lru_scan_bwd") if reverse else "lru_scan_fwd",
    )(*args)


def _ffn_kernel(*refs, n_tiles, ts, final):
    if final:
        (x_ref, xp_ref, xn_ref, g_ref, wu_ref, wv_ref, cw_ref, cb_ref, wo_ref, gf_ref,
         o_ref, xs_ref, act_ref) = refs
    else:
        (x_ref, xp_ref, xn_ref, g_ref, wu_ref, wv_ref, cw_ref, cb_ref, wo_ref,
         o_ref, xs_ref, act_ref) = refs
    x = x_ref[0]
    _store_normed_tile_with_halo(x, xp_ref, xn_ref, g_ref[...], xs_ref, pl.program_id(1), n_tiles, ts)
    for c in range(D_FF // FF_CHUNK):
        cols = slice(c * FF_CHUNK, (c + 1) * FF_CHUNK)
        u = _dot(xs_ref[...], wu_ref[:, cols])
        v = _dot(xs_ref[0:ts], wv_ref[:, cols])
        y = (_time_shift(u, -1, ts) * cw_ref[0:1, cols] + u[0:ts] * cw_ref[1:2, cols]
             + _time_shift(u, 1, ts) * cw_ref[2:3, cols] + cb_ref[:, cols])
        act_ref[:, cols] = (_gelu(y) * v).astype(BF16)

    out = x + _dot(act_ref[...], wo_ref[...])
    if final:
        out = _rms(out, gf_ref[...])
    o_ref[0] = out


def _ffn(x, g, wu, wv, cw, cb, wo, ts, final_g=None):
    b, s, d = x.shape
    n_tiles = s // ts
    tile, prev, nxt = _tile_and_halo_specs(s, ts, d)
    final = final_g is not None
    in_specs = [tile, prev, nxt, _const_spec((1, d)), _const_spec(wu.shape), _const_spec(wv.shape),
                _const_spec(cw.shape), _const_spec(cb.shape), _const_spec(wo.shape)]
    args = [x, x, x, g, wu, wv, cw, cb, wo]
    if final:
        in_specs.append(_const_spec((1, d)))
        args.append(final_g)
    kern = functools.partial(_ffn_kernel, n_tiles=n_tiles, ts=ts, final=final)
    return pl.pallas_call(
        kern,
        grid=(b, n_tiles),
        in_specs=in_specs,
        out_specs=tile,
        out_shape=jax.ShapeDtypeStruct((b, s, d), F32),
        scratch_shapes=[pltpu.VMEM((ts + 2 * HALO, d), BF16), pltpu.VMEM((ts, D_FF), BF16)],
        compiler_params=_params(BATCH_THEN_TILES),
        name="ffn_final" if final else "ffn",
    )(*args)


def _chunk_pos(rows):
    return (lax.broadcasted_iota(jnp.int32, (rows, 1), 0) % RET_CHUNK).astype(F32)


def _split_lanes(ref, parts):
    width = ref.shape[-1] // parts
    return [ref.at[:, :, i * width:(i + 1) * width] for i in range(parts)]


def _ret_proj_kernel(lg_ref, x_ref, g_ref, w_ref, cos_ref, sin_ref,
                     qkk_ref, v_ref, so_ref, xs_ref, kzb_ref, s_ref, *, ts):
    q_ref, k_ref, kz_ref = _split_lanes(qkk_ref, 3)
    sg_ref, ob_ref = _split_lanes(so_ref, 2)

    @pl.when(pl.program_id(1) == 0)
    def _():
        s_ref[...] = jnp.zeros_like(s_ref)

    xs_ref[...] = _rms(x_ref[0], g_ref[...]).astype(BF16)
    cos = cos_ref[...]
    sin = sin_ref[...]
    half = RET_DK // 2
    pos = _chunk_pos(ts)

    def rotary(t):
        t1 = t[:, :half]
        t2 = t[:, half:]
        return t1 * cos - t2 * sin, t2 * cos + t1 * sin

    for h in range(RET_HEADS):
        lo = h * RET_DK
        mid = lo + half
        hi = lo + RET_DK
        q1, q2 = rotary(_dot(xs_ref[...], w_ref[:, lo:hi]))
        q_ref[0, :, lo:mid] = q1.astype(BF16)
        q_ref[0, :, mid:hi] = q2.astype(BF16)
        k1, k2 = rotary(_dot(xs_ref[...], w_ref[:, RET_QK + lo:RET_QK + hi]) * (RET_DK ** -0.5))
        k_ref[0, :, lo:mid] = k1.astype(BF16)
        k_ref[0, :, mid:hi] = k2.astype(BF16)
        zeta_f = jnp.exp(lg_ref[h] * (RET_CHUNK - 1.0 - pos))
        kz_ref[0, :, lo:mid] = (k1 * zeta_f).astype(BF16)
        kz_ref[0, :, mid:hi] = (k2 * zeta_f).astype(BF16)
        zeta_b = jnp.exp(lg_ref[RET_HEADS + h] * pos)
        kzb_ref[:, lo:mid] = (k1 * zeta_b).astype(BF16)
        kzb_ref[:, mid:hi] = (k2 * zeta_b).astype(BF16)
    for h in range(RET_HEADS):
        vv = slice(h * RET_DV, (h + 1) * RET_DV)
        lo = 2 * RET_QK + h * RET_DV
        v_ref[0, :, vv] = _dot(xs_ref[...], w_ref[:, lo:lo + RET_DV]).astype(BF16)
        lo = 2 * RET_QK + RET_V + h * RET_DV
        sg_ref[0, :, vv] = jax.nn.silu(_dot(xs_ref[...], w_ref[:, lo:lo + RET_DV]))

    cpos = _chunk_pos(RET_CHUNK)
    for c in reversed(range(ts // RET_CHUNK)):
        rows = slice(c * RET_CHUNK, (c + 1) * RET_CHUNK)
        for h in range(RET_HEADS):
            lg = lg_ref[RET_HEADS + h]
            xi = jnp.exp(lg * (RET_CHUNK - cpos))
            g_chunk = jnp.exp(jnp.full((1, 1), lg * RET_CHUNK, F32))
            qk = slice(h * RET_DK, (h + 1) * RET_DK)
            vv = slice(h * RET_DV, (h + 1) * RET_DV)
            state = s_ref[h]
            ob_ref[0, rows, vv] = _dot(q_ref[0, rows, qk], state.astype(BF16)) * xi
            s_ref[h] = state * g_chunk + _dot_tn(kzb_ref[rows, qk], v_ref[0, rows, vv])


def _ret_proj(log_g, x, g, w_in, cos, sin, ts):
    b, s, d = x.shape
    n_tiles = s // ts
    tile = lambda width: pl.BlockSpec((1, ts, width), lambda bi, i: (bi, n_tiles - 1 - i, 0))
    rope = pl.BlockSpec((ts, RET_DK // 2), lambda bi, i: (n_tiles - 1 - i, 0))
    act = lambda width, dtype: jax.ShapeDtypeStruct((b, s, width), dtype)
    return pl.pallas_call(
        functools.partial(_ret_proj_kernel, ts=ts),
        grid=(b, n_tiles),
        in_specs=[pl.BlockSpec(memory_space=pltpu.SMEM), tile(d), _const_spec((1, d)), _const_spec(w_in.shape),
                  rope, rope],
        out_specs=[tile(3 * RET_QK), tile(RET_V), tile(2 * RET_V)],
        out_shape=[act(3 * RET_QK, BF16), act(RET_V, BF16), act(2 * RET_V, F32)],
        scratch_shapes=[pltpu.VMEM((ts, d), BF16), pltpu.VMEM((ts, RET_QK), BF16),
                        pltpu.VMEM((RET_HEADS, RET_DK, RET_DV), F32)],
        compiler_params=_params(BATCH_THEN_TILES),
        name="ret_proj",
    )(log_g, x, g, w_in, cos, sin)


def _ret_fwd_kernel(lg_ref, qkk_ref, v_ref, so_ref, x_ref, ng_ref, w_ref, o_ref, s_ref, z_ref, *, ts):
    q_ref, k_ref, kz_ref = _split_lanes(qkk_ref, 3)
    sg_ref, ob_ref = _split_lanes(so_ref, 2)

    @pl.when(pl.program_id(1) == 0)
    def _():
        s_ref[...] = jnp.zeros_like(s_ref)

    cpos = _chunk_pos(RET_CHUNK)
    n_idx = lax.broadcasted_iota(jnp.int32, (RET_CHUNK, RET_CHUNK), 0)
    m_idx = lax.broadcasted_iota(jnp.int32, (RET_CHUNK, RET_CHUNK), 1)
    diff = (n_idx - m_idx).astype(F32)
    for h in range(RET_HEADS):
        lf = lg_ref[h]
        lb = lg_ref[RET_HEADS + h]
        decay = jnp.where(diff >= 0.0, jnp.exp(lf * jnp.maximum(diff, 0.0)),
                          jnp.exp(lb * jnp.maximum(-diff, 0.0)))
        xi = jnp.exp(lf * (cpos + 1.0))
        g_chunk = jnp.exp(jnp.full((1, 1), lf * RET_CHUNK, F32))
        qk = slice(h * RET_DK, (h + 1) * RET_DK)
        vv = slice(h * RET_DV, (h + 1) * RET_DV)
        for c in range(ts // RET_CHUNK):
            rows = slice(c * RET_CHUNK, (c + 1) * RET_CHUNK)
            qc = q_ref[0, rows, qk]
            vc = v_ref[0, rows, vv]
            state = s_ref[h]
            scores = _dot_nt(qc, k_ref[0, rows, qk]) * decay
            y = (_dot(scores.astype(BF16), vc) + _dot(qc, state.astype(BF16)) * xi) + ob_ref[0, rows, vv]
            s_ref[h] = state * g_chunk + _dot_tn(kz_ref[0, rows, qk], vc)
            y = y * lax.rsqrt(jnp.mean(y * y, axis=-1, keepdims=True) + EPS)
            y = y * ng_ref[:, vv]
            z_ref[rows, vv] = (sg_ref[0, rows, vv] * y).astype(BF16)

    o_ref[0] = x_ref[0] + _dot(z_ref[...], w_ref[...])


def _ret_fwd(log_g, qkk, v, so, x, ng, w_out, ts):
    b, s, d = x.shape
    tile = lambda width: pl.BlockSpec((1, ts, width), lambda bi, i: (bi, i, 0))
    return pl.pallas_call(
        functools.partial(_ret_fwd_kernel, ts=ts),
        grid=(b, s // ts),
        in_specs=[pl.BlockSpec(memory_space=pltpu.SMEM), tile(3 * RET_QK), tile(RET_V), tile(2 * RET_V), tile(d),
                  _const_spec((1, RET_V)), _const_spec(w_out.shape)],
        out_specs=tile(d),
        out_shape=jax.ShapeDtypeStruct((b, s, d), F32),
        scratch_shapes=[pltpu.VMEM((RET_HEADS, RET_DK, RET_DV), F32), pltpu.VMEM((ts, RET_V), BF16)],
        compiler_params=_params(BATCH_THEN_TILES),
        name="ret_fwd",
    )(log_g, qkk, v, so, x, ng, w_out)


def _prepare(norm_mix, norm_ffn, norm_final, lru_w_in, lru_conv_w, lru_conv_b, lru_w_a, lru_b_a,
             lru_w_x, lru_b_x, lru_lambda, lru_w_out, ret_w_in, ret_decay_logit, ret_norm, ret_w_out,
             ffn_w_in, ffn_conv_w, ffn_conv_b, ffn_w_out, seq):
    ffn = []
    for i in range(2):
        w_in = ffn_w_in[i].astype(BF16)
        ffn.append(dict(
            g=norm_ffn[i][None, :],
            wu=w_in[:, :D_FF],
            wv=w_in[:, D_FF:],
            cw=ffn_conv_w[i],
            cb=ffn_conv_b[i][None, :],
            wo=ffn_w_out[i].astype(BF16),
        ))
    half = RET_DK // 2
    theta = ROPE_BASE ** (-jnp.arange(half, dtype=F32) / half)
    ang = jnp.arange(seq, dtype=F32)[:, None] * theta[None, :]
    return dict(
        ffn=ffn,
        norm_mix=[norm_mix[0][None, :], norm_mix[1][None, :]],
        norm_final=norm_final[None, :],
        lru_w_in=lru_w_in[0].astype(BF16),
        lru_cw=0.5 * lru_conv_w[0],
        lru_cb=0.5 * lru_conv_b[0][None, :],
        lru_wa=[lru_w_a[0, d].astype(BF16) for d in range(2)],
        lru_ba=[0.5 * lru_b_a[0, d][None, :] for d in range(2)],
        lru_wx=[lru_w_x[0, d].astype(BF16) for d in range(2)],
        lru_bx=[0.5 * lru_b_x[0, d][None, :] for d in range(2)],
        lru_lam=[lru_lambda[0, d][None, :] for d in range(2)],
        lru_w_out=lru_w_out[0].astype(BF16),
        ret_w_in=ret_w_in[0].astype(BF16),
        ret_log_g=jax.nn.log_sigmoid(ret_decay_logit[0].astype(F32)).reshape(2 * RET_HEADS),
        ret_norm=ret_norm[0][None, :],
        ret_w_out=ret_w_out[0].astype(BF16),
        cos=jnp.cos(ang),
        sin=jnp.sin(ang),
    )


def _encoder(x, p, ts):
    b, s, d = x.shape
    in_ts = min(ts, LRU_IN_ROWS // b)
    ffn_ts = ts * (FFN_TILE // SEQ_TILE)
    gate, xc = _lru_in(x, p["norm_mix"][0], p["lru_w_in"], p["lru_cw"], p["lru_cb"], in_ts)
    scan = lambda di, rows, fused: _lru_scan(xc, p["lru_wa"][di], p["lru_ba"][di], p["lru_wx"][di],
                                             p["lru_bx"][di], p["lru_lam"][di], b, min(rows, ts * b),
                                             reverse=bool(di), fused=fused)
    hf = scan(0, SCAN_ROWS_FWD, None)
    x = scan(1, SCAN_ROWS_BWD, (hf, gate, x, p["lru_w_out"]))
    f = p["ffn"][0]
    x = _ffn(x, f["g"], f["wu"], f["wv"], f["cw"], f["cb"], f["wo"], ffn_ts)
    qkk, v, so = _ret_proj(p["ret_log_g"], x, p["norm_mix"][1], p["ret_w_in"], p["cos"], p["sin"], ts)
    x = _ret_fwd(p["ret_log_g"], qkk, v, so, x, p["ret_norm"], p["ret_w_out"], ts)
    f = p["ffn"][1]
    return _ffn(x, f["g"], f["wu"], f["wv"], f["cw"], f["cb"], f["wo"], ffn_ts, final_g=p["norm_final"])


def kernel(x_prompt, x_sample, norm_mix, norm_ffn, norm_final, lru_w_in, lru_conv_w, lru_conv_b, lru_w_a, lru_b_a, lru_w_x, lru_b_x, lru_lambda, lru_w_out, ret_w_in, ret_decay_logit, ret_norm, ret_w_out, ffn_w_in, ffn_conv_w, ffn_conv_b, ffn_w_out):
    assert x_prompt.shape[1] == x_sample.shape[1] and x_prompt.shape[1] % FFN_TILE == 0
    assert all(SUBLANES % x.shape[0] == 0 for x in (x_prompt, x_sample))
    p = _prepare(norm_mix, norm_ffn, norm_final, lru_w_in, lru_conv_w, lru_conv_b, lru_w_a, lru_b_a,
                 lru_w_x, lru_b_x, lru_lambda, lru_w_out, ret_w_in, ret_decay_logit, ret_norm,
                 ret_w_out, ffn_w_in, ffn_conv_w, ffn_conv_b, ffn_w_out, x_prompt.shape[1])
    return (_encoder(x_prompt, p, SEQ_TILE), _encoder(x_sample, p, SEQ_TILE))
```

```python
import functools

import jax
import jax.numpy as jnp
from jax import lax
from jax.experimental import pallas as pl
from jax.experimental.pallas import tpu as pltpu

F32 = jnp.float32
BF16 = jnp.bfloat16

EPS = 1e-6
D_MODEL = 1024
LRU_BLOCKS = 4
LRU_BLOCK_W = D_MODEL // LRU_BLOCKS
LRU_C = 8.0
RET_HEADS = 4
RET_DK = 256
RET_DV = 512
RET_QK = RET_HEADS * RET_DK
RET_V = RET_HEADS * RET_DV
RET_CHUNK = 256
ROPE_BASE = 10000.0
LOG2_E = 1.4426950408889634
D_FF = 2816

SUBLANES = 8
LANES = 128
LANE_SLABS = D_MODEL // LANES
HALO = SUBLANES
SEQ_TILE = 512
FFN_TILE = 1024
LRU_IN_ROWS = 1024
SCAN_ROWS_BWD = 512
FF_CHUNK = 256
VMEM_LIMIT_BYTES = 56 * 1024 * 1024


def _params(semantics):
    return pltpu.CompilerParams(dimension_semantics=semantics, vmem_limit_bytes=VMEM_LIMIT_BYTES)


BATCH_THEN_TILES = ("parallel", "arbitrary")


def _const_spec(shape):
    zeros = (0,) * len(shape)
    return pl.BlockSpec(shape, lambda *_: zeros, pipeline_mode=pl.Buffered(1))


def _rms(x, g):
    return x * lax.rsqrt(jnp.mean(x * x, axis=-1, keepdims=True) + EPS) * g


def _gelu(x):
    return jax.nn.gelu(x, approximate=True)


def _sqrt_nonneg(x):
    return jnp.where(x > 0.0, x * lax.rsqrt(x), 0.0)


def _dot(a, b):
    return jnp.dot(a, b, preferred_element_type=F32)


def _dot_tn(a, b):
    return lax.dot_general(a, b, (((0,), (0,)), ((), ())), preferred_element_type=F32)


def _dot_nt(a, b):
    return lax.dot_general(a, b, (((1,), (1,)), ((), ())), preferred_element_type=F32)


def _tile_and_halo_specs(s, ts, d, batch_first=True):
    halo_per_tile = ts // HALO
    n_halo = s // HALO

    def spec(rows, tile_to_block):
        if batch_first:
            return pl.BlockSpec((1, rows, d), lambda bi, i: (bi, tile_to_block(i), 0))
        return pl.BlockSpec((1, rows, d), lambda i, bi: (bi, tile_to_block(i), 0))

    tile = spec(ts, lambda i: i)
    prev = spec(HALO, lambda i: jnp.maximum(i * halo_per_tile - 1, 0))
    nxt = spec(HALO, lambda i: jnp.minimum((i + 1) * halo_per_tile, n_halo - 1))
    return tile, prev, nxt


def _store_normed_tile_with_halo(x, xp_ref, xn_ref, g, xs_ref, j, n_tiles, ts):
    xs_ref[0:ts] = _rms(x, g).astype(BF16)
    nxt = jnp.where(j == n_tiles - 1, 0.0, _rms(xn_ref[0], g))
    prv = jnp.where(j == 0, 0.0, _rms(xp_ref[0], g))
    xs_ref[ts:ts + 2 * HALO] = jnp.concatenate([nxt, prv], axis=0).astype(BF16)


def _time_shift(ext, k, ts):
    return pltpu.roll(ext, (-k) % ext.shape[0], 0)[0:ts]


def _batch_tile_specs(batch, s, ts, d, reverse=False):
    n_tiles = s // ts
    halo_per_tile = ts // HALO
    n_halo = s // HALO

    def tile_of(i):
        return (n_tiles - 1 - i) if reverse else i

    tile = pl.BlockSpec((batch, ts, d), lambda i: (0, tile_of(i), 0))
    prev = pl.BlockSpec((batch, HALO, d), lambda i: (0, jnp.maximum(tile_of(i) * halo_per_tile - 1, 0), 0))
    nxt = pl.BlockSpec((batch, HALO, d),
                       lambda i: (0, jnp.minimum((tile_of(i) + 1) * halo_per_tile, n_halo - 1), 0))
    return tile, prev, nxt


def _slab_spec(rows, n_tiles, reverse=False):
    return pl.BlockSpec((LANE_SLABS, rows, LANES), lambda i: (0, (n_tiles - 1 - i) if reverse else i, 0))


def _lru_in_kernel(x_ref, xp_ref, xn_ref, g_ref, w_ref, cw_ref, cb_ref, wa_ref, ba_ref, wx_ref, bx_ref, lam_ref,
                   gate_ref, xc_ref, hf_ref, xs_ref, carry_ref, a_ref, u_ref, *, n_tiles, ts, batch):
    w = D_MODEL
    j = pl.program_id(0)
    g = g_ref[...]
    body = batch * ts
    halo = 2 * HALO
    for b in range(batch):
        xs_ref[b * ts:(b + 1) * ts] = _rms(x_ref[b], g).astype(BF16)
        nxt = jnp.where(j == n_tiles - 1, 0.0, _rms(xn_ref[b], g))
        prv = jnp.where(j == 0, 0.0, _rms(xp_ref[b], g))
        xs_ref[body + b * halo:body + (b + 1) * halo] = jnp.concatenate([nxt, prv], axis=0).astype(BF16)

    slabs_per_block = LRU_BLOCK_W // LANES
    for nb in range(LRU_BLOCKS):
        sl = slice(nb * LRU_BLOCK_W, (nb + 1) * LRU_BLOCK_W)
        gate = _gelu(_dot(xs_ref[0:body], w_ref[:, sl]))
        rec = _dot(xs_ref[...], w_ref[:, w + nb * LRU_BLOCK_W:w + (nb + 1) * LRU_BLOCK_W])
        for b in range(batch):
            ext = jnp.concatenate([rec[b * ts:(b + 1) * ts], rec[body + b * halo:body + (b + 1) * halo]], axis=0)
            xc = (_time_shift(ext, -2, ts) * cw_ref[0:1, sl] + _time_shift(ext, -1, ts) * cw_ref[1:2, sl]
                  + ext[0:ts] * cw_ref[2:3, sl] + _time_shift(ext, 1, ts) * cw_ref[3:4, sl] + cb_ref[:, sl])
            rows = pl.ds(b, ts, stride=batch)
            for k in range(slabs_per_block):
                slab = nb * slabs_per_block + k
                lanes = slice(k * LANES, (k + 1) * LANES)
                xc_ref[slab, rows, :] = xc[:, lanes]
                gate_ref[slab, rows, :] = gate[b * ts:(b + 1) * ts, lanes]

    _lru_gates_and_scan(xc_ref, wa_ref, ba_ref, wx_ref, bx_ref, lam_ref, hf_ref, carry_ref, a_ref, u_ref,
                        reverse=False, rows=body, batch=batch)


def _lru_in(x, g, w_in, cw, cb, wa, ba, wx, bx, lam, ts):
    b, s, d = x.shape
    n_tiles = s // ts
    rows = ts * b
    tile, prev, nxt = _batch_tile_specs(b, s, ts, d)
    slabs = _slab_spec(rows, n_tiles)
    out = jax.ShapeDtypeStruct((LANE_SLABS, s * b, LANES), F32)
    row = _const_spec((1, d))
    gate_w = _const_spec((LRU_BLOCKS, LRU_BLOCK_W, LRU_BLOCK_W))
    return pl.pallas_call(
        functools.partial(_lru_in_kernel, n_tiles=n_tiles, ts=ts, batch=b),
        grid=(n_tiles,),
        in_specs=[tile, prev, nxt, row, _const_spec((d, 2 * d)), _const_spec(cw.shape), row,
                  gate_w, row, gate_w, row, row],
        out_specs=[slabs, slabs, slabs],
        out_shape=[out, out, out],
        scratch_shapes=[pltpu.VMEM((b * (ts + 2 * HALO), d), BF16), pltpu.VMEM((SUBLANES, d), F32),
                        pltpu.VMEM((rows, d), F32), pltpu.VMEM((rows, d), F32)],
        compiler_params=_params(("arbitrary",)),
        name="lru_in",
    )(x, x, x, g, w_in, cw, cb, wa, ba, wx, bx, lam)


def _lru_gates_and_scan(xc_ref, wa_ref, ba_ref, wx_ref, bx_ref, lam_ref, h_ref, carry_ref, a_ref, u_ref,
                        *, reverse, rows, batch):
    @pl.when(pl.program_id(0) == 0)
    def _():
        carry_ref[...] = jnp.zeros_like(carry_ref)

    neg_lam = -lam_ref[...]
    softplus = jnp.maximum(neg_lam, 0.0) + jnp.log1p(jnp.exp(-jnp.abs(neg_lam)))
    decay = (0.5 * LRU_C) * softplus
    slabs_per_block = LRU_BLOCK_W // LANES
    for nb in range(LRU_BLOCKS):
        sl = slice(nb * LRU_BLOCK_W, (nb + 1) * LRU_BLOCK_W)
        xh = jnp.concatenate([xc_ref[nb * slabs_per_block + k] for k in range(slabs_per_block)], axis=1)
        xh16 = xh.astype(BF16)
        tr = jnp.tanh(_dot(xh16, wa_ref[nb]) + ba_ref[:, sl])
        ti = jnp.tanh(_dot(xh16, wx_ref[nb]) + bx_ref[:, sl])
        neg_log_a = decay[:, sl] * (tr + 1.0)
        a = jnp.exp2(neg_log_a * (-LOG2_E))
        one_minus_a2 = jnp.tanh(neg_log_a) * (a * a + 1.0)
        a_ref[:, sl] = a
        u_ref[:, sl] = _sqrt_nonneg(one_minus_a2) * (xh * (ti + 1.0))

    groups = rows // SUBLANES
    substeps = SUBLANES // batch
    shift = (SUBLANES - batch) if reverse else batch % SUBLANES
    sub = lax.broadcasted_iota(jnp.int32, (SUBLANES, D_MODEL), 0)

    def step(gi, c):
        g = (groups - 1 - gi) if reverse else gi
        r = pl.ds(pl.multiple_of(g * SUBLANES, SUBLANES), SUBLANES)
        a8 = a_ref[r, :]
        u8 = u_ref[r, :]
        h = a8 * c + u8
        out = h
        for k in range(1, substeps):
            h = a8 * pltpu.roll(h, shift, 0) + u8
            if reverse:
                out = jnp.where(sub < (substeps - k) * batch, h, out)
            else:
                out = jnp.where(sub >= k * batch, h, out)
        for slab in range(LANE_SLABS):
            h_ref[slab, r, :] = out[:, slab * LANES:(slab + 1) * LANES]
        return pltpu.roll(h, shift, 0) if substeps > 1 else h

    carry_ref[...] = lax.fori_loop(0, groups, step, carry_ref[...], unroll=4)


def _lru_bwd_out_kernel(xc_ref, wa_ref, ba_ref, wx_ref, bx_ref, lam_ref, hf_ref, gate_ref, x_ref, wo_ref,
                        o_ref, carry_ref, a_ref, u_ref, hb_ref, z_ref, res_ref, *, rows, batch):
    _lru_gates_and_scan(xc_ref, wa_ref, ba_ref, wx_ref, bx_ref, lam_ref, hb_ref, carry_ref, a_ref, u_ref,
                        reverse=True, rows=rows, batch=batch)
    for slab in range(LANE_SLABS):
        lanes = slice(slab * LANES, (slab + 1) * LANES)
        z_ref[:, lanes] = ((hf_ref[slab] + hb_ref[slab]) * gate_ref[slab]).astype(BF16)
    res = _dot(z_ref[...], wo_ref[...])
    for slab in range(LANE_SLABS):
        res_ref[slab] = res[:, slab * LANES:(slab + 1) * LANES]
    ts = rows // batch
    for b in range(batch):
        for slab in range(LANE_SLABS):
            lanes = slice(slab * LANES, (slab + 1) * LANES)
            o_ref[b, :, lanes] = x_ref[b, :, lanes] + res_ref[slab, pl.ds(b, ts, stride=batch), :]


def _lru_bwd_out(xc, wa, ba, wx, bx, lam, hf, gate, x, w_out, rows):
    slabs, n, lanes = xc.shape
    batch, s, d = x.shape
    n_tiles = n // rows
    tile = _slab_spec(rows, n_tiles, reverse=True)
    x_tile, _, _ = _batch_tile_specs(batch, s, rows // batch, d, reverse=True)
    row = _const_spec((1, d))
    gate_w = _const_spec((LRU_BLOCKS, LRU_BLOCK_W, LRU_BLOCK_W))
    return pl.pallas_call(
        functools.partial(_lru_bwd_out_kernel, rows=rows, batch=batch),
        grid=(n_tiles,),
        in_specs=[tile, gate_w, row, gate_w, row, row, tile, tile, x_tile, _const_spec(w_out.shape)],
        out_specs=x_tile,
        out_shape=jax.ShapeDtypeStruct(x.shape, F32),
        scratch_shapes=[pltpu.VMEM((SUBLANES, d), F32), pltpu.VMEM((rows, d), F32), pltpu.VMEM((rows, d), F32),
                        pltpu.VMEM((slabs, rows, lanes), F32), pltpu.VMEM((rows, d), BF16),
                        pltpu.VMEM((slabs, rows, lanes), F32)],
        compiler_params=_params(("arbitrary",)),
        name="lru_bwd_out",
    )(xc, wa, ba, wx, bx, lam, hf, gate, x, w_out)


def _ffn_kernel(*refs, n_tiles, ts, final):
    if final:
        (x_ref, xp_ref, xn_ref, g_ref, wu_ref, wv_ref, cw_ref, cb_ref, wo_ref, gf_ref,
         o_ref, xs_ref, act_ref) = refs
    else:
        (x_ref, xp_ref, xn_ref, g_ref, wu_ref, wv_ref, cw_ref, cb_ref, wo_ref,
         o_ref, xs_ref, act_ref) = refs
    x = x_ref[0]
    _store_normed_tile_with_halo(x, xp_ref, xn_ref, g_ref[...], xs_ref, pl.program_id(1), n_tiles, ts)
    for c in range(D_FF // FF_CHUNK):
        cols = slice(c * FF_CHUNK, (c + 1) * FF_CHUNK)
        u = _dot(xs_ref[...], wu_ref[:, cols])
        v = _dot(xs_ref[0:ts], wv_ref[:, cols])
        y = (_time_shift(u, -1, ts) * cw_ref[0:1, cols] + u[0:ts] * cw_ref[1:2, cols]
             + _time_shift(u, 1, ts) * cw_ref[2:3, cols] + cb_ref[:, cols])
        act_ref[:, cols] = (_gelu(y) * v).astype(BF16)

    out = x + _dot(act_ref[...], wo_ref[...])
    if final:
        out = _rms(out, gf_ref[...])
    o_ref[0] = out


def _ffn(x, g, wu, wv, cw, cb, wo, ts, final_g=None):
    b, s, d = x.shape
    n_tiles = s // ts
    tile, prev, nxt = _tile_and_halo_specs(s, ts, d)
    final = final_g is not None
    in_specs = [tile, prev, nxt, _const_spec((1, d)), _const_spec(wu.shape), _const_spec(wv.shape),
                _const_spec(cw.shape), _const_spec(cb.shape), _const_spec(wo.shape)]
    args = [x, x, x, g, wu, wv, cw, cb, wo]
    if final:
        in_specs.append(_const_spec((1, d)))
        args.append(final_g)
    kern = functools.partial(_ffn_kernel, n_tiles=n_tiles, ts=ts, final=final)
    return pl.pallas_call(
        kern,
        grid=(b, n_tiles),
        in_specs=in_specs,
        out_specs=tile,
        out_shape=jax.ShapeDtypeStruct((b, s, d), F32),
        scratch_shapes=[pltpu.VMEM((ts + 2 * HALO, d), BF16), pltpu.VMEM((ts, D_FF), BF16)],
        compiler_params=_params(BATCH_THEN_TILES),
        name="ffn_final" if final else "ffn",
    )(*args)


def _chunk_pos(rows):
    return (lax.broadcasted_iota(jnp.int32, (rows, 1), 0) % RET_CHUNK).astype(F32)


def _split_lanes(ref, parts):
    width = ref.shape[-1] // parts
    return [ref.at[:, :, i * width:(i + 1) * width] for i in range(parts)]


def _ret_proj_kernel(lg_ref, x_ref, g_ref, w_ref, cos_ref, sin_ref,
                     qkk_ref, v_ref, so_ref, xs_ref, kzb_ref, s_ref, *, ts):
    q_ref, k_ref, kz_ref = _split_lanes(qkk_ref, 3)
    sg_ref, ob_ref = _split_lanes(so_ref, 2)

    @pl.when(pl.program_id(1) == 0)
    def _():
        s_ref[...] = jnp.zeros_like(s_ref)

    xs_ref[...] = _rms(x_ref[0], g_ref[...]).astype(BF16)
    cos = cos_ref[...]
    sin = sin_ref[...]
    half = RET_DK // 2
    pos = _chunk_pos(ts)

    def rotary(t):
        t1 = t[:, :half]
        t2 = t[:, half:]
        return t1 * cos - t2 * sin, t2 * cos + t1 * sin

    for h in range(RET_HEADS):
        lo = h * RET_DK
        mid = lo + half
        hi = lo + RET_DK
        q1, q2 = rotary(_dot(xs_ref[...], w_ref[:, lo:hi]))
        q_ref[0, :, lo:mid] = q1.astype(BF16)
        q_ref[0, :, mid:hi] = q2.astype(BF16)
        k1, k2 = rotary(_dot(xs_ref[...], w_ref[:, RET_QK + lo:RET_QK + hi]) * (RET_DK ** -0.5))
        k_ref[0, :, lo:mid] = k1.astype(BF16)
        k_ref[0, :, mid:hi] = k2.astype(BF16)
        zeta_f = jnp.exp(lg_ref[h] * (RET_CHUNK - 1.0 - pos))
        kz_ref[0, :, lo:mid] = (k1 * zeta_f).astype(BF16)
        kz_ref[0, :, mid:hi] = (k2 * zeta_f).astype(BF16)
        zeta_b = jnp.exp(lg_ref[RET_HEADS + h] * pos)
        kzb_ref[:, lo:mid] = (k1 * zeta_b).astype(BF16)
        kzb_ref[:, mid:hi] = (k2 * zeta_b).astype(BF16)
    for h in range(RET_HEADS):
        vv = slice(h * RET_DV, (h + 1) * RET_DV)
        lo = 2 * RET_QK + h * RET_DV
        v_ref[0, :, vv] = _dot(xs_ref[...], w_ref[:, lo:lo + RET_DV]).astype(BF16)
        lo = 2 * RET_QK + RET_V + h * RET_DV
        sg_ref[0, :, vv] = jax.nn.silu(_dot(xs_ref[...], w_ref[:, lo:lo + RET_DV]))

    cpos = _chunk_pos(RET_CHUNK)
    for c in reversed(range(ts // RET_CHUNK)):
        rows = slice(c * RET_CHUNK, (c + 1) * RET_CHUNK)
        for h in range(RET_HEADS):
            lg = lg_ref[RET_HEADS + h]
            xi = jnp.exp(lg * (RET_CHUNK - cpos))
            g_chunk = jnp.exp(jnp.full((1, 1), lg * RET_CHUNK, F32))
            qk = slice(h * RET_DK, (h + 1) * RET_DK)
            vv = slice(h * RET_DV, (h + 1) * RET_DV)
            state = s_ref[h]
            ob_ref[0, rows, vv] = _dot(q_ref[0, rows, qk], state.astype(BF16)) * xi
            s_ref[h] = state * g_chunk + _dot_tn(kzb_ref[rows, qk], v_ref[0, rows, vv])


def _ret_proj(log_g, x, g, w_in, cos, sin, ts):
    b, s, d = x.shape
    n_tiles = s // ts
    tile = lambda width: pl.BlockSpec((1, ts, width), lambda bi, i: (bi, n_tiles - 1 - i, 0))
    rope = pl.BlockSpec((ts, RET_DK // 2), lambda bi, i: (n_tiles - 1 - i, 0))
    act = lambda width, dtype: jax.ShapeDtypeStruct((b, s, width), dtype)
    return pl.pallas_call(
        functools.partial(_ret_proj_kernel, ts=ts),
        grid=(b, n_tiles),
        in_specs=[pl.BlockSpec(memory_space=pltpu.SMEM), tile(d), _const_spec((1, d)), _const_spec(w_in.shape),
                  rope, rope],
        out_specs=[tile(3 * RET_QK), tile(RET_V), tile(2 * RET_V)],
        out_shape=[act(3 * RET_QK, BF16), act(RET_V, BF16), act(2 * RET_V, F32)],
        scratch_shapes=[pltpu.VMEM((ts, d), BF16), pltpu.VMEM((ts, RET_QK), BF16),
                        pltpu.VMEM((RET_HEADS, RET_DK, RET_DV), F32)],
        compiler_params=_params(BATCH_THEN_TILES),
        name="ret_proj",
    )(log_g, x, g, w_in, cos, sin)


def _ret_fwd_kernel(lg_ref, qkk_ref, v_ref, so_ref, x_ref, ng_ref, w_ref, o_ref, s_ref, z_ref, *, ts):
    q_ref, k_ref, kz_ref = _split_lanes(qkk_ref, 3)
    sg_ref, ob_ref = _split_lanes(so_ref, 2)

    @pl.when(pl.program_id(1) == 0)
    def _():
        s_ref[...] = jnp.zeros_like(s_ref)

    cpos = _chunk_pos(RET_CHUNK)
    n_idx = lax.broadcasted_iota(jnp.int32, (RET_CHUNK, RET_CHUNK), 0)
    m_idx = lax.broadcasted_iota(jnp.int32, (RET_CHUNK, RET_CHUNK), 1)
    diff = (n_idx - m_idx).astype(F32)
    for h in range(RET_HEADS):
        lf = lg_ref[h]
        lb = lg_ref[RET_HEADS + h]
        decay = jnp.where(diff >= 0.0, jnp.exp(lf * jnp.maximum(diff, 0.0)),
                          jnp.exp(lb * jnp.maximum(-diff, 0.0)))
        xi = jnp.exp(lf * (cpos + 1.0))
        g_chunk = jnp.exp(jnp.full((1, 1), lf * RET_CHUNK, F32))
        qk = slice(h * RET_DK, (h + 1) * RET_DK)
        vv = slice(h * RET_DV, (h + 1) * RET_DV)
        for c in range(ts // RET_CHUNK):
            rows = slice(c * RET_CHUNK, (c + 1) * RET_CHUNK)
            qc = q_ref[0, rows, qk]
            vc = v_ref[0, rows, vv]
            state = s_ref[h]
            scores = _dot_nt(qc, k_ref[0, rows, qk]) * decay
            y = (_dot(scores.astype(BF16), vc) + _dot(qc, state.astype(BF16)) * xi) + ob_ref[0, rows, vv]
            s_ref[h] = state * g_chunk + _dot_tn(kz_ref[0, rows, qk], vc)
            y = y * lax.rsqrt(jnp.mean(y * y, axis=-1, keepdims=True) + EPS)
            y = y * ng_ref[:, vv]
            z_ref[rows, vv] = (sg_ref[0, rows, vv] * y).astype(BF16)

    o_ref[0] = x_ref[0] + _dot(z_ref[...], w_ref[...])


def _ret_fwd(log_g, qkk, v, so, x, ng, w_out, ts):
    b, s, d = x.shape
    tile = lambda width: pl.BlockSpec((1, ts, width), lambda bi, i: (bi, i, 0))
    return pl.pallas_call(
        functools.partial(_ret_fwd_kernel, ts=ts),
        grid=(b, s // ts),
        in_specs=[pl.BlockSpec(memory_space=pltpu.SMEM), tile(3 * RET_QK), tile(RET_V), tile(2 * RET_V), tile(d),
                  _const_spec((1, RET_V)), _const_spec(w_out.shape)],
        out_specs=tile(d),
        out_shape=jax.ShapeDtypeStruct((b, s, d), F32),
        scratch_shapes=[pltpu.VMEM((RET_HEADS, RET_DK, RET_DV), F32), pltpu.VMEM((ts, RET_V), BF16)],
        compiler_params=_params(BATCH_THEN_TILES),
        name="ret_fwd",
    )(log_g, qkk, v, so, x, ng, w_out)


def _prepare(norm_mix, norm_ffn, norm_final, lru_w_in, lru_conv_w, lru_conv_b, lru_w_a, lru_b_a,
             lru_w_x, lru_b_x, lru_lambda, lru_w_out, ret_w_in, ret_decay_logit, ret_norm, ret_w_out,
             ffn_w_in, ffn_conv_w, ffn_conv_b, ffn_w_out, seq):
    ffn = []
    for i in range(2):
        w_in = ffn_w_in[i].astype(BF16)
        ffn.append(dict(
            g=norm_ffn[i][None, :],
            wu=w_in[:, :D_FF],
            wv=w_in[:, D_FF:],
            cw=ffn_conv_w[i],
            cb=ffn_conv_b[i][None, :],
            wo=ffn_w_out[i].astype(BF16),
        ))
    half = RET_DK // 2
    theta = ROPE_BASE ** (-jnp.arange(half, dtype=F32) / half)
    ang = jnp.arange(seq, dtype=F32)[:, None] * theta[None, :]
    return dict(
        ffn=ffn,
        norm_mix=[norm_mix[0][None, :], norm_mix[1][None, :]],
        norm_final=norm_final[None, :],
        lru_w_in=lru_w_in[0].astype(BF16),
        lru_cw=0.5 * lru_conv_w[0],
        lru_cb=0.5 * lru_conv_b[0][None, :],
        lru_wa=[lru_w_a[0, d].astype(BF16) for d in range(2)],
        lru_ba=[0.5 * lru_b_a[0, d][None, :] for d in range(2)],
        lru_wx=[lru_w_x[0, d].astype(BF16) for d in range(2)],
        lru_bx=[0.5 * lru_b_x[0, d][None, :] for d in range(2)],
        lru_lam=[lru_lambda[0, d][None, :] for d in range(2)],
        lru_w_out=lru_w_out[0].astype(BF16),
        ret_w_in=ret_w_in[0].astype(BF16),
        ret_log_g=jax.nn.log_sigmoid(ret_decay_logit[0].astype(F32)).reshape(2 * RET_HEADS),
        ret_norm=ret_norm[0][None, :],
        ret_w_out=ret_w_out[0].astype(BF16),
        cos=jnp.cos(ang),
        sin=jnp.sin(ang),
    )


def _encoder(x, p, ts):
    b, s, d = x.shape
    in_ts = min(ts, LRU_IN_ROWS // b)
    ffn_ts = ts * (FFN_TILE // SEQ_TILE)
    direction = lambda di: (p["lru_wa"][di], p["lru_ba"][di], p["lru_wx"][di], p["lru_bx"][di], p["lru_lam"][di])
    gate, xc, hf = _lru_in(x, p["norm_mix"][0], p["lru_w_in"], p["lru_cw"], p["lru_cb"], *direction(0), in_ts)
    x = _lru_bwd_out(xc, *direction(1), hf, gate, x, p["lru_w_out"], min(SCAN_ROWS_BWD, ts * b))
    f = p["ffn"][0]
    x = _ffn(x, f["g"], f["wu"], f["wv"], f["cw"], f["cb"], f["wo"], ffn_ts)
    qkk, v, so = _ret_proj(p["ret_log_g"], x, p["norm_mix"][1], p["ret_w_in"], p["cos"], p["sin"], ts)
    x = _ret_fwd(p["ret_log_g"], qkk, v, so, x, p["ret_norm"], p["ret_w_out"], ts)
    f = p["ffn"][1]
    return _ffn(x, f["g"], f["wu"], f["wv"], f["cw"], f["cb"], f["wo"], ffn_ts, final_g=p["norm_final"])


def kernel(x_prompt, x_sample, norm_mix, norm_ffn, norm_final, lru_w_in, lru_conv_w, lru_conv_b, lru_w_a, lru_b_a, lru_w_x, lru_b_x, lru_lambda, lru_w_out, ret_w_in, ret_decay_logit, ret_norm, ret_w_out, ffn_w_in, ffn_conv_w, ffn_conv_b, ffn_w_out):
    assert x_prompt.shape[1] == x_sample.shape[1] and x_prompt.shape[1] % FFN_TILE == 0
    assert all(SUBLANES % x.shape[0] == 0 for x in (x_prompt, x_sample))
    p = _prepare(norm_mix, norm_ffn, norm_final, lru_w_in, lru_conv_w, lru_conv_b, lru_w_a, lru_b_a,
                 lru_w_x, lru_b_x, lru_lambda, lru_w_out, ret_w_in, ret_decay_logit, ret_norm,
                 ret_w_out, ffn_w_in, ffn_conv_w, ffn_conv_b, ffn_w_out, x_prompt.shape[1])
    return (_encoder(x_prompt, p, SEQ_TILE), _encoder(x_sample, p, SEQ_TILE))
```

```python
import functools

import jax
import jax.numpy as jnp
from jax import lax
from jax.experimental import pallas as pl
from jax.experimental.pallas import tpu as pltpu

F32 = jnp.float32
BF16 = jnp.bfloat16

EPS = 1e-6
D_MODEL = 1024
LRU_BLOCKS = 4
LRU_BLOCK_W = D_MODEL // LRU_BLOCKS
LRU_C = 8.0
RET_HEADS = 4
RET_DK = 256
RET_DV = 512
RET_QK = RET_HEADS * RET_DK
RET_V = RET_HEADS * RET_DV
RET_CHUNK = 256
ROPE_BASE = 10000.0
LOG2_E = 1.4426950408889634
D_FF = 2816

SUBLANES = 8
LANES = 128
LANE_SLABS = D_MODEL // LANES
HALO = SUBLANES
SEQ_TILE = 512
FFN_TILE = 1024
LRU_IN_ROWS = 1024
SCAN_ROWS_BWD = 512
FF_CHUNK = 256
VMEM_LIMIT_BYTES = 56 * 1024 * 1024


def _params(semantics):
    return pltpu.CompilerParams(dimension_semantics=semantics, vmem_limit_bytes=VMEM_LIMIT_BYTES)


BATCH_THEN_TILES = ("parallel", "arbitrary")


def _const_spec(shape):
    zeros = (0,) * len(shape)
    return pl.BlockSpec(shape, lambda *_: zeros, pipeline_mode=pl.Buffered(1))


def _rms(x, g):
    return x * lax.rsqrt(jnp.mean(x * x, axis=-1, keepdims=True) + EPS) * g


def _gelu(x):
    return jax.nn.gelu(x, approximate=True)


def _sqrt_nonneg(x):
    return jnp.where(x > 0.0, x * lax.rsqrt(x), 0.0)


def _dot(a, b):
    return jnp.dot(a, b, preferred_element_type=F32)


def _dot_tn(a, b):
    return lax.dot_general(a, b, (((0,), (0,)), ((), ())), preferred_element_type=F32)


def _dot_nt(a, b):
    return lax.dot_general(a, b, (((1,), (1,)), ((), ())), preferred_element_type=F32)


def _tile_and_halo_specs(s, ts, d, batch_first=True):
    halo_per_tile = ts // HALO
    n_halo = s // HALO

    def spec(rows, tile_to_block):
        if batch_first:
            return pl.BlockSpec((1, rows, d), lambda bi, i: (bi, tile_to_block(i), 0))
        return pl.BlockSpec((1, rows, d), lambda i, bi: (bi, tile_to_block(i), 0))

    tile = spec(ts, lambda i: i)
    prev = spec(HALO, lambda i: jnp.maximum(i * halo_per_tile - 1, 0))
    nxt = spec(HALO, lambda i: jnp.minimum((i + 1) * halo_per_tile, n_halo - 1))
    return tile, prev, nxt


def _store_normed_tile_with_halo(x, xp_ref, xn_ref, g, xs_ref, j, n_tiles, ts):
    xs_ref[0:ts] = _rms(x, g).astype(BF16)
    nxt = jnp.where(j == n_tiles - 1, 0.0, _rms(xn_ref[0], g))
    prv = jnp.where(j == 0, 0.0, _rms(xp_ref[0], g))
    xs_ref[ts:ts + 2 * HALO] = jnp.concatenate([nxt, prv], axis=0).astype(BF16)


def _time_shift(ext, k, ts):
    return pltpu.roll(ext, (-k) % ext.shape[0], 0)[0:ts]


def _batch_tile_specs(batch, s, ts, d, reverse=False):
    n_tiles = s // ts
    halo_per_tile = ts // HALO
    n_halo = s // HALO

    def tile_of(i):
        return (n_tiles - 1 - i) if reverse else i

    tile = pl.BlockSpec((batch, ts, d), lambda i: (0, tile_of(i), 0))
    prev = pl.BlockSpec((batch, HALO, d), lambda i: (0, jnp.maximum(tile_of(i) * halo_per_tile - 1, 0), 0))
    nxt = pl.BlockSpec((batch, HALO, d),
                       lambda i: (0, jnp.minimum((tile_of(i) + 1) * halo_per_tile, n_halo - 1), 0))
    return tile, prev, nxt


def _slab_spec(rows, n_tiles, reverse=False):
    return pl.BlockSpec((LANE_SLABS, rows, LANES), lambda i: (0, (n_tiles - 1 - i) if reverse else i, 0))


def _lru_in_kernel(x_ref, xp_ref, xn_ref, g_ref, w_ref, cw_ref, cb_ref, wa_ref, ba_ref, wx_ref, bx_ref, lam_ref,
                   gate_ref, xc_ref, hf_ref, xs_ref, carry_ref, a_ref, u_ref, *, n_tiles, ts, batch):
    w = D_MODEL
    j = pl.program_id(0)
    g = g_ref[...]
    body = batch * ts
    halo = 2 * HALO
    for b in range(batch):
        xs_ref[b * ts:(b + 1) * ts] = _rms(x_ref[b], g).astype(BF16)
        nxt = jnp.where(j == n_tiles - 1, 0.0, _rms(xn_ref[b], g))
        prv = jnp.where(j == 0, 0.0, _rms(xp_ref[b], g))
        xs_ref[body + b * halo:body + (b + 1) * halo] = jnp.concatenate([nxt, prv], axis=0).astype(BF16)

    _lru_init_carry(carry_ref)
    decay = _lru_decay(lam_ref)
    slabs_per_block = LRU_BLOCK_W // LANES
    for nb in range(LRU_BLOCKS):
        sl = slice(nb * LRU_BLOCK_W, (nb + 1) * LRU_BLOCK_W)
        gate = _gelu(_dot(xs_ref[0:body], w_ref[:, sl]))
        rec = _dot(xs_ref[...], w_ref[:, w + nb * LRU_BLOCK_W:w + (nb + 1) * LRU_BLOCK_W])
        for b in range(batch):
            ext = jnp.concatenate([rec[b * ts:(b + 1) * ts], rec[body + b * halo:body + (b + 1) * halo]], axis=0)
            xc = (_time_shift(ext, -2, ts) * cw_ref[0:1, sl] + _time_shift(ext, -1, ts) * cw_ref[1:2, sl]
                  + ext[0:ts] * cw_ref[2:3, sl] + _time_shift(ext, 1, ts) * cw_ref[3:4, sl] + cb_ref[:, sl])
            rows = pl.ds(b, ts, stride=batch)
            for k in range(slabs_per_block):
                slab = nb * slabs_per_block + k
                lanes = slice(k * LANES, (k + 1) * LANES)
                xc_ref[slab, rows, :] = xc[:, lanes]
                gate_ref[slab, rows, :] = gate[b * ts:(b + 1) * ts, lanes]

    for nb in range(LRU_BLOCKS):
        _lru_block_gates(nb, xc_ref, wa_ref, ba_ref, wx_ref, bx_ref, decay, a_ref, u_ref)
    _lru_scan_tile(hf_ref, carry_ref, a_ref, u_ref, reverse=False, rows=body, batch=batch)


def _lru_in(x, g, w_in, cw, cb, wa, ba, wx, bx, lam, ts):
    b, s, d = x.shape
    n_tiles = s // ts
    rows = ts * b
    tile, prev, nxt = _batch_tile_specs(b, s, ts, d)
    slabs = _slab_spec(rows, n_tiles)
    out = jax.ShapeDtypeStruct((LANE_SLABS, s * b, LANES), F32)
    row = _const_spec((1, d))
    gate_w = _const_spec((LRU_BLOCKS, LRU_BLOCK_W, LRU_BLOCK_W))
    return pl.pallas_call(
        functools.partial(_lru_in_kernel, n_tiles=n_tiles, ts=ts, batch=b),
        grid=(n_tiles,),
        in_specs=[tile, prev, nxt, row, _const_spec((d, 2 * d)), _const_spec(cw.shape), row,
                  gate_w, row, gate_w, row, row],
        out_specs=[slabs, slabs, slabs],
        out_shape=[out, out, out],
        scratch_shapes=[pltpu.VMEM((b * (ts + 2 * HALO), d), BF16), pltpu.VMEM((SUBLANES, d), F32),
                        pltpu.VMEM((rows, d), F32), pltpu.VMEM((rows, d), F32)],
        compiler_params=_params(("arbitrary",)),
        name="lru_in",
    )(x, x, x, g, w_in, cw, cb, wa, ba, wx, bx, lam)


def _lru_init_carry(carry_ref):
    @pl.when(pl.program_id(0) == 0)
    def _():
        carry_ref[...] = jnp.zeros_like(carry_ref)


def _lru_decay(lam_ref):
    neg_lam = -lam_ref[...]
    softplus = jnp.maximum(neg_lam, 0.0) + jnp.log1p(jnp.exp(-jnp.abs(neg_lam)))
    return (0.5 * LRU_C) * softplus


def _lru_block_gates(nb, xc_ref, wa_ref, ba_ref, wx_ref, bx_ref, decay, a_ref, u_ref):
    slabs_per_block = LRU_BLOCK_W // LANES
    sl = slice(nb * LRU_BLOCK_W, (nb + 1) * LRU_BLOCK_W)
    xh = jnp.concatenate([xc_ref[nb * slabs_per_block + k] for k in range(slabs_per_block)], axis=1)
    xh16 = xh.astype(BF16)
    tr = jnp.tanh(_dot(xh16, wa_ref[nb]) + ba_ref[:, sl])
    ti = jnp.tanh(_dot(xh16, wx_ref[nb]) + bx_ref[:, sl])
    neg_log_a = decay[:, sl] * (tr + 1.0)
    a = jnp.exp2(neg_log_a * (-LOG2_E))
    one_minus_a2 = jnp.tanh(neg_log_a) * (a * a + 1.0)
    a_ref[:, sl] = a
    u_ref[:, sl] = _sqrt_nonneg(one_minus_a2) * (xh * (ti + 1.0))


def _lru_scan_rows(a8, u8, c, *, reverse, batch):
    substeps = SUBLANES // batch
    shift = (SUBLANES - batch) if reverse else batch % SUBLANES
    sub = lax.broadcasted_iota(jnp.int32, a8.shape, 0)
    h = a8 * c + u8
    out = h
    for k in range(1, substeps):
        h = a8 * pltpu.roll(h, shift, 0) + u8
        if reverse:
            out = jnp.where(sub < (substeps - k) * batch, h, out)
        else:
            out = jnp.where(sub >= k * batch, h, out)
    return out, (pltpu.roll(h, shift, 0) if substeps > 1 else h)


def _lru_scan_block_unrolled(nb, h_ref, carry_ref, a_ref, u_ref, *, reverse, rows, batch):
    slabs_per_block = LRU_BLOCK_W // LANES
    sl = slice(nb * LRU_BLOCK_W, (nb + 1) * LRU_BLOCK_W)
    groups = rows // SUBLANES
    c = carry_ref[:, sl]
    for gi in range(groups):
        g = (groups - 1 - gi) if reverse else gi
        r = slice(g * SUBLANES, (g + 1) * SUBLANES)
        out, c = _lru_scan_rows(a_ref[r, sl], u_ref[r, sl], c, reverse=reverse, batch=batch)
        for k in range(slabs_per_block):
            h_ref[nb * slabs_per_block + k, r, :] = out[:, k * LANES:(k + 1) * LANES]
    carry_ref[:, sl] = c


def _lru_scan_tile(h_ref, carry_ref, a_ref, u_ref, *, reverse, rows, batch):
    groups = rows // SUBLANES

    def step(gi, c):
        g = (groups - 1 - gi) if reverse else gi
        r = pl.ds(pl.multiple_of(g * SUBLANES, SUBLANES), SUBLANES)
        out, c = _lru_scan_rows(a_ref[r, :], u_ref[r, :], c, reverse=reverse, batch=batch)
        for slab in range(LANE_SLABS):
            h_ref[slab, r, :] = out[:, slab * LANES:(slab + 1) * LANES]
        return c

    carry_ref[...] = lax.fori_loop(0, groups, step, carry_ref[...], unroll=4)


def _lru_bwd_out_kernel(xc_ref, wa_ref, ba_ref, wx_ref, bx_ref, lam_ref, hf_ref, gate_ref, x_ref, wo_ref,
                        o_ref, carry_ref, a_ref, u_ref, hb_ref, z_ref, res_ref, *, rows, batch):
    _lru_init_carry(carry_ref)
    decay = _lru_decay(lam_ref)
    for nb in range(LRU_BLOCKS):
        _lru_block_gates(nb, xc_ref, wa_ref, ba_ref, wx_ref, bx_ref, decay, a_ref, u_ref)
        _lru_scan_block_unrolled(nb, hb_ref, carry_ref, a_ref, u_ref, reverse=True, rows=rows, batch=batch)
    for slab in range(LANE_SLABS):
        lanes = slice(slab * LANES, (slab + 1) * LANES)
        z_ref[:, lanes] = ((hf_ref[slab] + hb_ref[slab]) * gate_ref[slab]).astype(BF16)
    res = _dot(z_ref[...], wo_ref[...])
    for slab in range(LANE_SLABS):
        res_ref[slab] = res[:, slab * LANES:(slab + 1) * LANES]
    ts = rows // batch
    for b in range(batch):
        for slab in range(LANE_SLABS):
            lanes = slice(slab * LANES, (slab + 1) * LANES)
            o_ref[b, :, lanes] = x_ref[b, :, lanes] + res_ref[slab, pl.ds(b, ts, stride=batch), :]


def _lru_bwd_out(xc, wa, ba, wx, bx, lam, hf, gate, x, w_out, rows):
    slabs, n, lanes = xc.shape
    batch, s, d = x.shape
    n_tiles = n // rows
    tile = _slab_spec(rows, n_tiles, reverse=True)
    x_tile, _, _ = _batch_tile_specs(batch, s, rows // batch, d, reverse=True)
    row = _const_spec((1, d))
    gate_w = _const_spec((LRU_BLOCKS, LRU_BLOCK_W, LRU_BLOCK_W))
    return pl.pallas_call(
        functools.partial(_lru_bwd_out_kernel, rows=rows, batch=batch),
        grid=(n_tiles,),
        in_specs=[tile, gate_w, row, gate_w, row, row, tile, tile, x_tile, _const_spec(w_out.shape)],
        out_specs=x_tile,
        out_shape=jax.ShapeDtypeStruct(x.shape, F32),
        scratch_shapes=[pltpu.VMEM((SUBLANES, d), F32), pltpu.VMEM((rows, d), F32), pltpu.VMEM((rows, d), F32),
                        pltpu.VMEM((slabs, rows, lanes), F32), pltpu.VMEM((rows, d), BF16),
                        pltpu.VMEM((slabs, rows, lanes), F32)],
        compiler_params=_params(("arbitrary",)),
        name="lru_bwd_out",
    )(xc, wa, ba, wx, bx, lam, hf, gate, x, w_out)


def _ffn_kernel(*refs, n_tiles, ts, final):
    if final:
        (x_ref, xp_ref, xn_ref, g_ref, wu_ref, wv_ref, cw_ref, cb_ref, wo_ref, gf_ref,
         o_ref, xs_ref, act_ref) = refs
    else:
        (x_ref, xp_ref, xn_ref, g_ref, wu_ref, wv_ref, cw_ref, cb_ref, wo_ref,
         o_ref, xs_ref, act_ref) = refs
    x = x_ref[0]
    _store_normed_tile_with_halo(x, xp_ref, xn_ref, g_ref[...], xs_ref, pl.program_id(1), n_tiles, ts)
    for c in range(D_FF // FF_CHUNK):
        cols = slice(c * FF_CHUNK, (c + 1) * FF_CHUNK)
        u = _dot(xs_ref[...], wu_ref[:, cols])
        v = _dot(xs_ref[0:ts], wv_ref[:, cols])
        y = (_time_shift(u, -1, ts) * cw_ref[0:1, cols] + u[0:ts] * cw_ref[1:2, cols]
             + _time_shift(u, 1, ts) * cw_ref[2:3, cols] + cb_ref[:, cols])
        act_ref[:, cols] = (_gelu(y) * v).astype(BF16)

    out = x + _dot(act_ref[...], wo_ref[...])
    if final:
        out = _rms(out, gf_ref[...])
    o_ref[0] = out


def _ffn(x, g, wu, wv, cw, cb, wo, ts, final_g=None):
    b, s, d = x.shape
    n_tiles = s // ts
    tile, prev, nxt = _tile_and_halo_specs(s, ts, d)
    final = final_g is not None
    in_specs = [tile, prev, nxt, _const_spec((1, d)), _const_spec(wu.shape), _const_spec(wv.shape),
                _const_spec(cw.shape), _const_spec(cb.shape), _const_spec(wo.shape)]
    args = [x, x, x, g, wu, wv, cw, cb, wo]
    if final:
        in_specs.append(_const_spec((1, d)))
        args.append(final_g)
    kern = functools.partial(_ffn_kernel, n_tiles=n_tiles, ts=ts, final=final)
    return pl.pallas_call(
        kern,
        grid=(b, n_tiles),
        in_specs=in_specs,
        out_specs=tile,
        out_shape=jax.ShapeDtypeStruct((b, s, d), F32),
        scratch_shapes=[pltpu.VMEM((ts + 2 * HALO, d), BF16), pltpu.VMEM((ts, D_FF), BF16)],
        compiler_params=_params(BATCH_THEN_TILES),
        name="ffn_final" if final else "ffn",
    )(*args)


def _chunk_pos(rows):
    return (lax.broadcasted_iota(jnp.int32, (rows, 1), 0) % RET_CHUNK).astype(F32)


def _split_lanes(ref, parts):
    width = ref.shape[-1] // parts
    return [ref.at[:, :, i * width:(i + 1) * width] for i in range(parts)]


def _ret_proj_kernel(lg_ref, x_ref, g_ref, w_ref, cos_ref, sin_ref,
                     qkk_ref, v_ref, so_ref, xs_ref, kzb_ref, s_ref, *, ts):
    q_ref, k_ref, kz_ref = _split_lanes(qkk_ref, 3)
    sg_ref, ob_ref = _split_lanes(so_ref, 2)

    @pl.when(pl.program_id(1) == 0)
    def _():
        s_ref[...] = jnp.zeros_like(s_ref)

    xs_ref[...] = _rms(x_ref[0], g_ref[...]).astype(BF16)
    cos = cos_ref[...]
    sin = sin_ref[...]
    half = RET_DK // 2
    pos = _chunk_pos(ts)

    def rotary(t):
        t1 = t[:, :half]
        t2 = t[:, half:]
        return t1 * cos - t2 * sin, t2 * cos + t1 * sin

    for h in range(RET_HEADS):
        lo = h * RET_DK
        mid = lo + half
        hi = lo + RET_DK
        q1, q2 = rotary(_dot(xs_ref[...], w_ref[:, lo:hi]))
        q_ref[0, :, lo:mid] = q1.astype(BF16)
        q_ref[0, :, mid:hi] = q2.astype(BF16)
        k1, k2 = rotary(_dot(xs_ref[...], w_ref[:, RET_QK + lo:RET_QK + hi]) * (RET_DK ** -0.5))
        k_ref[0, :, lo:mid] = k1.astype(BF16)
        k_ref[0, :, mid:hi] = k2.astype(BF16)
        zeta_f = jnp.exp(lg_ref[h] * (RET_CHUNK - 1.0 - pos))
        kz_ref[0, :, lo:mid] = (k1 * zeta_f).astype(BF16)
        kz_ref[0, :, mid:hi] = (k2 * zeta_f).astype(BF16)
        zeta_b = jnp.exp(lg_ref[RET_HEADS + h] * pos)
        kzb_ref[:, lo:mid] = (k1 * zeta_b).astype(BF16)
        kzb_ref[:, mid:hi] = (k2 * zeta_b).astype(BF16)
    for h in range(RET_HEADS):
        vv = slice(h * RET_DV, (h + 1) * RET_DV)
        lo = 2 * RET_QK + h * RET_DV
        v_ref[0, :, vv] = _dot(xs_ref[...], w_ref[:, lo:lo + RET_DV]).astype(BF16)
        lo = 2 * RET_QK + RET_V + h * RET_DV
        sg_ref[0, :, vv] = jax.nn.silu(_dot(xs_ref[...], w_ref[:, lo:lo + RET_DV]))

    cpos = _chunk_pos(RET_CHUNK)
    for c in reversed(range(ts // RET_CHUNK)):
        rows = slice(c * RET_CHUNK, (c + 1) * RET_CHUNK)
        for h in range(RET_HEADS):
            lg = lg_ref[RET_HEADS + h]
            xi = jnp.exp(lg * (RET_CHUNK - cpos))
            g_chunk = jnp.exp(jnp.full((1, 1), lg * RET_CHUNK, F32))
            qk = slice(h * RET_DK, (h + 1) * RET_DK)
            vv = slice(h * RET_DV, (h + 1) * RET_DV)
            state = s_ref[h]
            ob_ref[0, rows, vv] = _dot(q_ref[0, rows, qk], state.astype(BF16)) * xi
            s_ref[h] = state * g_chunk + _dot_tn(kzb_ref[rows, qk], v_ref[0, rows, vv])


def _ret_proj(log_g, x, g, w_in, cos, sin, ts):
    b, s, d = x.shape
    n_tiles = s // ts
    tile = lambda width: pl.BlockSpec((1, ts, width), lambda bi, i: (bi, n_tiles - 1 - i, 0))
    rope = pl.BlockSpec((ts, RET_DK // 2), lambda bi, i: (n_tiles - 1 - i, 0))
    act = lambda width, dtype: jax.ShapeDtypeStruct((b, s, width), dtype)
    return pl.pallas_call(
        functools.partial(_ret_proj_kernel, ts=ts),
        grid=(b, n_tiles),
        in_specs=[pl.BlockSpec(memory_space=pltpu.SMEM), tile(d), _const_spec((1, d)), _const_spec(w_in.shape),
                  rope, rope],
        out_specs=[tile(3 * RET_QK), tile(RET_V), tile(2 * RET_V)],
        out_shape=[act(3 * RET_QK, BF16), act(RET_V, BF16), act(2 * RET_V, F32)],
        scratch_shapes=[pltpu.VMEM((ts, d), BF16), pltpu.VMEM((ts, RET_QK), BF16),
                        pltpu.VMEM((RET_HEADS, RET_DK, RET_DV), F32)],
        compiler_params=_params(BATCH_THEN_TILES),
        name="ret_proj",
    )(log_g, x, g, w_in, cos, sin)


def _ret_fwd_kernel(lg_ref, qkk_ref, v_ref, so_ref, x_ref, ng_ref, w_ref, o_ref, s_ref, z_ref, decay_ref, *, ts):
    q_ref, k_ref, kz_ref = _split_lanes(qkk_ref, 3)
    sg_ref, ob_ref = _split_lanes(so_ref, 2)

    @pl.when(pl.program_id(1) == 0)
    def _():
        s_ref[...] = jnp.zeros_like(s_ref)
        n_idx = lax.broadcasted_iota(jnp.int32, (RET_CHUNK, RET_CHUNK), 0)
        m_idx = lax.broadcasted_iota(jnp.int32, (RET_CHUNK, RET_CHUNK), 1)
        diff = (n_idx - m_idx).astype(F32)
        for h in range(RET_HEADS):
            decay_ref[h] = jnp.where(diff >= 0.0, jnp.exp(lg_ref[h] * jnp.maximum(diff, 0.0)),
                                     jnp.exp(lg_ref[RET_HEADS + h] * jnp.maximum(-diff, 0.0)))

    cpos = _chunk_pos(RET_CHUNK)
    for h in range(RET_HEADS):
        lf = lg_ref[h]
        decay = decay_ref[h]
        xi = jnp.exp(lf * (cpos + 1.0))
        g_chunk = jnp.exp(jnp.full((1, 1), lf * RET_CHUNK, F32))
        qk = slice(h * RET_DK, (h + 1) * RET_DK)
        vv = slice(h * RET_DV, (h + 1) * RET_DV)
        for c in range(ts // RET_CHUNK):
            rows = slice(c * RET_CHUNK, (c + 1) * RET_CHUNK)
            qc = q_ref[0, rows, qk]
            vc = v_ref[0, rows, vv]
            state = s_ref[h]
            scores = _dot_nt(qc, k_ref[0, rows, qk]) * decay
            y = (_dot(scores.astype(BF16), vc) + _dot(qc, state.astype(BF16)) * xi) + ob_ref[0, rows, vv]
            s_ref[h] = state * g_chunk + _dot_tn(kz_ref[0, rows, qk], vc)
            y = y * lax.rsqrt(jnp.mean(y * y, axis=-1, keepdims=True) + EPS)
            y = y * ng_ref[:, vv]
            z_ref[rows, vv] = (sg_ref[0, rows, vv] * y).astype(BF16)

    o_ref[0] = x_ref[0] + _dot(z_ref[...], w_ref[...])


def _ret_fwd(log_g, qkk, v, so, x, ng, w_out, ts):
    b, s, d = x.shape
    tile = lambda width: pl.BlockSpec((1, ts, width), lambda bi, i: (bi, i, 0))
    return pl.pallas_call(
        functools.partial(_ret_fwd_kernel, ts=ts),
        grid=(b, s // ts),
        in_specs=[pl.BlockSpec(memory_space=pltpu.SMEM), tile(3 * RET_QK), tile(RET_V), tile(2 * RET_V), tile(d),
                  _const_spec((1, RET_V)), _const_spec(w_out.shape)],
        out_specs=tile(d),
        out_shape=jax.ShapeDtypeStruct((b, s, d), F32),
        scratch_shapes=[pltpu.VMEM((RET_HEADS, RET_DK, RET_DV), F32), pltpu.VMEM((ts, RET_V), BF16),
                        pltpu.VMEM((RET_HEADS, RET_CHUNK, RET_CHUNK), F32)],
        compiler_params=_params(BATCH_THEN_TILES),
        name="ret_fwd",
    )(log_g, qkk, v, so, x, ng, w_out)


def _prepare(norm_mix, norm_ffn, norm_final, lru_w_in, lru_conv_w, lru_conv_b, lru_w_a, lru_b_a,
             lru_w_x, lru_b_x, lru_lambda, lru_w_out, ret_w_in, ret_decay_logit, ret_norm, ret_w_out,
             ffn_w_in, ffn_conv_w, ffn_conv_b, ffn_w_out, seq):
    ffn = []
    for i in range(2):
        w_in = ffn_w_in[i].astype(BF16)
        ffn.append(dict(
            g=norm_ffn[i][None, :],
            wu=w_in[:, :D_FF],
            wv=w_in[:, D_FF:],
            cw=ffn_conv_w[i],
            cb=ffn_conv_b[i][None, :],
            wo=ffn_w_out[i].astype(BF16),
        ))
    half = RET_DK // 2
    theta = ROPE_BASE ** (-jnp.arange(half, dtype=F32) / half)
    ang = jnp.arange(seq, dtype=F32)[:, None] * theta[None, :]
    return dict(
        ffn=ffn,
        norm_mix=[norm_mix[0][None, :], norm_mix[1][None, :]],
        norm_final=norm_final[None, :],
        lru_w_in=lru_w_in[0].astype(BF16),
        lru_cw=0.5 * lru_conv_w[0],
        lru_cb=0.5 * lru_conv_b[0][None, :],
        lru_wa=[lru_w_a[0, d].astype(BF16) for d in range(2)],
        lru_ba=[0.5 * lru_b_a[0, d][None, :] for d in range(2)],
        lru_wx=[lru_w_x[0, d].astype(BF16) for d in range(2)],
        lru_bx=[0.5 * lru_b_x[0, d][None, :] for d in range(2)],
        lru_lam=[lru_lambda[0, d][None, :] for d in range(2)],
        lru_w_out=lru_w_out[0].astype(BF16),
        ret_w_in=ret_w_in[0].astype(BF16),
        ret_log_g=jax.nn.log_sigmoid(ret_decay_logit[0].astype(F32)).reshape(2 * RET_HEADS),
        ret_norm=ret_norm[0][None, :],
        ret_w_out=ret_w_out[0].astype(BF16),
        cos=jnp.cos(ang),
        sin=jnp.sin(ang),
    )


def _encoder(x, p, ts):
    b, s, d = x.shape
    in_ts = min(ts, LRU_IN_ROWS // b)
    ffn_ts = ts * (FFN_TILE // SEQ_TILE)
    direction = lambda di: (p["lru_wa"][di], p["lru_ba"][di], p["lru_wx"][di], p["lru_bx"][di], p["lru_lam"][di])
    gate, xc, hf = _lru_in(x, p["norm_mix"][0], p["lru_w_in"], p["lru_cw"], p["lru_cb"], *direction(0), in_ts)
    x = _lru_bwd_out(xc, *direction(1), hf, gate, x, p["lru_w_out"], min(SCAN_ROWS_BWD, ts * b))
    f = p["ffn"][0]
    x = _ffn(x, f["g"], f["wu"], f["wv"], f["cw"], f["cb"], f["wo"], ffn_ts)
    qkk, v, so = _ret_proj(p["ret_log_g"], x, p["norm_mix"][1], p["ret_w_in"], p["cos"], p["sin"], ts)
    x = _ret_fwd(p["ret_log_g"], qkk, v, so, x, p["ret_norm"], p["ret_w_out"], ts)
    f = p["ffn"][1]
    return _ffn(x, f["g"], f["wu"], f["wv"], f["cw"], f["cb"], f["wo"], ffn_ts, final_g=p["norm_final"])


def kernel(x_prompt, x_sample, norm_mix, norm_ffn, norm_final, lru_w_in, lru_conv_w, lru_conv_b, lru_w_a, lru_b_a, lru_w_x, lru_b_x, lru_lambda, lru_w_out, ret_w_in, ret_decay_logit, ret_norm, ret_w_out, ffn_w_in, ffn_conv_w, ffn_conv_b, ffn_w_out):
    assert x_prompt.shape[1] == x_sample.shape[1] and x_prompt.shape[1] % FFN_TILE == 0
    assert all(SUBLANES % x.shape[0] == 0 for x in (x_prompt, x_sample))
    p = _prepare(norm_mix, norm_ffn, norm_final, lru_w_in, lru_conv_w, lru_conv_b, lru_w_a, lru_b_a,
                 lru_w_x, lru_b_x, lru_lambda, lru_w_out, ret_w_in, ret_decay_logit, ret_norm,
                 ret_w_out, ffn_w_in, ffn_conv_w, ffn_conv_b, ffn_w_out, x_prompt.shape[1])
    return (_encoder(x_prompt, p, SEQ_TILE), _encoder(x_sample, p, SEQ_TILE))
```

```python
import functools

import jax
import jax.numpy as jnp
from jax import lax
from jax.experimental import pallas as pl
from jax.experimental.pallas import tpu as pltpu

F32 = jnp.float32
BF16 = jnp.bfloat16

EPS = 1e-6
D_MODEL = 1024
LRU_BLOCKS = 4
LRU_BLOCK_W = D_MODEL // LRU_BLOCKS
LRU_C = 8.0
RET_HEADS = 4
RET_DK = 256
RET_DV = 512
RET_QK = RET_HEADS * RET_DK
RET_V = RET_HEADS * RET_DV
RET_CHUNK = 256
ROPE_BASE = 10000.0
LOG2_E = 1.4426950408889634
D_FF = 2816

SUBLANES = 8
LANES = 128
LANE_SLABS = D_MODEL // LANES
HALO = SUBLANES
SEQ_TILE = 512
FFN_TILE = 1024
LRU_IN_ROWS = 1024
SCAN_ROWS_BWD = 512
FF_CHUNK = 256
VMEM_LIMIT_BYTES = 56 * 1024 * 1024


def _params(semantics):
    return pltpu.CompilerParams(dimension_semantics=semantics, vmem_limit_bytes=VMEM_LIMIT_BYTES)


BATCH_THEN_TILES = ("parallel", "arbitrary")


def _const_spec(shape):
    zeros = (0,) * len(shape)
    return pl.BlockSpec(shape, lambda *_: zeros, pipeline_mode=pl.Buffered(1))


def _rms(x, g):
    return x * lax.rsqrt(jnp.mean(x * x, axis=-1, keepdims=True) + EPS) * g


def _gelu(x):
    return jax.nn.gelu(x, approximate=True)


def _sqrt_nonneg(x):
    return jnp.where(x > 0.0, x * lax.rsqrt(x), 0.0)


def _dot(a, b):
    return jnp.dot(a, b, preferred_element_type=F32)


def _dot_tn(a, b):
    return lax.dot_general(a, b, (((0,), (0,)), ((), ())), preferred_element_type=F32)


def _dot_nt(a, b):
    return lax.dot_general(a, b, (((1,), (1,)), ((), ())), preferred_element_type=F32)


def _tile_and_halo_specs(s, ts, d, batch_first=True):
    halo_per_tile = ts // HALO
    n_halo = s // HALO

    def spec(rows, tile_to_block):
        if batch_first:
            return pl.BlockSpec((1, rows, d), lambda bi, i: (bi, tile_to_block(i), 0))
        return pl.BlockSpec((1, rows, d), lambda i, bi: (bi, tile_to_block(i), 0))

    tile = spec(ts, lambda i: i)
    prev = spec(HALO, lambda i: jnp.maximum(i * halo_per_tile - 1, 0))
    nxt = spec(HALO, lambda i: jnp.minimum((i + 1) * halo_per_tile, n_halo - 1))
    return tile, prev, nxt


def _store_normed_tile_with_halo(x, xp_ref, xn_ref, g, xs_ref, j, n_tiles, ts):
    xs_ref[0:ts] = _rms(x, g).astype(BF16)
    nxt = jnp.where(j == n_tiles - 1, 0.0, _rms(xn_ref[0], g))
    prv = jnp.where(j == 0, 0.0, _rms(xp_ref[0], g))
    xs_ref[ts:ts + 2 * HALO] = jnp.concatenate([nxt, prv], axis=0).astype(BF16)


def _time_shift(ext, k, ts):
    return pltpu.roll(ext, (-k) % ext.shape[0], 0)[0:ts]


def _batch_tile_specs(batch, s, ts, d, reverse=False):
    n_tiles = s // ts
    halo_per_tile = ts // HALO
    n_halo = s // HALO

    def tile_of(i):
        return (n_tiles - 1 - i) if reverse else i

    tile = pl.BlockSpec((batch, ts, d), lambda i: (0, tile_of(i), 0))
    prev = pl.BlockSpec((batch, HALO, d), lambda i: (0, jnp.maximum(tile_of(i) * halo_per_tile - 1, 0), 0))
    nxt = pl.BlockSpec((batch, HALO, d),
                       lambda i: (0, jnp.minimum((tile_of(i) + 1) * halo_per_tile, n_halo - 1), 0))
    return tile, prev, nxt


def _slab_spec(rows, n_tiles, reverse=False):
    return pl.BlockSpec((LANE_SLABS, rows, LANES), lambda i: (0, (n_tiles - 1 - i) if reverse else i, 0))


def _lru_in_kernel(x_ref, xp_ref, xn_ref, g_ref, w_ref, cw_ref, cb_ref, wa_ref, ba_ref, wx_ref, bx_ref, lam_ref,
                   gate_ref, xc_ref, hf_ref, xs_ref, carry_ref, a_ref, u_ref, *, n_tiles, ts, batch):
    _lru_init_carry(carry_ref)
    w = D_MODEL
    j = pl.program_id(0)
    g = g_ref[...]
    body = batch * ts
    halo = 2 * HALO
    for b in range(batch):
        xs_ref[b * ts:(b + 1) * ts] = _rms(x_ref[b], g).astype(BF16)
        nxt = jnp.where(j == n_tiles - 1, 0.0, _rms(xn_ref[b], g))
        prv = jnp.where(j == 0, 0.0, _rms(xp_ref[b], g))
        xs_ref[body + b * halo:body + (b + 1) * halo] = jnp.concatenate([nxt, prv], axis=0).astype(BF16)

    decay = _lru_decay(lam_ref)
    slabs_per_block = LRU_BLOCK_W // LANES
    for nb in range(LRU_BLOCKS):
        sl = slice(nb * LRU_BLOCK_W, (nb + 1) * LRU_BLOCK_W)
        gate = _gelu(_dot(xs_ref[0:body], w_ref[:, sl]))
        rec = _dot(xs_ref[...], w_ref[:, w + nb * LRU_BLOCK_W:w + (nb + 1) * LRU_BLOCK_W])
        for b in range(batch):
            ext = jnp.concatenate([rec[b * ts:(b + 1) * ts], rec[body + b * halo:body + (b + 1) * halo]], axis=0)
            xc = (_time_shift(ext, -2, ts) * cw_ref[0:1, sl] + _time_shift(ext, -1, ts) * cw_ref[1:2, sl]
                  + ext[0:ts] * cw_ref[2:3, sl] + _time_shift(ext, 1, ts) * cw_ref[3:4, sl] + cb_ref[:, sl])
            rows = pl.ds(b, ts, stride=batch)
            for k in range(slabs_per_block):
                slab = nb * slabs_per_block + k
                lanes = slice(k * LANES, (k + 1) * LANES)
                xc_ref[slab, rows, :] = xc[:, lanes]
                gate_ref[slab, rows, :] = gate[b * ts:(b + 1) * ts, lanes]

    for nb in range(LRU_BLOCKS):
        _lru_block_gates(nb, xc_ref, wa_ref, ba_ref, wx_ref, bx_ref, decay, a_ref, u_ref)
    _lru_scan_tile(hf_ref, carry_ref, a_ref, u_ref, reverse=False, rows=body, batch=batch)


def _lru_in(x, g, w_in, cw, cb, wa, ba, wx, bx, lam, ts):
    b, s, d = x.shape
    n_tiles = s // ts
    rows = ts * b
    tile, prev, nxt = _batch_tile_specs(b, s, ts, d)
    slabs = _slab_spec(rows, n_tiles)
    out = jax.ShapeDtypeStruct((LANE_SLABS, s * b, LANES), F32)
    row = _const_spec((1, d))
    gate_w = _const_spec((LRU_BLOCKS, LRU_BLOCK_W, LRU_BLOCK_W))
    return pl.pallas_call(
        functools.partial(_lru_in_kernel, n_tiles=n_tiles, ts=ts, batch=b),
        grid=(n_tiles,),
        in_specs=[tile, prev, nxt, row, _const_spec((d, 2 * d)), _const_spec(cw.shape), row,
                  gate_w, row, gate_w, row, row],
        out_specs=[slabs, slabs, slabs],
        out_shape=[out, out, out],
        scratch_shapes=[pltpu.VMEM((b * (ts + 2 * HALO), d), BF16), pltpu.VMEM((SUBLANES, d), F32),
                        pltpu.VMEM((rows, d), F32), pltpu.VMEM((rows, d), F32)],
        compiler_params=_params(("arbitrary",)),
        name="lru_in",
    )(x, x, x, g, w_in, cw, cb, wa, ba, wx, bx, lam)


def _lru_init_carry(carry_ref):
    @pl.when(pl.program_id(0) == 0)
    def _():
        carry_ref[...] = jnp.zeros_like(carry_ref)


def _lru_decay(lam_ref):
    neg_lam = -lam_ref[...]
    softplus = jnp.maximum(neg_lam, 0.0) + jnp.log1p(jnp.exp(-jnp.abs(neg_lam)))
    return (0.5 * LRU_C) * softplus


def _lru_block_gates(nb, xc_ref, wa_ref, ba_ref, wx_ref, bx_ref, decay, a_ref, u_ref):
    slabs_per_block = LRU_BLOCK_W // LANES
    sl = slice(nb * LRU_BLOCK_W, (nb + 1) * LRU_BLOCK_W)
    xh = jnp.concatenate([xc_ref[nb * slabs_per_block + k] for k in range(slabs_per_block)], axis=1)
    xh16 = xh.astype(BF16)
    tr = jnp.tanh(_dot(xh16, wa_ref[nb]) + ba_ref[:, sl])
    ti = jnp.tanh(_dot(xh16, wx_ref[nb]) + bx_ref[:, sl])
    neg_log_a = decay[:, sl] * (tr + 1.0)
    a = jnp.exp2(neg_log_a * (-LOG2_E))
    one_minus_a2 = jnp.tanh(neg_log_a) * (a * a + 1.0)
    a_ref[:, sl] = a
    u_ref[:, sl] = _sqrt_nonneg(one_minus_a2) * (xh * (ti + 1.0))


def _lru_scan_rows(a8, u8, c, *, reverse, batch):
    substeps = SUBLANES // batch
    shift = (SUBLANES - batch) if reverse else batch % SUBLANES
    sub = lax.broadcasted_iota(jnp.int32, a8.shape, 0)
    h = a8 * c + u8
    out = h
    for k in range(1, substeps):
        h = a8 * pltpu.roll(h, shift, 0) + u8
        if reverse:
            out = jnp.where(sub < (substeps - k) * batch, h, out)
        else:
            out = jnp.where(sub >= k * batch, h, out)
    return out, (pltpu.roll(h, shift, 0) if substeps > 1 else h)


def _lru_scan_block_unrolled(nb, h_ref, carry_ref, a_ref, u_ref, *, reverse, rows, batch):
    slabs_per_block = LRU_BLOCK_W // LANES
    sl = slice(nb * LRU_BLOCK_W, (nb + 1) * LRU_BLOCK_W)
    groups = rows // SUBLANES
    c = carry_ref[:, sl]
    for gi in range(groups):
        g = (groups - 1 - gi) if reverse else gi
        r = slice(g * SUBLANES, (g + 1) * SUBLANES)
        out, c = _lru_scan_rows(a_ref[r, sl], u_ref[r, sl], c, reverse=reverse, batch=batch)
        for k in range(slabs_per_block):
            h_ref[nb * slabs_per_block + k, r, :] = out[:, k * LANES:(k + 1) * LANES]
    carry_ref[:, sl] = c


def _lru_scan_tile(h_ref, carry_ref, a_ref, u_ref, *, reverse, rows, batch):
    groups = rows // SUBLANES

    def step(gi, c):
        g = (groups - 1 - gi) if reverse else gi
        r = pl.ds(pl.multiple_of(g * SUBLANES, SUBLANES), SUBLANES)
        out, c = _lru_scan_rows(a_ref[r, :], u_ref[r, :], c, reverse=reverse, batch=batch)
        for slab in range(LANE_SLABS):
            h_ref[slab, r, :] = out[:, slab * LANES:(slab + 1) * LANES]
        return c

    carry_ref[...] = lax.fori_loop(0, groups, step, carry_ref[...], unroll=4)


def _lru_bwd_out_kernel(xc_ref, wa_ref, ba_ref, wx_ref, bx_ref, lam_ref, hf_ref, gate_ref, x_ref, wo_ref,
                        o_ref, carry_ref, a_ref, u_ref, hb_ref, z_ref, res_ref, *, rows, batch):
    _lru_init_carry(carry_ref)
    decay = _lru_decay(lam_ref)
    for nb in range(LRU_BLOCKS):
        _lru_block_gates(nb, xc_ref, wa_ref, ba_ref, wx_ref, bx_ref, decay, a_ref, u_ref)
        _lru_scan_block_unrolled(nb, hb_ref, carry_ref, a_ref, u_ref, reverse=True, rows=rows, batch=batch)
    for slab in range(LANE_SLABS):
        lanes = slice(slab * LANES, (slab + 1) * LANES)
        z_ref[:, lanes] = ((hf_ref[slab] + hb_ref[slab]) * gate_ref[slab]).astype(BF16)
    res = _dot(z_ref[...], wo_ref[...])
    for slab in range(LANE_SLABS):
        res_ref[slab] = res[:, slab * LANES:(slab + 1) * LANES]
    ts = rows // batch
    for b in range(batch):
        for slab in range(LANE_SLABS):
            lanes = slice(slab * LANES, (slab + 1) * LANES)
            o_ref[b, :, lanes] = x_ref[b, :, lanes] + res_ref[slab, pl.ds(b, ts, stride=batch), :]


def _lru_bwd_out(xc, wa, ba, wx, bx, lam, hf, gate, x, w_out, rows):
    slabs, n, lanes = xc.shape
    batch, s, d = x.shape
    n_tiles = n // rows
    tile = _slab_spec(rows, n_tiles, reverse=True)
    x_tile, _, _ = _batch_tile_specs(batch, s, rows // batch, d, reverse=True)
    row = _const_spec((1, d))
    gate_w = _const_spec((LRU_BLOCKS, LRU_BLOCK_W, LRU_BLOCK_W))
    return pl.pallas_call(
        functools.partial(_lru_bwd_out_kernel, rows=rows, batch=batch),
        grid=(n_tiles,),
        in_specs=[tile, gate_w, row, gate_w, row, row, tile, tile, x_tile, _const_spec(w_out.shape)],
        out_specs=x_tile,
        out_shape=jax.ShapeDtypeStruct(x.shape, F32),
        scratch_shapes=[pltpu.VMEM((SUBLANES, d), F32), pltpu.VMEM((rows, d), F32), pltpu.VMEM((rows, d), F32),
                        pltpu.VMEM((slabs, rows, lanes), F32), pltpu.VMEM((rows, d), BF16),
                        pltpu.VMEM((slabs, rows, lanes), F32)],
        compiler_params=_params(("arbitrary",)),
        name="lru_bwd_out",
    )(xc, wa, ba, wx, bx, lam, hf, gate, x, w_out)


def _ffn_kernel(*refs, n_tiles, ts, final):
    if final:
        (x_ref, xp_ref, xn_ref, g_ref, wu_ref, wv_ref, cw_ref, cb_ref, wo_ref, gf_ref,
         o_ref, xs_ref, act_ref) = refs
    else:
        (x_ref, xp_ref, xn_ref, g_ref, wu_ref, wv_ref, cw_ref, cb_ref, wo_ref,
         o_ref, xs_ref, act_ref) = refs
    x = x_ref[0]
    _store_normed_tile_with_halo(x, xp_ref, xn_ref, g_ref[...], xs_ref, pl.program_id(1), n_tiles, ts)
    for c in range(D_FF // FF_CHUNK):
        cols = slice(c * FF_CHUNK, (c + 1) * FF_CHUNK)
        u = _dot(xs_ref[...], wu_ref[:, cols])
        v = _dot(xs_ref[0:ts], wv_ref[:, cols])
        y = (_time_shift(u, -1, ts) * cw_ref[0:1, cols] + u[0:ts] * cw_ref[1:2, cols]
             + _time_shift(u, 1, ts) * cw_ref[2:3, cols] + cb_ref[:, cols])
        act_ref[:, cols] = (_gelu(y) * v).astype(BF16)

    out = x + _dot(act_ref[...], wo_ref[...])
    if final:
        out = _rms(out, gf_ref[...])
    o_ref[0] = out


def _ffn(x, g, wu, wv, cw, cb, wo, ts, final_g=None):
    b, s, d = x.shape
    n_tiles = s // ts
    tile, prev, nxt = _tile_and_halo_specs(s, ts, d)
    final = final_g is not None
    in_specs = [tile, prev, nxt, _const_spec((1, d)), _const_spec(wu.shape), _const_spec(wv.shape),
                _const_spec(cw.shape), _const_spec(cb.shape), _const_spec(wo.shape)]
    args = [x, x, x, g, wu, wv, cw, cb, wo]
    if final:
        in_specs.append(_const_spec((1, d)))
        args.append(final_g)
    kern = functools.partial(_ffn_kernel, n_tiles=n_tiles, ts=ts, final=final)
    return pl.pallas_call(
        kern,
        grid=(b, n_tiles),
        in_specs=in_specs,
        out_specs=tile,
        out_shape=jax.ShapeDtypeStruct((b, s, d), F32),
        scratch_shapes=[pltpu.VMEM((ts + 2 * HALO, d), BF16), pltpu.VMEM((ts, D_FF), BF16)],
        compiler_params=_params(BATCH_THEN_TILES),
        name="ffn_final" if final else "ffn",
    )(*args)


def _chunk_pos(rows):
    return (lax.broadcasted_iota(jnp.int32, (rows, 1), 0) % RET_CHUNK).astype(F32)


def _split_lanes(ref, parts):
    width = ref.shape[-1] // parts
    return [ref.at[:, :, i * width:(i + 1) * width] for i in range(parts)]


def _ret_proj_kernel(lg_ref, x_ref, g_ref, w_ref, cos_ref, sin_ref,
                     qkk_ref, v_ref, so_ref, xs_ref, kzb_ref, s_ref, *, ts):
    q_ref, k_ref, kz_ref = _split_lanes(qkk_ref, 3)
    sg_ref, ob_ref = _split_lanes(so_ref, 2)

    @pl.when(pl.program_id(1) == 0)
    def _():
        s_ref[...] = jnp.zeros_like(s_ref)

    xs_ref[...] = _rms(x_ref[0], g_ref[...]).astype(BF16)
    cos = cos_ref[...]
    sin = sin_ref[...]
    half = RET_DK // 2
    pos = _chunk_pos(ts)

    def rotary(t):
        t1 = t[:, :half]
        t2 = t[:, half:]
        return t1 * cos - t2 * sin, t2 * cos + t1 * sin

    for h in range(RET_HEADS):
        lo = h * RET_DK
        mid = lo + half
        hi = lo + RET_DK
        q1, q2 = rotary(_dot(xs_ref[...], w_ref[:, lo:hi]))
        q_ref[0, :, lo:mid] = q1.astype(BF16)
        q_ref[0, :, mid:hi] = q2.astype(BF16)
        k1, k2 = rotary(_dot(xs_ref[...], w_ref[:, RET_QK + lo:RET_QK + hi]) * (RET_DK ** -0.5))
        k_ref[0, :, lo:mid] = k1.astype(BF16)
        k_ref[0, :, mid:hi] = k2.astype(BF16)
        zeta_f = jnp.exp(lg_ref[h] * (RET_CHUNK - 1.0 - pos))
        kz_ref[0, :, lo:mid] = (k1 * zeta_f).astype(BF16)
        kz_ref[0, :, mid:hi] = (k2 * zeta_f).astype(BF16)
        zeta_b = jnp.exp(lg_ref[RET_HEADS + h] * pos)
        kzb_ref[:, lo:mid] = (k1 * zeta_b).astype(BF16)
        kzb_ref[:, mid:hi] = (k2 * zeta_b).astype(BF16)
    for h in range(RET_HEADS):
        vv = slice(h * RET_DV, (h + 1) * RET_DV)
        lo = 2 * RET_QK + h * RET_DV
        v_ref[0, :, vv] = _dot(xs_ref[...], w_ref[:, lo:lo + RET_DV]).astype(BF16)
        lo = 2 * RET_QK + RET_V + h * RET_DV
        sg_ref[0, :, vv] = jax.nn.silu(_dot(xs_ref[...], w_ref[:, lo:lo + RET_DV]))

    cpos = _chunk_pos(RET_CHUNK)
    for c in reversed(range(ts // RET_CHUNK)):
        rows = slice(c * RET_CHUNK, (c + 1) * RET_CHUNK)
        for h in range(RET_HEADS):
            lg = lg_ref[RET_HEADS + h]
            xi = jnp.exp(lg * (RET_CHUNK - cpos))
            g_chunk = jnp.exp(jnp.full((1, 1), lg * RET_CHUNK, F32))
            qk = slice(h * RET_DK, (h + 1) * RET_DK)
            vv = slice(h * RET_DV, (h + 1) * RET_DV)
            state = s_ref[h]
            ob_ref[0, rows, vv] = _dot(q_ref[0, rows, qk], state.astype(BF16)) * xi
            s_ref[h] = state * g_chunk + _dot_tn(kzb_ref[rows, qk], v_ref[0, rows, vv])


def _ret_proj(log_g, x, g, w_in, cos, sin, ts):
    b, s, d = x.shape
    n_tiles = s // ts
    tile = lambda width: pl.BlockSpec((1, ts, width), lambda bi, i: (bi, n_tiles - 1 - i, 0))
    rope = pl.BlockSpec((ts, RET_DK // 2), lambda bi, i: (n_tiles - 1 - i, 0))
    act = lambda width, dtype: jax.ShapeDtypeStruct((b, s, width), dtype)
    return pl.pallas_call(
        functools.partial(_ret_proj_kernel, ts=ts),
        grid=(b, n_tiles),
        in_specs=[pl.BlockSpec(memory_space=pltpu.SMEM), tile(d), _const_spec((1, d)), _const_spec(w_in.shape),
                  rope, rope],
        out_specs=[tile(3 * RET_QK), tile(RET_V), tile(2 * RET_V)],
        out_shape=[act(3 * RET_QK, BF16), act(RET_V, BF16), act(2 * RET_V, F32)],
        scratch_shapes=[pltpu.VMEM((ts, d), BF16), pltpu.VMEM((ts, RET_QK), BF16),
                        pltpu.VMEM((RET_HEADS, RET_DK, RET_DV), F32)],
        compiler_params=_params(BATCH_THEN_TILES),
        name="ret_proj",
    )(log_g, x, g, w_in, cos, sin)


def _ret_fwd_kernel(lg_ref, qkk_ref, v_ref, so_ref, x_ref, ng_ref, w_ref, o_ref, s_ref, z_ref, decay_ref, *, ts):
    q_ref, k_ref, kz_ref = _split_lanes(qkk_ref, 3)
    sg_ref, ob_ref = _split_lanes(so_ref, 2)

    @pl.when(pl.program_id(1) == 0)
    def _():
        s_ref[...] = jnp.zeros_like(s_ref)
        n_idx = lax.broadcasted_iota(jnp.int32, (RET_CHUNK, RET_CHUNK), 0)
        m_idx = lax.broadcasted_iota(jnp.int32, (RET_CHUNK, RET_CHUNK), 1)
        diff = (n_idx - m_idx).astype(F32)
        for h in range(RET_HEADS):
            decay_ref[h] = jnp.where(diff >= 0.0, jnp.exp(lg_ref[h] * jnp.maximum(diff, 0.0)),
                                     jnp.exp(lg_ref[RET_HEADS + h] * jnp.maximum(-diff, 0.0)))

    cpos = _chunk_pos(RET_CHUNK)
    for h in range(RET_HEADS):
        lf = lg_ref[h]
        decay = decay_ref[h]
        xi = jnp.exp(lf * (cpos + 1.0))
        g_chunk = jnp.exp(jnp.full((1, 1), lf * RET_CHUNK, F32))
        qk = slice(h * RET_DK, (h + 1) * RET_DK)
        vv = slice(h * RET_DV, (h + 1) * RET_DV)
        for c in range(ts // RET_CHUNK):
            rows = slice(c * RET_CHUNK, (c + 1) * RET_CHUNK)
            qc = q_ref[0, rows, qk]
            vc = v_ref[0, rows, vv]
            state = s_ref[h]
            scores = _dot_nt(qc, k_ref[0, rows, qk]) * decay
            y = (_dot(scores.astype(BF16), vc) + _dot(qc, state.astype(BF16)) * xi) + ob_ref[0, rows, vv]
            s_ref[h] = state * g_chunk + _dot_tn(kz_ref[0, rows, qk], vc)
            y = y * lax.rsqrt(jnp.mean(y * y, axis=-1, keepdims=True) + EPS)
            y = y * ng_ref[:, vv]
            z_ref[rows, vv] = (sg_ref[0, rows, vv] * y).astype(BF16)

    o_ref[0] = x_ref[0] + _dot(z_ref[...], w_ref[...])


def _ret_fwd(log_g, qkk, v, so, x, ng, w_out, ts):
    b, s, d = x.shape
    tile = lambda width: pl.BlockSpec((1, ts, width), lambda bi, i: (bi, i, 0))
    return pl.pallas_call(
        functools.partial(_ret_fwd_kernel, ts=ts),
        grid=(b, s // ts),
        in_specs=[pl.BlockSpec(memory_space=pltpu.SMEM), tile(3 * RET_QK), tile(RET_V), tile(2 * RET_V), tile(d),
                  _const_spec((1, RET_V)), _const_spec(w_out.shape)],
        out_specs=tile(d),
        out_shape=jax.ShapeDtypeStruct((b, s, d), F32),
        scratch_shapes=[pltpu.VMEM((RET_HEADS, RET_DK, RET_DV), F32), pltpu.VMEM((ts, RET_V), BF16),
                        pltpu.VMEM((RET_HEADS, RET_CHUNK, RET_CHUNK), F32)],
        compiler_params=_params(BATCH_THEN_TILES),
        name="ret_fwd",
    )(log_g, qkk, v, so, x, ng, w_out)


def _prepare(norm_mix, norm_ffn, norm_final, lru_w_in, lru_conv_w, lru_conv_b, lru_w_a, lru_b_a,
             lru_w_x, lru_b_x, lru_lambda, lru_w_out, ret_w_in, ret_decay_logit, ret_norm, ret_w_out,
             ffn_w_in, ffn_conv_w, ffn_conv_b, ffn_w_out, seq):
    ffn = []
    for i in range(2):
        w_in = ffn_w_in[i].astype(BF16)
        ffn.append(dict(
            g=norm_ffn[i][None, :],
            wu=w_in[:, :D_FF],
            wv=w_in[:, D_FF:],
            cw=ffn_conv_w[i],
            cb=ffn_conv_b[i][None, :],
            wo=ffn_w_out[i].astype(BF16),
        ))
    half = RET_DK // 2
    theta = ROPE_BASE ** (-jnp.arange(half, dtype=F32) / half)
    ang = jnp.arange(seq, dtype=F32)[:, None] * theta[None, :]
    return dict(
        ffn=ffn,
        norm_mix=[norm_mix[0][None, :], norm_mix[1][None, :]],
        norm_final=norm_final[None, :],
        lru_w_in=lru_w_in[0].astype(BF16),
        lru_cw=0.5 * lru_conv_w[0],
        lru_cb=0.5 * lru_conv_b[0][None, :],
        lru_wa=[lru_w_a[0, d].astype(BF16) for d in range(2)],
        lru_ba=[0.5 * lru_b_a[0, d][None, :] for d in range(2)],
        lru_wx=[lru_w_x[0, d].astype(BF16) for d in range(2)],
        lru_bx=[0.5 * lru_b_x[0, d][None, :] for d in range(2)],
        lru_lam=[lru_lambda[0, d][None, :] for d in range(2)],
        lru_w_out=lru_w_out[0].astype(BF16),
        ret_w_in=ret_w_in[0].astype(BF16),
        ret_log_g=jax.nn.log_sigmoid(ret_decay_logit[0].astype(F32)).reshape(2 * RET_HEADS),
        ret_norm=ret_norm[0][None, :],
        ret_w_out=ret_w_out[0].astype(BF16),
        cos=jnp.cos(ang),
        sin=jnp.sin(ang),
    )


def _encoder(x, p, ts):
    b, s, d = x.shape
    in_ts = min(ts, LRU_IN_ROWS // b)
    ffn_ts = ts * (FFN_TILE // SEQ_TILE)
    direction = lambda di: (p["lru_wa"][di], p["lru_ba"][di], p["lru_wx"][di], p["lru_bx"][di], p["lru_lam"][di])
    gate, xc, hf = _lru_in(x, p["norm_mix"][0], p["lru_w_in"], p["lru_cw"], p["lru_cb"], *direction(0), in_ts)
    x = _lru_bwd_out(xc, *direction(1), hf, gate, x, p["lru_w_out"], min(SCAN_ROWS_BWD, ts * b))
    f = p["ffn"][0]
    x = _ffn(x, f["g"], f["wu"], f["wv"], f["cw"], f["cb"], f["wo"], ffn_ts)
    qkk, v, so = _ret_proj(p["ret_log_g"], x, p["norm_mix"][1], p["ret_w_in"], p["cos"], p["sin"], ts)
    x = _ret_fwd(p["ret_log_g"], qkk, v, so, x, p["ret_norm"], p["ret_w_out"], ts)
    f = p["ffn"][1]
    return _ffn(x, f["g"], f["wu"], f["wv"], f["cw"], f["cb"], f["wo"], ffn_ts, final_g=p["norm_final"])


def kernel(x_prompt, x_sample, norm_mix, norm_ffn, norm_final, lru_w_in, lru_conv_w, lru_conv_b, lru_w_a, lru_b_a, lru_w_x, lru_b_x, lru_lambda, lru_w_out, ret_w_in, ret_decay_logit, ret_norm, ret_w_out, ffn_w_in, ffn_conv_w, ffn_conv_b, ffn_w_out):
    assert x_prompt.shape[1] == x_sample.shape[1] and x_prompt.shape[1] % FFN_TILE == 0
    assert all(SUBLANES % x.shape[0] == 0 for x in (x_prompt, x_sample))
    p = _prepare(norm_mix, norm_ffn, norm_final, lru_w_in, lru_conv_w, lru_conv_b, lru_w_a, lru_b_a,
                 lru_w_x, lru_b_x, lru_lambda, lru_w_out, ret_w_in, ret_decay_logit, ret_norm,
                 ret_w_out, ffn_w_in, ffn_conv_w, ffn_conv_b, ffn_w_out, x_prompt.shape[1])
    return (_encoder(x_prompt, p, SEQ_TILE), _encoder(x_sample, p, SEQ_TILE))
```

```python
import functools

import jax
import jax.numpy as jnp
from jax import lax
from jax.experimental import pallas as pl
from jax.experimental.pallas import tpu as pltpu

F32 = jnp.float32
BF16 = jnp.bfloat16

EPS = 1e-6
D_MODEL = 1024
LRU_BLOCKS = 4
LRU_BLOCK_W = D_MODEL // LRU_BLOCKS
LRU_C = 8.0
RET_HEADS = 4
RET_DK = 256
RET_DV = 512
RET_QK = RET_HEADS * RET_DK
RET_V = RET_HEADS * RET_DV
RET_CHUNK = 256
ROPE_BASE = 10000.0
LOG2_E = 1.4426950408889634
D_FF = 2816

SUBLANES = 8
LANES = 128
LANE_SLABS = D_MODEL // LANES
HALO = SUBLANES
SEQ_TILE = 512
FFN_TILE = 1024
LRU_IN_ROWS = 1024
SCAN_ROWS_BWD = 512
FF_CHUNK = 256
VMEM_LIMIT_BYTES = 56 * 1024 * 1024


def _params(semantics):
    return pltpu.CompilerParams(dimension_semantics=semantics, vmem_limit_bytes=VMEM_LIMIT_BYTES)


BATCH_THEN_TILES = ("parallel", "arbitrary")


def _const_spec(shape):
    zeros = (0,) * len(shape)
    return pl.BlockSpec(shape, lambda *_: zeros, pipeline_mode=pl.Buffered(1))


def _rms(x, g):
    return x * lax.rsqrt(jnp.mean(x * x, axis=-1, keepdims=True) + EPS) * g


def _gelu(x):
    return jax.nn.gelu(x, approximate=True)


def _sqrt_nonneg(x):
    return jnp.where(x > 0.0, x * lax.rsqrt(x), 0.0)


def _dot(a, b):
    return jnp.dot(a, b, preferred_element_type=F32)


def _dot_tn(a, b):
    return lax.dot_general(a, b, (((0,), (0,)), ((), ())), preferred_element_type=F32)


def _dot_nt(a, b):
    return lax.dot_general(a, b, (((1,), (1,)), ((), ())), preferred_element_type=F32)


def _tile_and_halo_specs(s, ts, d, batch_first=True):
    halo_per_tile = ts // HALO
    n_halo = s // HALO

    def spec(rows, tile_to_block):
        if batch_first:
            return pl.BlockSpec((1, rows, d), lambda bi, i: (bi, tile_to_block(i), 0))
        return pl.BlockSpec((1, rows, d), lambda i, bi: (bi, tile_to_block(i), 0))

    tile = spec(ts, lambda i: i)
    prev = spec(HALO, lambda i: jnp.maximum(i * halo_per_tile - 1, 0))
    nxt = spec(HALO, lambda i: jnp.minimum((i + 1) * halo_per_tile, n_halo - 1))
    return tile, prev, nxt


def _store_normed_tile_with_halo(x, xp_ref, xn_ref, g, xs_ref, j, n_tiles, ts):
    xs_ref[0:ts] = _rms(x, g).astype(BF16)
    nxt = jnp.where(j == n_tiles - 1, 0.0, _rms(xn_ref[0], g))
    prv = jnp.where(j == 0, 0.0, _rms(xp_ref[0], g))
    xs_ref[ts:ts + 2 * HALO] = jnp.concatenate([nxt, prv], axis=0).astype(BF16)


def _time_shift(ext, k, ts):
    return pltpu.roll(ext, (-k) % ext.shape[0], 0)[0:ts]


def _batch_tile_specs(batch, s, ts, d, reverse=False):
    n_tiles = s // ts
    halo_per_tile = ts // HALO
    n_halo = s // HALO

    def tile_of(i):
        return (n_tiles - 1 - i) if reverse else i

    tile = pl.BlockSpec((batch, ts, d), lambda i: (0, tile_of(i), 0))
    prev = pl.BlockSpec((batch, HALO, d), lambda i: (0, jnp.maximum(tile_of(i) * halo_per_tile - 1, 0), 0))
    nxt = pl.BlockSpec((batch, HALO, d),
                       lambda i: (0, jnp.minimum((tile_of(i) + 1) * halo_per_tile, n_halo - 1), 0))
    return tile, prev, nxt


def _slab_spec(rows, n_tiles, reverse=False):
    return pl.BlockSpec((LANE_SLABS, rows, LANES), lambda i: (0, (n_tiles - 1 - i) if reverse else i, 0))


def _lru_in_kernel(x_ref, xp_ref, xn_ref, g_ref, w_ref, cw_ref, cb_ref, wa_ref, ba_ref, wx_ref, bx_ref, lam_ref,
                   gate_ref, xc_ref, hf_ref, xs_ref, carry_ref, a_ref, u_ref, *, n_tiles, ts, batch):
    _lru_init_carry(carry_ref)
    w = D_MODEL
    j = pl.program_id(0)
    g = g_ref[...]
    body = batch * ts
    halo = 2 * HALO
    for b in range(batch):
        xs_ref[b * ts:(b + 1) * ts] = _rms(x_ref[b], g).astype(BF16)
        nxt = jnp.where(j == n_tiles - 1, 0.0, _rms(xn_ref[b], g))
        prv = jnp.where(j == 0, 0.0, _rms(xp_ref[b], g))
        xs_ref[body + b * halo:body + (b + 1) * halo] = jnp.concatenate([nxt, prv], axis=0).astype(BF16)

    decay = _lru_decay(lam_ref)
    slabs_per_block = LRU_BLOCK_W // LANES
    for nb in range(LRU_BLOCKS):
        sl = slice(nb * LRU_BLOCK_W, (nb + 1) * LRU_BLOCK_W)
        gate = _gelu(_dot(xs_ref[0:body], w_ref[:, sl]))
        rec = _dot(xs_ref[...], w_ref[:, w + nb * LRU_BLOCK_W:w + (nb + 1) * LRU_BLOCK_W])
        for b in range(batch):
            ext = jnp.concatenate([rec[b * ts:(b + 1) * ts], rec[body + b * halo:body + (b + 1) * halo]], axis=0)
            xc = (_time_shift(ext, -2, ts) * cw_ref[0:1, sl] + _time_shift(ext, -1, ts) * cw_ref[1:2, sl]
                  + ext[0:ts] * cw_ref[2:3, sl] + _time_shift(ext, 1, ts) * cw_ref[3:4, sl] + cb_ref[:, sl])
            rows = pl.ds(b, ts, stride=batch)
            for k in range(slabs_per_block):
                slab = nb * slabs_per_block + k
                lanes = slice(k * LANES, (k + 1) * LANES)
                xc_ref[slab, rows, :] = xc[:, lanes]
                gate_ref[slab, rows, :] = gate[b * ts:(b + 1) * ts, lanes]

    for nb in range(LRU_BLOCKS):
        _lru_block_gates(nb, xc_ref, wa_ref, ba_ref, wx_ref, bx_ref, decay, a_ref, u_ref)
        _lru_scan_block(nb, hf_ref, carry_ref, a_ref, u_ref, reverse=False, rows=body, batch=batch)


def _lru_in(x, g, w_in, cw, cb, wa, ba, wx, bx, lam, ts):
    b, s, d = x.shape
    n_tiles = s // ts
    rows = ts * b
    tile, prev, nxt = _batch_tile_specs(b, s, ts, d)
    slabs = _slab_spec(rows, n_tiles)
    out = jax.ShapeDtypeStruct((LANE_SLABS, s * b, LANES), F32)
    row = _const_spec((1, d))
    gate_w = _const_spec((LRU_BLOCKS, LRU_BLOCK_W, LRU_BLOCK_W))
    return pl.pallas_call(
        functools.partial(_lru_in_kernel, n_tiles=n_tiles, ts=ts, batch=b),
        grid=(n_tiles,),
        in_specs=[tile, prev, nxt, row, _const_spec((d, 2 * d)), _const_spec(cw.shape), row,
                  gate_w, row, gate_w, row, row],
        out_specs=[slabs, slabs, slabs],
        out_shape=[out, out, out],
        scratch_shapes=[pltpu.VMEM((b * (ts + 2 * HALO), d), BF16), pltpu.VMEM((SUBLANES, d), F32),
                        pltpu.VMEM((rows, d), F32), pltpu.VMEM((rows, d), F32)],
        compiler_params=_params(("arbitrary",)),
        name="lru_in",
    )(x, x, x, g, w_in, cw, cb, wa, ba, wx, bx, lam)


def _lru_init_carry(carry_ref):
    @pl.when(pl.program_id(0) == 0)
    def _():
        carry_ref[...] = jnp.zeros_like(carry_ref)


def _lru_decay(lam_ref):
    neg_lam = -lam_ref[...]
    softplus = jnp.maximum(neg_lam, 0.0) + jnp.log1p(jnp.exp(-jnp.abs(neg_lam)))
    return (0.5 * LRU_C) * softplus


def _lru_block_gates(nb, xc_ref, wa_ref, ba_ref, wx_ref, bx_ref, decay, a_ref, u_ref):
    slabs_per_block = LRU_BLOCK_W // LANES
    sl = slice(nb * LRU_BLOCK_W, (nb + 1) * LRU_BLOCK_W)
    xh = jnp.concatenate([xc_ref[nb * slabs_per_block + k] for k in range(slabs_per_block)], axis=1)
    xh16 = xh.astype(BF16)
    tr = jnp.tanh(_dot(xh16, wa_ref[nb]) + ba_ref[:, sl])
    ti = jnp.tanh(_dot(xh16, wx_ref[nb]) + bx_ref[:, sl])
    neg_log_a = decay[:, sl] * (tr + 1.0)
    a = jnp.exp2(neg_log_a * (-LOG2_E))
    one_minus_a2 = jnp.tanh(neg_log_a) * (a * a + 1.0)
    a_ref[:, sl] = a
    u_ref[:, sl] = _sqrt_nonneg(one_minus_a2) * (xh * (ti + 1.0))


def _lru_scan_rows(a8, u8, c, *, reverse, batch):
    substeps = SUBLANES // batch
    shift = (SUBLANES - batch) if reverse else batch % SUBLANES
    sub = lax.broadcasted_iota(jnp.int32, a8.shape, 0)
    h = a8 * c + u8
    out = h
    for k in range(1, substeps):
        h = a8 * pltpu.roll(h, shift, 0) + u8
        if reverse:
            out = jnp.where(sub < (substeps - k) * batch, h, out)
        else:
            out = jnp.where(sub >= k * batch, h, out)
    return out, (pltpu.roll(h, shift, 0) if substeps > 1 else h)


def _lru_scan_block(nb, h_ref, carry_ref, a_ref, u_ref, *, reverse, rows, batch):
    slabs_per_block = LRU_BLOCK_W // LANES
    sl = slice(nb * LRU_BLOCK_W, (nb + 1) * LRU_BLOCK_W)
    groups = rows // SUBLANES
    c = carry_ref[:, sl]
    for gi in range(groups):
        g = (groups - 1 - gi) if reverse else gi
        r = slice(g * SUBLANES, (g + 1) * SUBLANES)
        out, c = _lru_scan_rows(a_ref[r, sl], u_ref[r, sl], c, reverse=reverse, batch=batch)
        for k in range(slabs_per_block):
            h_ref[nb * slabs_per_block + k, r, :] = out[:, k * LANES:(k + 1) * LANES]
    carry_ref[:, sl] = c


def _lru_bwd_out_kernel(xc_ref, wa_ref, ba_ref, wx_ref, bx_ref, lam_ref, hf_ref, gate_ref, x_ref, wo_ref,
                        o_ref, carry_ref, a_ref, u_ref, hb_ref, z_ref, res_ref, *, rows, batch):
    _lru_init_carry(carry_ref)
    decay = _lru_decay(lam_ref)
    for nb in range(LRU_BLOCKS):
        _lru_block_gates(nb, xc_ref, wa_ref, ba_ref, wx_ref, bx_ref, decay, a_ref, u_ref)
        _lru_scan_block(nb, hb_ref, carry_ref, a_ref, u_ref, reverse=True, rows=rows, batch=batch)
    for slab in range(LANE_SLABS):
        lanes = slice(slab * LANES, (slab + 1) * LANES)
        z_ref[:, lanes] = ((hf_ref[slab] + hb_ref[slab]) * gate_ref[slab]).astype(BF16)
    res = _dot(z_ref[...], wo_ref[...])
    for slab in range(LANE_SLABS):
        res_ref[slab] = res[:, slab * LANES:(slab + 1) * LANES]
    ts = rows // batch
    for b in range(batch):
        for slab in range(LANE_SLABS):
            lanes = slice(slab * LANES, (slab + 1) * LANES)
            o_ref[b, :, lanes] = x_ref[b, :, lanes] + res_ref[slab, pl.ds(b, ts, stride=batch), :]


def _lru_bwd_out(xc, wa, ba, wx, bx, lam, hf, gate, x, w_out, rows):
    slabs, n, lanes = xc.shape
    batch, s, d = x.shape
    n_tiles = n // rows
    tile = _slab_spec(rows, n_tiles, reverse=True)
    x_tile, _, _ = _batch_tile_specs(batch, s, rows // batch, d, reverse=True)
    row = _const_spec((1, d))
    gate_w = _const_spec((LRU_BLOCKS, LRU_BLOCK_W, LRU_BLOCK_W))
    return pl.pallas_call(
        functools.partial(_lru_bwd_out_kernel, rows=rows, batch=batch),
        grid=(n_tiles,),
        in_specs=[tile, gate_w, row, gate_w, row, row, tile, tile, x_tile, _const_spec(w_out.shape)],
        out_specs=x_tile,
        out_shape=jax.ShapeDtypeStruct(x.shape, F32),
        scratch_shapes=[pltpu.VMEM((SUBLANES, d), F32), pltpu.VMEM((rows, d), F32), pltpu.VMEM((rows, d), F32),
                        pltpu.VMEM((slabs, rows, lanes), F32), pltpu.VMEM((rows, d), BF16),
                        pltpu.VMEM((slabs, rows, lanes), F32)],
        compiler_params=_params(("arbitrary",)),
        name="lru_bwd_out",
    )(xc, wa, ba, wx, bx, lam, hf, gate, x, w_out)


def _ffn_kernel(*refs, n_tiles, ts, final):
    if final:
        (x_ref, xp_ref, xn_ref, g_ref, wu_ref, wv_ref, cw_ref, cb_ref, wo_ref, gf_ref,
         o_ref, xs_ref, act_ref) = refs
    else:
        (x_ref, xp_ref, xn_ref, g_ref, wu_ref, wv_ref, cw_ref, cb_ref, wo_ref,
         o_ref, xs_ref, act_ref) = refs
    x = x_ref[0]
    _store_normed_tile_with_halo(x, xp_ref, xn_ref, g_ref[...], xs_ref, pl.program_id(1), n_tiles, ts)
    for c in range(D_FF // FF_CHUNK):
        cols = slice(c * FF_CHUNK, (c + 1) * FF_CHUNK)
        u = _dot(xs_ref[...], wu_ref[:, cols])
        v = _dot(xs_ref[0:ts], wv_ref[:, cols])
        y = (_time_shift(u, -1, ts) * cw_ref[0:1, cols] + u[0:ts] * cw_ref[1:2, cols]
             + _time_shift(u, 1, ts) * cw_ref[2:3, cols] + cb_ref[:, cols])
        act_ref[:, cols] = (_gelu(y) * v).astype(BF16)

    out = x + _dot(act_ref[...], wo_ref[...])
    if final:
        out = _rms(out, gf_ref[...])
    o_ref[0] = out


def _ffn(x, g, wu, wv, cw, cb, wo, ts, final_g=None):
    b, s, d = x.shape
    n_tiles = s // ts
    tile, prev, nxt = _tile_and_halo_specs(s, ts, d)
    final = final_g is not None
    in_specs = [tile, prev, nxt, _const_spec((1, d)), _const_spec(wu.shape), _const_spec(wv.shape),
                _const_spec(cw.shape), _const_spec(cb.shape), _const_spec(wo.shape)]
    args = [x, x, x, g, wu, wv, cw, cb, wo]
    if final:
        in_specs.append(_const_spec((1, d)))
        args.append(final_g)
    kern = functools.partial(_ffn_kernel, n_tiles=n_tiles, ts=ts, final=final)
    return pl.pallas_call(
        kern,
        grid=(b, n_tiles),
        in_specs=in_specs,
        out_specs=tile,
        out_shape=jax.ShapeDtypeStruct((b, s, d), F32),
        scratch_shapes=[pltpu.VMEM((ts + 2 * HALO, d), BF16), pltpu.VMEM((ts, D_FF), BF16)],
        compiler_params=_params(BATCH_THEN_TILES),
        name="ffn_final" if final else "ffn",
    )(*args)


def _chunk_pos(rows):
    return (lax.broadcasted_iota(jnp.int32, (rows, 1), 0) % RET_CHUNK).astype(F32)


def _split_lanes(ref, parts):
    width = ref.shape[-1] // parts
    return [ref.at[:, :, i * width:(i + 1) * width] for i in range(parts)]


def _ret_proj_kernel(lg_ref, x_ref, g_ref, w_ref, cos_ref, sin_ref,
                     qkk_ref, v_ref, so_ref, xs_ref, kzb_ref, s_ref, *, ts):
    q_ref, k_ref, kz_ref = _split_lanes(qkk_ref, 3)
    sg_ref, ob_ref = _split_lanes(so_ref, 2)

    @pl.when(pl.program_id(1) == 0)
    def _():
        s_ref[...] = jnp.zeros_like(s_ref)

    xs_ref[...] = _rms(x_ref[0], g_ref[...]).astype(BF16)
    cos = cos_ref[...]
    sin = sin_ref[...]
    half = RET_DK // 2
    pos = _chunk_pos(ts)

    def rotary(t):
        t1 = t[:, :half]
        t2 = t[:, half:]
        return t1 * cos - t2 * sin, t2 * cos + t1 * sin

    for h in range(RET_HEADS):
        lo = h * RET_DK
        mid = lo + half
        hi = lo + RET_DK
        q1, q2 = rotary(_dot(xs_ref[...], w_ref[:, lo:hi]))
        q_ref[0, :, lo:mid] = q1.astype(BF16)
        q_ref[0, :, mid:hi] = q2.astype(BF16)
        k1, k2 = rotary(_dot(xs_ref[...], w_ref[:, RET_QK + lo:RET_QK + hi]) * (RET_DK ** -0.5))
        k_ref[0, :, lo:mid] = k1.astype(BF16)
        k_ref[0, :, mid:hi] = k2.astype(BF16)
        zeta_f = jnp.exp(lg_ref[h] * (RET_CHUNK - 1.0 - pos))
        kz_ref[0, :, lo:mid] = (k1 * zeta_f).astype(BF16)
        kz_ref[0, :, mid:hi] = (k2 * zeta_f).astype(BF16)
        zeta_b = jnp.exp(lg_ref[RET_HEADS + h] * pos)
        kzb_ref[:, lo:mid] = (k1 * zeta_b).astype(BF16)
        kzb_ref[:, mid:hi] = (k2 * zeta_b).astype(BF16)
    for h in range(RET_HEADS):
        vv = slice(h * RET_DV, (h + 1) * RET_DV)
        lo = 2 * RET_QK + h * RET_DV
        v_ref[0, :, vv] = _dot(xs_ref[...], w_ref[:, lo:lo + RET_DV]).astype(BF16)
        lo = 2 * RET_QK + RET_V + h * RET_DV
        sg_ref[0, :, vv] = jax.nn.silu(_dot(xs_ref[...], w_ref[:, lo:lo + RET_DV]))

    cpos = _chunk_pos(RET_CHUNK)
    for c in reversed(range(ts // RET_CHUNK)):
        rows = slice(c * RET_CHUNK, (c + 1) * RET_CHUNK)
        for h in range(RET_HEADS):
            lg = lg_ref[RET_HEADS + h]
            xi = jnp.exp(lg * (RET_CHUNK - cpos))
            g_chunk = jnp.exp(jnp.full((1, 1), lg * RET_CHUNK, F32))
            qk = slice(h * RET_DK, (h + 1) * RET_DK)
            vv = slice(h * RET_DV, (h + 1) * RET_DV)
            state = s_ref[h]
            ob_ref[0, rows, vv] = _dot(q_ref[0, rows, qk], state.astype(BF16)) * xi
            s_ref[h] = state * g_chunk + _dot_tn(kzb_ref[rows, qk], v_ref[0, rows, vv])


def _ret_proj(log_g, x, g, w_in, cos, sin, ts):
    b, s, d = x.shape
    n_tiles = s // ts
    tile = lambda width: pl.BlockSpec((1, ts, width), lambda bi, i: (bi, n_tiles - 1 - i, 0))
    rope = pl.BlockSpec((ts, RET_DK // 2), lambda bi, i: (n_tiles - 1 - i, 0))
    act = lambda width, dtype: jax.ShapeDtypeStruct((b, s, width), dtype)
    return pl.pallas_call(
        functools.partial(_ret_proj_kernel, ts=ts),
        grid=(b, n_tiles),
        in_specs=[pl.BlockSpec(memory_space=pltpu.SMEM), tile(d), _const_spec((1, d)), _const_spec(w_in.shape),
                  rope, rope],
        out_specs=[tile(3 * RET_QK), tile(RET_V), tile(2 * RET_V)],
        out_shape=[act(3 * RET_QK, BF16), act(RET_V, BF16), act(2 * RET_V, F32)],
        scratch_shapes=[pltpu.VMEM((ts, d), BF16), pltpu.VMEM((ts, RET_QK), BF16),
                        pltpu.VMEM((RET_HEADS, RET_DK, RET_DV), F32)],
        compiler_params=_params(BATCH_THEN_TILES),
        name="ret_proj",
    )(log_g, x, g, w_in, cos, sin)


def _ret_fwd_kernel(lg_ref, qkk_ref, v_ref, so_ref, x_ref, ng_ref, w_ref, o_ref, s_ref, z_ref, decay_ref, *, ts):
    q_ref, k_ref, kz_ref = _split_lanes(qkk_ref, 3)
    sg_ref, ob_ref = _split_lanes(so_ref, 2)

    @pl.when(pl.program_id(1) == 0)
    def _():
        s_ref[...] = jnp.zeros_like(s_ref)
        n_idx = lax.broadcasted_iota(jnp.int32, (RET_CHUNK, RET_CHUNK), 0)
        m_idx = lax.broadcasted_iota(jnp.int32, (RET_CHUNK, RET_CHUNK), 1)
        diff = (n_idx - m_idx).astype(F32)
        for h in range(RET_HEADS):
            decay_ref[h] = jnp.where(diff >= 0.0, jnp.exp(lg_ref[h] * jnp.maximum(diff, 0.0)),
                                     jnp.exp(lg_ref[RET_HEADS + h] * jnp.maximum(-diff, 0.0)))

    cpos = _chunk_pos(RET_CHUNK)
    for h in range(RET_HEADS):
        lf = lg_ref[h]
        decay = decay_ref[h]
        xi = jnp.exp(lf * (cpos + 1.0))
        g_chunk = jnp.exp(jnp.full((1, 1), lf * RET_CHUNK, F32))
        qk = slice(h * RET_DK, (h + 1) * RET_DK)
        vv = slice(h * RET_DV, (h + 1) * RET_DV)
        for c in range(ts // RET_CHUNK):
            rows = slice(c * RET_CHUNK, (c + 1) * RET_CHUNK)
            qc = q_ref[0, rows, qk]
            vc = v_ref[0, rows, vv]
            state = s_ref[h]
            scores = _dot_nt(qc, k_ref[0, rows, qk]) * decay
            y = (_dot(scores.astype(BF16), vc) + _dot(qc, state.astype(BF16)) * xi) + ob_ref[0, rows, vv]
            s_ref[h] = state * g_chunk + _dot_tn(kz_ref[0, rows, qk], vc)
            y = y * lax.rsqrt(jnp.mean(y * y, axis=-1, keepdims=True) + EPS)
            y = y * ng_ref[:, vv]
            z_ref[rows, vv] = (sg_ref[0, rows, vv] * y).astype(BF16)

    o_ref[0] = x_ref[0] + _dot(z_ref[...], w_ref[...])


def _ret_fwd(log_g, qkk, v, so, x, ng, w_out, ts):
    b, s, d = x.shape
    tile = lambda width: pl.BlockSpec((1, ts, width), lambda bi, i: (bi, i, 0))
    return pl.pallas_call(
        functools.partial(_ret_fwd_kernel, ts=ts),
        grid=(b, s // ts),
        in_specs=[pl.BlockSpec(memory_space=pltpu.SMEM), tile(3 * RET_QK), tile(RET_V), tile(2 * RET_V), tile(d),
                  _const_spec((1, RET_V)), _const_spec(w_out.shape)],
        out_specs=tile(d),
        out_shape=jax.ShapeDtypeStruct((b, s, d), F32),
        scratch_shapes=[pltpu.VMEM((RET_HEADS, RET_DK, RET_DV), F32), pltpu.VMEM((ts, RET_V), BF16),
                        pltpu.VMEM((RET_HEADS, RET_CHUNK, RET_CHUNK), F32)],
        compiler_params=_params(BATCH_THEN_TILES),
        name="ret_fwd",
    )(log_g, qkk, v, so, x, ng, w_out)


def _prepare(norm_mix, norm_ffn, norm_final, lru_w_in, lru_conv_w, lru_conv_b, lru_w_a, lru_b_a,
             lru_w_x, lru_b_x, lru_lambda, lru_w_out, ret_w_in, ret_decay_logit, ret_norm, ret_w_out,
             ffn_w_in, ffn_conv_w, ffn_conv_b, ffn_w_out, seq):
    ffn = []
    for i in range(2):
        w_in = ffn_w_in[i].astype(BF16)
        ffn.append(dict(
            g=norm_ffn[i][None, :],
            wu=w_in[:, :D_FF],
            wv=w_in[:, D_FF:],
            cw=ffn_conv_w[i],
            cb=ffn_conv_b[i][None, :],
            wo=ffn_w_out[i].astype(BF16),
        ))
    half = RET_DK // 2
    theta = ROPE_BASE ** (-jnp.arange(half, dtype=F32) / half)
    ang = jnp.arange(seq, dtype=F32)[:, None] * theta[None, :]
    return dict(
        ffn=ffn,
        norm_mix=[norm_mix[0][None, :], norm_mix[1][None, :]],
        norm_final=norm_final[None, :],
        lru_w_in=lru_w_in[0].astype(BF16),
        lru_cw=0.5 * lru_conv_w[0],
        lru_cb=0.5 * lru_conv_b[0][None, :],
        lru_wa=[lru_w_a[0, d].astype(BF16) for d in range(2)],
        lru_ba=[0.5 * lru_b_a[0, d][None, :] for d in range(2)],
        lru_wx=[lru_w_x[0, d].astype(BF16) for d in range(2)],
        lru_bx=[0.5 * lru_b_x[0, d][None, :] for d in range(2)],
        lru_lam=[lru_lambda[0, d][None, :] for d in range(2)],
        lru_w_out=lru_w_out[0].astype(BF16),
        ret_w_in=ret_w_in[0].astype(BF16),
        ret_log_g=jax.nn.log_sigmoid(ret_decay_logit[0].astype(F32)).reshape(2 * RET_HEADS),
        ret_norm=ret_norm[0][None, :],
        ret_w_out=ret_w_out[0].astype(BF16),
        cos=jnp.cos(ang),
        sin=jnp.sin(ang),
    )


def _encoder(x, p, ts):
    b, s, d = x.shape
    in_ts = min(ts, LRU_IN_ROWS // b)
    ffn_ts = ts * (FFN_TILE // SEQ_TILE)
    direction = lambda di: (p["lru_wa"][di], p["lru_ba"][di], p["lru_wx"][di], p["lru_bx"][di], p["lru_lam"][di])
    gate, xc, hf = _lru_in(x, p["norm_mix"][0], p["lru_w_in"], p["lru_cw"], p["lru_cb"], *direction(0), in_ts)
    x = _lru_bwd_out(xc, *direction(1), hf, gate, x, p["lru_w_out"], min(SCAN_ROWS_BWD, ts * b))
    f = p["ffn"][0]
    x = _ffn(x, f["g"], f["wu"], f["wv"], f["cw"], f["cb"], f["wo"], ffn_ts)
    qkk, v, so = _ret_proj(p["ret_log_g"], x, p["norm_mix"][1], p["ret_w_in"], p["cos"], p["sin"], ts)
    x = _ret_fwd(p["ret_log_g"], qkk, v, so, x, p["ret_norm"], p["ret_w_out"], ts)
    f = p["ffn"][1]
    return _ffn(x, f["g"], f["wu"], f["wv"], f["cw"], f["cb"], f["wo"], ffn_ts, final_g=p["norm_final"])


def kernel(x_prompt, x_sample, norm_mix, norm_ffn, norm_final, lru_w_in, lru_conv_w, lru_conv_b, lru_w_a, lru_b_a, lru_w_x, lru_b_x, lru_lambda, lru_w_out, ret_w_in, ret_decay_logit, ret_norm, ret_w_out, ffn_w_in, ffn_conv_w, ffn_conv_b, ffn_w_out):
    assert x_prompt.shape[1] == x_sample.shape[1] and x_prompt.shape[1] % FFN_TILE == 0
    assert all(SUBLANES % x.shape[0] == 0 for x in (x_prompt, x_sample))
    p = _prepare(norm_mix, norm_ffn, norm_final, lru_w_in, lru_conv_w, lru_conv_b, lru_w_a, lru_b_a,
                 lru_w_x, lru_b_x, lru_lambda, lru_w_out, ret_w_in, ret_decay_logit, ret_norm,
                 ret_w_out, ffn_w_in, ffn_conv_w, ffn_conv_b, ffn_w_out, x_prompt.shape[1])
    return (_encoder(x_prompt, p, SEQ_TILE), _encoder(x_sample, p, SEQ_TILE))
```

```python
import functools

import jax
import jax.numpy as jnp
from jax import lax
from jax.experimental import pallas as pl
from jax.experimental.pallas import tpu as pltpu

F32 = jnp.float32
BF16 = jnp.bfloat16

EPS = 1e-6
D_MODEL = 1024
LRU_BLOCKS = 4
LRU_BLOCK_W = D_MODEL // LRU_BLOCKS
LRU_C = 8.0
RET_HEADS = 4
RET_DK = 256
RET_DV = 512
RET_QK = RET_HEADS * RET_DK
RET_V = RET_HEADS * RET_DV
RET_CHUNK = 256
ROPE_BASE = 10000.0
LOG2_E = 1.4426950408889634
D_FF = 2816

SUBLANES = 8
LANES = 128
LANE_SLABS = D_MODEL // LANES
HALO = SUBLANES
SEQ_TILE = 512
FFN_TILE = 1024
LRU_IN_ROWS = 1024
SCAN_ROWS_BWD = 512
FF_CHUNK = 256
VMEM_LIMIT_BYTES = 56 * 1024 * 1024


def _params(semantics):
    return pltpu.CompilerParams(dimension_semantics=semantics, vmem_limit_bytes=VMEM_LIMIT_BYTES)


BATCH_THEN_TILES = ("parallel", "arbitrary")


def _const_spec(shape):
    zeros = (0,) * len(shape)
    return pl.BlockSpec(shape, lambda *_: zeros, pipeline_mode=pl.Buffered(1))


def _rms(x, g):
    return x * lax.rsqrt(jnp.mean(x * x, axis=-1, keepdims=True) + EPS) * g


def _gelu(x):
    return jax.nn.gelu(x, approximate=True)


def _sqrt_nonneg(x):
    return jnp.where(x > 0.0, x * lax.rsqrt(x), 0.0)


def _dot(a, b):
    return jnp.dot(a, b, preferred_element_type=F32)


def _dot_tn(a, b):
    return lax.dot_general(a, b, (((0,), (0,)), ((), ())), preferred_element_type=F32)


def _dot_nt(a, b):
    return lax.dot_general(a, b, (((1,), (1,)), ((), ())), preferred_element_type=F32)


def _tile_and_halo_specs(s, ts, d):
    halo_per_tile = ts // HALO
    n_halo = s // HALO

    def spec(rows, tile_to_block):
        return pl.BlockSpec((1, rows, d), lambda bi, i: (bi, tile_to_block(i), 0))

    tile = spec(ts, lambda i: i)
    prev = spec(HALO, lambda i: jnp.maximum(i * halo_per_tile - 1, 0))
    nxt = spec(HALO, lambda i: jnp.minimum((i + 1) * halo_per_tile, n_halo - 1))
    return tile, prev, nxt


def _store_normed_tile_with_halo(x, xp_ref, xn_ref, g, xs_ref, j, n_tiles, ts):
    xs_ref[0:ts] = _rms(x, g).astype(BF16)
    nxt = jnp.where(j == n_tiles - 1, 0.0, _rms(xn_ref[0], g))
    prv = jnp.where(j == 0, 0.0, _rms(xp_ref[0], g))
    xs_ref[ts:ts + 2 * HALO] = jnp.concatenate([nxt, prv], axis=0).astype(BF16)


def _time_shift(ext, k, ts):
    return pltpu.roll(ext, (-k) % ext.shape[0], 0)[0:ts]


def _batch_tile_specs(batch, s, ts, d, reverse=False):
    n_tiles = s // ts
    halo_per_tile = ts // HALO
    n_halo = s // HALO

    def tile_of(i):
        return (n_tiles - 1 - i) if reverse else i

    tile = pl.BlockSpec((batch, ts, d), lambda i: (0, tile_of(i), 0))
    prev = pl.BlockSpec((batch, HALO, d), lambda i: (0, jnp.maximum(tile_of(i) * halo_per_tile - 1, 0), 0))
    nxt = pl.BlockSpec((batch, HALO, d),
                       lambda i: (0, jnp.minimum((tile_of(i) + 1) * halo_per_tile, n_halo - 1), 0))
    return tile, prev, nxt


def _slab_spec(rows, n_tiles, reverse=False):
    return pl.BlockSpec((LANE_SLABS, rows, LANES), lambda i: (0, (n_tiles - 1 - i) if reverse else i, 0))


def _lru_in_kernel(x_ref, xp_ref, xn_ref, g_ref, w_ref, cw_ref, cb_ref, wa_ref, ba_ref, wx_ref, bx_ref, lam_ref,
                   gate_ref, xc_ref, hf_ref, xs_ref, carry_ref, a_ref, u_ref, *, n_tiles, ts, batch):
    _lru_init_carry(carry_ref)
    w = D_MODEL
    j = pl.program_id(0)
    g = g_ref[...]
    body = batch * ts
    halo = 2 * HALO
    for b in range(batch):
        xs_ref[b * ts:(b + 1) * ts] = _rms(x_ref[b], g).astype(BF16)
        nxt = jnp.where(j == n_tiles - 1, 0.0, _rms(xn_ref[b], g))
        prv = jnp.where(j == 0, 0.0, _rms(xp_ref[b], g))
        xs_ref[body + b * halo:body + (b + 1) * halo] = jnp.concatenate([nxt, prv], axis=0).astype(BF16)

    decay = _lru_decay(lam_ref)
    slabs_per_block = LRU_BLOCK_W // LANES
    for nb in range(LRU_BLOCKS):
        sl = slice(nb * LRU_BLOCK_W, (nb + 1) * LRU_BLOCK_W)
        gate = _gelu(_dot(xs_ref[0:body], w_ref[:, sl]))
        rec = _dot(xs_ref[...], w_ref[:, w + nb * LRU_BLOCK_W:w + (nb + 1) * LRU_BLOCK_W])
        for b in range(batch):
            ext = jnp.concatenate([rec[b * ts:(b + 1) * ts], rec[body + b * halo:body + (b + 1) * halo]], axis=0)
            xc = (_time_shift(ext, -2, ts) * cw_ref[0:1, sl] + _time_shift(ext, -1, ts) * cw_ref[1:2, sl]
                  + ext[0:ts] * cw_ref[2:3, sl] + _time_shift(ext, 1, ts) * cw_ref[3:4, sl] + cb_ref[:, sl])
            rows = pl.ds(b, ts, stride=batch)
            for k in range(slabs_per_block):
                slab = nb * slabs_per_block + k
                lanes = slice(k * LANES, (k + 1) * LANES)
                xc_ref[slab, rows, :] = xc[:, lanes]
                gate_ref[slab, rows, :] = gate[b * ts:(b + 1) * ts, lanes]

    for nb in range(LRU_BLOCKS):
        _lru_block_gates(nb, xc_ref, wa_ref, ba_ref, wx_ref, bx_ref, decay, a_ref, u_ref)
        _lru_scan_block(nb, hf_ref, carry_ref, a_ref, u_ref, reverse=False, rows=body, batch=batch)


def _lru_in(x, g, w_in, cw, cb, wa, ba, wx, bx, lam, ts):
    b, s, d = x.shape
    n_tiles = s // ts
    rows = ts * b
    tile, prev, nxt = _batch_tile_specs(b, s, ts, d)
    slabs = _slab_spec(rows, n_tiles)
    out = jax.ShapeDtypeStruct((LANE_SLABS, s * b, LANES), F32)
    row = _const_spec((1, d))
    gate_w = _const_spec((LRU_BLOCKS, LRU_BLOCK_W, LRU_BLOCK_W))
    return pl.pallas_call(
        functools.partial(_lru_in_kernel, n_tiles=n_tiles, ts=ts, batch=b),
        grid=(n_tiles,),
        in_specs=[tile, prev, nxt, row, _const_spec((d, 2 * d)), _const_spec(cw.shape), row,
                  gate_w, row, gate_w, row, row],
        out_specs=[slabs, slabs, slabs],
        out_shape=[out, out, out],
        scratch_shapes=[pltpu.VMEM((b * (ts + 2 * HALO), d), BF16), pltpu.VMEM((SUBLANES, d), F32),
                        pltpu.VMEM((rows, d), F32), pltpu.VMEM((rows, d), F32)],
        compiler_params=_params(("arbitrary",)),
        name="lru_in",
    )(x, x, x, g, w_in, cw, cb, wa, ba, wx, bx, lam)


def _lru_init_carry(carry_ref):
    @pl.when(pl.program_id(0) == 0)
    def _():
        carry_ref[...] = jnp.zeros_like(carry_ref)


def _lru_decay(lam_ref):
    neg_lam = -lam_ref[...]
    softplus = jnp.maximum(neg_lam, 0.0) + jnp.log1p(jnp.exp(-jnp.abs(neg_lam)))
    return (0.5 * LRU_C) * softplus


def _lru_block_gates(nb, xc_ref, wa_ref, ba_ref, wx_ref, bx_ref, decay, a_ref, u_ref):
    slabs_per_block = LRU_BLOCK_W // LANES
    sl = slice(nb * LRU_BLOCK_W, (nb + 1) * LRU_BLOCK_W)
    xh = jnp.concatenate([xc_ref[nb * slabs_per_block + k] for k in range(slabs_per_block)], axis=1)
    xh16 = xh.astype(BF16)
    tr = jnp.tanh(_dot(xh16, wa_ref[nb]) + ba_ref[:, sl])
    ti = jnp.tanh(_dot(xh16, wx_ref[nb]) + bx_ref[:, sl])
    neg_log_a = decay[:, sl] * (tr + 1.0)
    a = jnp.exp2(neg_log_a * (-LOG2_E))
    one_minus_a2 = jnp.tanh(neg_log_a) * (a * a + 1.0)
    a_ref[:, sl] = a
    u_ref[:, sl] = _sqrt_nonneg(one_minus_a2) * (xh * (ti + 1.0))


def _lru_scan_rows(a8, u8, c, *, reverse, batch):
    substeps = SUBLANES // batch
    shift = (SUBLANES - batch) if reverse else batch % SUBLANES
    sub = lax.broadcasted_iota(jnp.int32, a8.shape, 0)
    h = a8 * c + u8
    out = h
    for k in range(1, substeps):
        h = a8 * pltpu.roll(h, shift, 0) + u8
        if reverse:
            out = jnp.where(sub < (substeps - k) * batch, h, out)
        else:
            out = jnp.where(sub >= k * batch, h, out)
    return out, (pltpu.roll(h, shift, 0) if substeps > 1 else h)


def _lru_scan_block(nb, h_ref, carry_ref, a_ref, u_ref, *, reverse, rows, batch):
    slabs_per_block = LRU_BLOCK_W // LANES
    sl = slice(nb * LRU_BLOCK_W, (nb + 1) * LRU_BLOCK_W)
    groups = rows // SUBLANES
    c = carry_ref[:, sl]
    for gi in range(groups):
        g = (groups - 1 - gi) if reverse else gi
        r = slice(g * SUBLANES, (g + 1) * SUBLANES)
        out, c = _lru_scan_rows(a_ref[r, sl], u_ref[r, sl], c, reverse=reverse, batch=batch)
        for k in range(slabs_per_block):
            h_ref[nb * slabs_per_block + k, r, :] = out[:, k * LANES:(k + 1) * LANES]
    carry_ref[:, sl] = c


def _lru_bwd_out_kernel(xc_ref, wa_ref, ba_ref, wx_ref, bx_ref, lam_ref, hf_ref, gate_ref, x_ref, wo_ref,
                        o_ref, carry_ref, a_ref, u_ref, hb_ref, z_ref, res_ref, *, rows, batch):
    _lru_init_carry(carry_ref)
    decay = _lru_decay(lam_ref)
    for nb in range(LRU_BLOCKS):
        _lru_block_gates(nb, xc_ref, wa_ref, ba_ref, wx_ref, bx_ref, decay, a_ref, u_ref)
        _lru_scan_block(nb, hb_ref, carry_ref, a_ref, u_ref, reverse=True, rows=rows, batch=batch)
    for slab in range(LANE_SLABS):
        lanes = slice(slab * LANES, (slab + 1) * LANES)
        z_ref[:, lanes] = ((hf_ref[slab] + hb_ref[slab]) * gate_ref[slab]).astype(BF16)
    res = _dot(z_ref[...], wo_ref[...])
    for slab in range(LANE_SLABS):
        res_ref[slab] = res[:, slab * LANES:(slab + 1) * LANES]
    ts = rows // batch
    for b in range(batch):
        for slab in range(LANE_SLABS):
            lanes = slice(slab * LANES, (slab + 1) * LANES)
            o_ref[b, :, lanes] = x_ref[b, :, lanes] + res_ref[slab, pl.ds(b, ts, stride=batch), :]


def _lru_bwd_out(xc, wa, ba, wx, bx, lam, hf, gate, x, w_out, rows):
    slabs, n, lanes = xc.shape
    batch, s, d = x.shape
    n_tiles = n // rows
    tile = _slab_spec(rows, n_tiles, reverse=True)
    x_tile, _, _ = _batch_tile_specs(batch, s, rows // batch, d, reverse=True)
    row = _const_spec((1, d))
    gate_w = _const_spec((LRU_BLOCKS, LRU_BLOCK_W, LRU_BLOCK_W))
    return pl.pallas_call(
        functools.partial(_lru_bwd_out_kernel, rows=rows, batch=batch),
        grid=(n_tiles,),
        in_specs=[tile, gate_w, row, gate_w, row, row, tile, tile, x_tile, _const_spec(w_out.shape)],
        out_specs=x_tile,
        out_shape=jax.ShapeDtypeStruct(x.shape, F32),
        scratch_shapes=[pltpu.VMEM((SUBLANES, d), F32), pltpu.VMEM((rows, d), F32), pltpu.VMEM((rows, d), F32),
                        pltpu.VMEM((slabs, rows, lanes), F32), pltpu.VMEM((rows, d), BF16),
                        pltpu.VMEM((slabs, rows, lanes), F32)],
        compiler_params=_params(("arbitrary",)),
        name="lru_bwd_out",
    )(xc, wa, ba, wx, bx, lam, hf, gate, x, w_out)


def _ffn_kernel(*refs, n_tiles, ts, final):
    if final:
        (x_ref, xp_ref, xn_ref, g_ref, wu_ref, wv_ref, cw_ref, cb_ref, wo_ref, gf_ref,
         o_ref, xs_ref, act_ref) = refs
    else:
        (x_ref, xp_ref, xn_ref, g_ref, wu_ref, wv_ref, cw_ref, cb_ref, wo_ref,
         o_ref, xs_ref, act_ref) = refs
    x = x_ref[0]
    _store_normed_tile_with_halo(x, xp_ref, xn_ref, g_ref[...], xs_ref, pl.program_id(1), n_tiles, ts)
    for c in range(D_FF // FF_CHUNK):
        cols = slice(c * FF_CHUNK, (c + 1) * FF_CHUNK)
        u = _dot(xs_ref[...], wu_ref[:, cols])
        v = _dot(xs_ref[0:ts], wv_ref[:, cols])
        y = (_time_shift(u, -1, ts) * cw_ref[0:1, cols] + u[0:ts] * cw_ref[1:2, cols]
             + _time_shift(u, 1, ts) * cw_ref[2:3, cols] + cb_ref[:, cols])
        act_ref[:, cols] = (_gelu(y) * v).astype(BF16)

    out = x + _dot(act_ref[...], wo_ref[...])
    if final:
        out = _rms(out, gf_ref[...])
    o_ref[0] = out


def _ffn(x, g, wu, wv, cw, cb, wo, ts, final_g=None):
    b, s, d = x.shape
    n_tiles = s // ts
    tile, prev, nxt = _tile_and_halo_specs(s, ts, d)
    final = final_g is not None
    in_specs = [tile, prev, nxt, _const_spec((1, d)), _const_spec(wu.shape), _const_spec(wv.shape),
                _const_spec(cw.shape), _const_spec(cb.shape), _const_spec(wo.shape)]
    args = [x, x, x, g, wu, wv, cw, cb, wo]
    if final:
        in_specs.append(_const_spec((1, d)))
        args.append(final_g)
    kern = functools.partial(_ffn_kernel, n_tiles=n_tiles, ts=ts, final=final)
    return pl.pallas_call(
        kern,
        grid=(b, n_tiles),
        in_specs=in_specs,
        out_specs=tile,
        out_shape=jax.ShapeDtypeStruct((b, s, d), F32),
        scratch_shapes=[pltpu.VMEM((ts + 2 * HALO, d), BF16), pltpu.VMEM((ts, D_FF), BF16)],
        compiler_params=_params(BATCH_THEN_TILES),
        name="ffn_final" if final else "ffn",
    )(*args)


def _chunk_pos(rows):
    return (lax.broadcasted_iota(jnp.int32, (rows, 1), 0) % RET_CHUNK).astype(F32)


def _split_lanes(ref, parts):
    width = ref.shape[-1] // parts
    return [ref.at[:, :, i * width:(i + 1) * width] for i in range(parts)]


def _ret_proj_kernel(lg_ref, x_ref, g_ref, w_ref, cos_ref, sin_ref,
                     qkk_ref, v_ref, so_ref, xs_ref, kzb_ref, s_ref, *, ts):
    q_ref, k_ref, kz_ref = _split_lanes(qkk_ref, 3)
    sg_ref, ob_ref = _split_lanes(so_ref, 2)

    @pl.when(pl.program_id(1) == 0)
    def _():
        s_ref[...] = jnp.zeros_like(s_ref)

    xs_ref[...] = _rms(x_ref[0], g_ref[...]).astype(BF16)
    cos = cos_ref[...]
    sin = sin_ref[...]
    half = RET_DK // 2
    pos = _chunk_pos(ts)

    def rotary(t):
        t1 = t[:, :half]
        t2 = t[:, half:]
        return t1 * cos - t2 * sin, t2 * cos + t1 * sin

    for h in range(RET_HEADS):
        lo = h * RET_DK
        mid = lo + half
        hi = lo + RET_DK
        q1, q2 = rotary(_dot(xs_ref[...], w_ref[:, lo:hi]))
        q_ref[0, :, lo:mid] = q1.astype(BF16)
        q_ref[0, :, mid:hi] = q2.astype(BF16)
        k1, k2 = rotary(_dot(xs_ref[...], w_ref[:, RET_QK + lo:RET_QK + hi]) * (RET_DK ** -0.5))
        k_ref[0, :, lo:mid] = k1.astype(BF16)
        k_ref[0, :, mid:hi] = k2.astype(BF16)
        zeta_f = jnp.exp(lg_ref[h] * (RET_CHUNK - 1.0 - pos))
        kz_ref[0, :, lo:mid] = (k1 * zeta_f).astype(BF16)
        kz_ref[0, :, mid:hi] = (k2 * zeta_f).astype(BF16)
        zeta_b = jnp.exp(lg_ref[RET_HEADS + h] * pos)
        kzb_ref[:, lo:mid] = (k1 * zeta_b).astype(BF16)
        kzb_ref[:, mid:hi] = (k2 * zeta_b).astype(BF16)
    for h in range(RET_HEADS):
        vv = slice(h * RET_DV, (h + 1) * RET_DV)
        lo = 2 * RET_QK + h * RET_DV
        v_ref[0, :, vv] = _dot(xs_ref[...], w_ref[:, lo:lo + RET_DV]).astype(BF16)
        lo = 2 * RET_QK + RET_V + h * RET_DV
        sg_ref[0, :, vv] = jax.nn.silu(_dot(xs_ref[...], w_ref[:, lo:lo + RET_DV]))

    cpos = _chunk_pos(RET_CHUNK)
    for c in reversed(range(ts // RET_CHUNK)):
        rows = slice(c * RET_CHUNK, (c + 1) * RET_CHUNK)
        for h in range(RET_HEADS):
            lg = lg_ref[RET_HEADS + h]
            xi = jnp.exp(lg * (RET_CHUNK - cpos))
            g_chunk = jnp.exp(jnp.full((1, 1), lg * RET_CHUNK, F32))
            qk = slice(h * RET_DK, (h + 1) * RET_DK)
            vv = slice(h * RET_DV, (h + 1) * RET_DV)
            state = s_ref[h]
            ob_ref[0, rows, vv] = _dot(q_ref[0, rows, qk], state.astype(BF16)) * xi
            s_ref[h] = state * g_chunk + _dot_tn(kzb_ref[rows, qk], v_ref[0, rows, vv])


def _ret_proj(log_g, x, g, w_in, cos, sin, ts):
    b, s, d = x.shape
    n_tiles = s // ts
    tile = lambda width: pl.BlockSpec((1, ts, width), lambda bi, i: (bi, n_tiles - 1 - i, 0))
    rope = pl.BlockSpec((ts, RET_DK // 2), lambda bi, i: (n_tiles - 1 - i, 0))
    act = lambda width, dtype: jax.ShapeDtypeStruct((b, s, width), dtype)
    return pl.pallas_call(
        functools.partial(_ret_proj_kernel, ts=ts),
        grid=(b, n_tiles),
        in_specs=[pl.BlockSpec(memory_space=pltpu.SMEM), tile(d), _const_spec((1, d)), _const_spec(w_in.shape),
                  rope, rope],
        out_specs=[tile(3 * RET_QK), tile(RET_V), tile(2 * RET_V)],
        out_shape=[act(3 * RET_QK, BF16), act(RET_V, BF16), act(2 * RET_V, F32)],
        scratch_shapes=[pltpu.VMEM((ts, d), BF16), pltpu.VMEM((ts, RET_QK), BF16),
                        pltpu.VMEM((RET_HEADS, RET_DK, RET_DV), F32)],
        compiler_params=_params(BATCH_THEN_TILES),
        name="ret_proj",
    )(log_g, x, g, w_in, cos, sin)


def _ret_fwd_kernel(lg_ref, qkk_ref, v_ref, so_ref, x_ref, ng_ref, w_ref, o_ref, s_ref, z_ref, decay_ref, *, ts):
    q_ref, k_ref, kz_ref = _split_lanes(qkk_ref, 3)
    sg_ref, ob_ref = _split_lanes(so_ref, 2)

    @pl.when(pl.program_id(1) == 0)
    def _():
        s_ref[...] = jnp.zeros_like(s_ref)
        n_idx = lax.broadcasted_iota(jnp.int32, (RET_CHUNK, RET_CHUNK), 0)
        m_idx = lax.broadcasted_iota(jnp.int32, (RET_CHUNK, RET_CHUNK), 1)
        diff = (n_idx - m_idx).astype(F32)
        for h in range(RET_HEADS):
            decay_ref[h] = jnp.where(diff >= 0.0, jnp.exp(lg_ref[h] * jnp.maximum(diff, 0.0)),
                                     jnp.exp(lg_ref[RET_HEADS + h] * jnp.maximum(-diff, 0.0)))

    cpos = _chunk_pos(RET_CHUNK)
    for h in range(RET_HEADS):
        lf = lg_ref[h]
        decay = decay_ref[h]
        xi = jnp.exp(lf * (cpos + 1.0))
        g_chunk = jnp.exp(jnp.full((1, 1), lf * RET_CHUNK, F32))
        qk = slice(h * RET_DK, (h + 1) * RET_DK)
        vv = slice(h * RET_DV, (h + 1) * RET_DV)
        for c in range(ts // RET_CHUNK):
            rows = slice(c * RET_CHUNK, (c + 1) * RET_CHUNK)
            qc = q_ref[0, rows, qk]
            vc = v_ref[0, rows, vv]
            state = s_ref[h]
            scores = _dot_nt(qc, k_ref[0, rows, qk]) * decay
            y = (_dot(scores.astype(BF16), vc) + _dot(qc, state.astype(BF16)) * xi) + ob_ref[0, rows, vv]
            s_ref[h] = state * g_chunk + _dot_tn(kz_ref[0, rows, qk], vc)
            y = y * lax.rsqrt(jnp.mean(y * y, axis=-1, keepdims=True) + EPS)
            y = y * ng_ref[:, vv]
            z_ref[rows, vv] = (sg_ref[0, rows, vv] * y).astype(BF16)

    o_ref[0] = x_ref[0] + _dot(z_ref[...], w_ref[...])


def _ret_fwd(log_g, qkk, v, so, x, ng, w_out, ts):
    b, s, d = x.shape
    tile = lambda width: pl.BlockSpec((1, ts, width), lambda bi, i: (bi, i, 0))
    return pl.pallas_call(
        functools.partial(_ret_fwd_kernel, ts=ts),
        grid=(b, s // ts),
        in_specs=[pl.BlockSpec(memory_space=pltpu.SMEM), tile(3 * RET_QK), tile(RET_V), tile(2 * RET_V), tile(d),
                  _const_spec((1, RET_V)), _const_spec(w_out.shape)],
        out_specs=tile(d),
        out_shape=jax.ShapeDtypeStruct((b, s, d), F32),
        scratch_shapes=[pltpu.VMEM((RET_HEADS, RET_DK, RET_DV), F32), pltpu.VMEM((ts, RET_V), BF16),
                        pltpu.VMEM((RET_HEADS, RET_CHUNK, RET_CHUNK), F32)],
        compiler_params=_params(BATCH_THEN_TILES),
        name="ret_fwd",
    )(log_g, qkk, v, so, x, ng, w_out)


def _prepare(norm_mix, norm_ffn, norm_final, lru_w_in, lru_conv_w, lru_conv_b, lru_w_a, lru_b_a,
             lru_w_x, lru_b_x, lru_lambda, lru_w_out, ret_w_in, ret_decay_logit, ret_norm, ret_w_out,
             ffn_w_in, ffn_conv_w, ffn_conv_b, ffn_w_out, seq):
    ffn = []
    for i in range(2):
        w_in = ffn_w_in[i].astype(BF16)
        ffn.append(dict(
            g=norm_ffn[i][None, :],
            wu=w_in[:, :D_FF],
            wv=w_in[:, D_FF:],
            cw=ffn_conv_w[i],
            cb=ffn_conv_b[i][None, :],
            wo=ffn_w_out[i].astype(BF16),
        ))
    half = RET_DK // 2
    theta = ROPE_BASE ** (-jnp.arange(half, dtype=F32) / half)
    ang = jnp.arange(seq, dtype=F32)[:, None] * theta[None, :]
    return dict(
        ffn=ffn,
        norm_mix=[norm_mix[0][None, :], norm_mix[1][None, :]],
        norm_final=norm_final[None, :],
        lru_w_in=lru_w_in[0].astype(BF16),
        lru_cw=0.5 * lru_conv_w[0],
        lru_cb=0.5 * lru_conv_b[0][None, :],
        lru_wa=[lru_w_a[0, d].astype(BF16) for d in range(2)],
        lru_ba=[0.5 * lru_b_a[0, d][None, :] for d in range(2)],
        lru_wx=[lru_w_x[0, d].astype(BF16) for d in range(2)],
        lru_bx=[0.5 * lru_b_x[0, d][None, :] for d in range(2)],
        lru_lam=[lru_lambda[0, d][None, :] for d in range(2)],
        lru_w_out=lru_w_out[0].astype(BF16),
        ret_w_in=ret_w_in[0].astype(BF16),
        ret_log_g=jax.nn.log_sigmoid(ret_decay_logit[0].astype(F32)).reshape(2 * RET_HEADS),
        ret_norm=ret_norm[0][None, :],
        ret_w_out=ret_w_out[0].astype(BF16),
        cos=jnp.cos(ang),
        sin=jnp.sin(ang),
    )


def _encoder(x, p, ts):
    b, s, d = x.shape
    in_ts = min(ts, LRU_IN_ROWS // b)
    ffn_ts = ts * (FFN_TILE // SEQ_TILE)
    direction = lambda di: (p["lru_wa"][di], p["lru_ba"][di], p["lru_wx"][di], p["lru_bx"][di], p["lru_lam"][di])
    gate, xc, hf = _lru_in(x, p["norm_mix"][0], p["lru_w_in"], p["lru_cw"], p["lru_cb"], *direction(0), in_ts)
    x = _lru_bwd_out(xc, *direction(1), hf, gate, x, p["lru_w_out"], min(SCAN_ROWS_BWD, ts * b))
    f = p["ffn"][0]
    x = _ffn(x, f["g"], f["wu"], f["wv"], f["cw"], f["cb"], f["wo"], ffn_ts)
    qkk, v, so = _ret_proj(p["ret_log_g"], x, p["norm_mix"][1], p["ret_w_in"], p["cos"], p["sin"], ts)
    x = _ret_fwd(p["ret_log_g"], qkk, v, so, x, p["ret_norm"], p["ret_w_out"], ts)
    f = p["ffn"][1]
    return _ffn(x, f["g"], f["wu"], f["wv"], f["cw"], f["cb"], f["wo"], ffn_ts, final_g=p["norm_final"])


def kernel(x_prompt, x_sample, norm_mix, norm_ffn, norm_final, lru_w_in, lru_conv_w, lru_conv_b, lru_w_a, lru_b_a, lru_w_x, lru_b_x, lru_lambda, lru_w_out, ret_w_in, ret_decay_logit, ret_norm, ret_w_out, ffn_w_in, ffn_conv_w, ffn_conv_b, ffn_w_out):
    assert x_prompt.shape[1] == x_sample.shape[1] and x_prompt.shape[1] % FFN_TILE == 0
    assert all(SUBLANES % x.shape[0] == 0 for x in (x_prompt, x_sample))
    p = _prepare(norm_mix, norm_ffn, norm_final, lru_w_in, lru_conv_w, lru_conv_b, lru_w_a, lru_b_a,
                 lru_w_x, lru_b_x, lru_lambda, lru_w_out, ret_w_in, ret_decay_logit, ret_norm,
                 ret_w_out, ffn_w_in, ffn_conv_w, ffn_conv_b, ffn_w_out, x_prompt.shape[1])
    return (_encoder(x_prompt, p, SEQ_TILE), _encoder(x_sample, p, SEQ_TILE))
```

```python
import functools

import jax
import jax.numpy as jnp
from jax import lax
from jax.experimental import pallas as pl
from jax.experimental.pallas import tpu as pltpu

F32 = jnp.float32
BF16 = jnp.bfloat16

EPS = 1e-6
D_MODEL = 1024
LRU_BLOCKS = 4
LRU_BLOCK_W = D_MODEL // LRU_BLOCKS
LRU_C = 8.0
RET_HEADS = 4
RET_DK = 256
RET_DV = 512
RET_QK = RET_HEADS * RET_DK
RET_V = RET_HEADS * RET_DV
RET_CHUNK = 256
ROPE_BASE = 10000.0
LOG2_E = 1.4426950408889634
D_FF = 2816

SUBLANES = 8
LANES = 128
LANE_SLABS = D_MODEL // LANES
HALO = SUBLANES
SEQ_TILE = 512
FFN_TILE = 1024
LRU_IN_ROWS = 1024
SCAN_ROWS_BWD = 512
FF_CHUNK = 256
VMEM_LIMIT_BYTES = 56 * 1024 * 1024


def _params(semantics):
    return pltpu.CompilerParams(dimension_semantics=semantics, vmem_limit_bytes=VMEM_LIMIT_BYTES)


BATCH_THEN_TILES = ("parallel", "arbitrary")


def _const_spec(shape):
    zeros = (0,) * len(shape)
    return pl.BlockSpec(shape, lambda *_: zeros, pipeline_mode=pl.Buffered(1))


def _rms(x, g):
    return x * lax.rsqrt(jnp.mean(x * x, axis=-1, keepdims=True) + EPS) * g


def _gelu(x):
    return jax.nn.gelu(x, approximate=True)


def _sqrt_nonneg(x):
    return jnp.where(x > 0.0, x * lax.rsqrt(x), 0.0)


def _dot(a, b):
    return jnp.dot(a, b, preferred_element_type=F32)


def _dot_tn(a, b):
    return lax.dot_general(a, b, (((0,), (0,)), ((), ())), preferred_element_type=F32)


def _dot_nt(a, b):
    return lax.dot_general(a, b, (((1,), (1,)), ((), ())), preferred_element_type=F32)


def _tile_and_halo_specs(s, ts, d):
    halo_per_tile = ts // HALO
    n_halo = s // HALO

    def spec(rows, tile_to_block):
        return pl.BlockSpec((1, rows, d), lambda bi, i: (bi, tile_to_block(i), 0))

    tile = spec(ts, lambda i: i)
    prev = spec(HALO, lambda i: jnp.maximum(i * halo_per_tile - 1, 0))
    nxt = spec(HALO, lambda i: jnp.minimum((i + 1) * halo_per_tile, n_halo - 1))
    return tile, prev, nxt


def _store_normed_tile_with_halo(x, xp_ref, xn_ref, g, xs_ref, j, n_tiles, ts):
    xs_ref[0:ts] = _rms(x, g).astype(BF16)
    nxt = jnp.where(j == n_tiles - 1, 0.0, _rms(xn_ref[0], g))
    prv = jnp.where(j == 0, 0.0, _rms(xp_ref[0], g))
    xs_ref[ts:ts + 2 * HALO] = jnp.concatenate([nxt, prv], axis=0).astype(BF16)


def _time_shift(ext, k, ts):
    return pltpu.roll(ext, (-k) % ext.shape[0], 0)[0:ts]


def _batch_tile_specs(batch, s, ts, d, reverse=False):
    n_tiles = s // ts
    halo_per_tile = ts // HALO
    n_halo = s // HALO

    def tile_of(i):
        return (n_tiles - 1 - i) if reverse else i

    tile = pl.BlockSpec((batch, ts, d), lambda i: (0, tile_of(i), 0))
    prev = pl.BlockSpec((batch, HALO, d), lambda i: (0, jnp.maximum(tile_of(i) * halo_per_tile - 1, 0), 0))
    nxt = pl.BlockSpec((batch, HALO, d),
                       lambda i: (0, jnp.minimum((tile_of(i) + 1) * halo_per_tile, n_halo - 1), 0))
    return tile, prev, nxt


def _slab_spec(rows, n_tiles, reverse=False):
    return pl.BlockSpec((LANE_SLABS, rows, LANES), lambda i: (0, (n_tiles - 1 - i) if reverse else i, 0))


def _lru_in_kernel(x_ref, xp_ref, xn_ref, g_ref, w_ref, cw_ref, cb_ref, wa_ref, ba_ref, wx_ref, bx_ref, lam_ref,
                   gate_ref, xc_ref, hf_ref, xs_ref, carry_ref, a_ref, u_ref, *, n_tiles, ts, batch):
    _lru_init_carry(carry_ref)
    w = D_MODEL
    j = pl.program_id(0)
    g = g_ref[...]
    body = batch * ts
    halo = 2 * HALO
    for b in range(batch):
        xs_ref[b * ts:(b + 1) * ts] = _rms(x_ref[b], g).astype(BF16)
        nxt = jnp.where(j == n_tiles - 1, 0.0, _rms(xn_ref[b], g))
        prv = jnp.where(j == 0, 0.0, _rms(xp_ref[b], g))
        xs_ref[body + b * halo:body + (b + 1) * halo] = jnp.concatenate([nxt, prv], axis=0).astype(BF16)

    decay = _lru_decay(lam_ref)
    slabs_per_block = LRU_BLOCK_W // LANES
    for nb in range(LRU_BLOCKS):
        sl = slice(nb * LRU_BLOCK_W, (nb + 1) * LRU_BLOCK_W)
        gate = _gelu(_dot(xs_ref[0:body], w_ref[:, sl]))
        rec = _dot(xs_ref[...], w_ref[:, w + nb * LRU_BLOCK_W:w + (nb + 1) * LRU_BLOCK_W])
        for b in range(batch):
            ext = jnp.concatenate([rec[b * ts:(b + 1) * ts], rec[body + b * halo:body + (b + 1) * halo]], axis=0)
            xc = (_time_shift(ext, -2, ts) * cw_ref[0:1, sl] + _time_shift(ext, -1, ts) * cw_ref[1:2, sl]
                  + ext[0:ts] * cw_ref[2:3, sl] + _time_shift(ext, 1, ts) * cw_ref[3:4, sl] + cb_ref[:, sl])
            rows = pl.ds(b, ts, stride=batch)
            for k in range(slabs_per_block):
                slab = nb * slabs_per_block + k
                lanes = slice(k * LANES, (k + 1) * LANES)
                xc_ref[slab, rows, :] = xc[:, lanes]
                gate_ref[slab, rows, :] = gate[b * ts:(b + 1) * ts, lanes]

    for nb in range(LRU_BLOCKS):
        _lru_block_gates(nb, xc_ref, wa_ref, ba_ref, wx_ref, bx_ref, decay, a_ref, u_ref)
        _lru_scan_block(nb, hf_ref, carry_ref, a_ref, u_ref, reverse=False, rows=body, batch=batch)


def _lru_in(x, g, w_in, cw, cb, wa, ba, wx, bx, lam, ts):
    b, s, d = x.shape
    n_tiles = s // ts
    rows = ts * b
    tile, prev, nxt = _batch_tile_specs(b, s, ts, d)
    slabs = _slab_spec(rows, n_tiles)
    out = jax.ShapeDtypeStruct((LANE_SLABS, s * b, LANES), F32)
    row = _const_spec((1, d))
    gate_w = _const_spec((LRU_BLOCKS, LRU_BLOCK_W, LRU_BLOCK_W))
    return pl.pallas_call(
        functools.partial(_lru_in_kernel, n_tiles=n_tiles, ts=ts, batch=b),
        grid=(n_tiles,),
        in_specs=[tile, prev, nxt, row, _const_spec((d, 2 * d)), _const_spec(cw.shape), row,
                  gate_w, row, gate_w, row, row],
        out_specs=[slabs, slabs, slabs],
        out_shape=[out, out, out],
        scratch_shapes=[pltpu.VMEM((b * (ts + 2 * HALO), d), BF16), pltpu.VMEM((SUBLANES, d), F32),
                        pltpu.VMEM((rows, d), F32), pltpu.VMEM((rows, d), F32)],
        compiler_params=_params(("arbitrary",)),
        name="lru_in",
    )(x, x, x, g, w_in, cw, cb, wa, ba, wx, bx, lam)


def _lru_init_carry(carry_ref):
    @pl.when(pl.program_id(0) == 0)
    def _():
        carry_ref[...] = jnp.zeros_like(carry_ref)


def _lru_decay(lam_ref):
    neg_lam = -lam_ref[...]
    softplus = jnp.maximum(neg_lam, 0.0) + jnp.log1p(jnp.exp(-jnp.abs(neg_lam)))
    return (0.5 * LRU_C) * softplus


def _lru_block_gates(nb, xc_ref, wa_ref, ba_ref, wx_ref, bx_ref, decay, a_ref, u_ref):
    slabs_per_block = LRU_BLOCK_W // LANES
    sl = slice(nb * LRU_BLOCK_W, (nb + 1) * LRU_BLOCK_W)
    xh = jnp.concatenate([xc_ref[nb * slabs_per_block + k] for k in range(slabs_per_block)], axis=1)
    xh16 = xh.astype(BF16)
    tr = jnp.tanh(_dot(xh16, wa_ref[nb]) + ba_ref[:, sl])
    ti = jnp.tanh(_dot(xh16, wx_ref[nb]) + bx_ref[:, sl])
    neg_log_a = decay[:, sl] * (tr + 1.0)
    a = jnp.exp2(neg_log_a * (-LOG2_E))
    one_minus_a2 = jnp.tanh(neg_log_a) * (a * a + 1.0)
    a_ref[:, sl] = a
    u_ref[:, sl] = _sqrt_nonneg(one_minus_a2) * (xh * (ti + 1.0))


def _lru_scan_rows(a8, u8, c, *, reverse, batch):
    substeps = SUBLANES // batch
    shift = (SUBLANES - batch) if reverse else batch % SUBLANES
    sub = lax.broadcasted_iota(jnp.int32, a8.shape, 0)
    h = a8 * c + u8
    out = h
    for k in range(1, substeps):
        h = a8 * pltpu.roll(h, shift, 0) + u8
        if reverse:
            out = jnp.where(sub < (substeps - k) * batch, h, out)
        else:
            out = jnp.where(sub >= k * batch, h, out)
    return out, (pltpu.roll(h, shift, 0) if substeps > 1 else h)


def _lru_scan_block(nb, h_ref, carry_ref, a_ref, u_ref, *, reverse, rows, batch):
    slabs_per_block = LRU_BLOCK_W // LANES
    sl = slice(nb * LRU_BLOCK_W, (nb + 1) * LRU_BLOCK_W)
    groups = rows // SUBLANES
    c = carry_ref[:, sl]
    for gi in range(groups):
        g = (groups - 1 - gi) if reverse else gi
        r = slice(g * SUBLANES, (g + 1) * SUBLANES)
        out, c = _lru_scan_rows(a_ref[r, sl], u_ref[r, sl], c, reverse=reverse, batch=batch)
        for k in range(slabs_per_block):
            h_ref[nb * slabs_per_block + k, r, :] = out[:, k * LANES:(k + 1) * LANES]
    carry_ref[:, sl] = c


def _lru_bwd_out_kernel(xc_ref, wa_ref, ba_ref, wx_ref, bx_ref, lam_ref, hf_ref, gate_ref, x_ref, wo_ref,
                        o_ref, carry_ref, a_ref, u_ref, hb_ref, z_ref, res_ref, *, rows, batch):
    _lru_init_carry(carry_ref)
    decay = _lru_decay(lam_ref)
    for nb in range(LRU_BLOCKS):
        _lru_block_gates(nb, xc_ref, wa_ref, ba_ref, wx_ref, bx_ref, decay, a_ref, u_ref)
        _lru_scan_block(nb, hb_ref, carry_ref, a_ref, u_ref, reverse=True, rows=rows, batch=batch)
    for slab in range(LANE_SLABS):
        lanes = slice(slab * LANES, (slab + 1) * LANES)
        z_ref[:, lanes] = ((hf_ref[slab] + hb_ref[slab]) * gate_ref[slab]).astype(BF16)
    res = _dot(z_ref[...], wo_ref[...])
    for slab in range(LANE_SLABS):
        res_ref[slab] = res[:, slab * LANES:(slab + 1) * LANES]
    ts = rows // batch
    for b in range(batch):
        for slab in range(LANE_SLABS):
            lanes = slice(slab * LANES, (slab + 1) * LANES)
            o_ref[b, :, lanes] = x_ref[b, :, lanes] + res_ref[slab, pl.ds(b, ts, stride=batch), :]


def _lru_bwd_out(xc, wa, ba, wx, bx, lam, hf, gate, x, w_out, rows):
    slabs, n, lanes = xc.shape
    batch, s, d = x.shape
    n_tiles = n // rows
    tile = _slab_spec(rows, n_tiles, reverse=True)
    x_tile, _, _ = _batch_tile_specs(batch, s, rows // batch, d, reverse=True)
    row = _const_spec((1, d))
    gate_w = _const_spec((LRU_BLOCKS, LRU_BLOCK_W, LRU_BLOCK_W))
    return pl.pallas_call(
        functools.partial(_lru_bwd_out_kernel, rows=rows, batch=batch),
        grid=(n_tiles,),
        in_specs=[tile, gate_w, row, gate_w, row, row, tile, tile, x_tile, _const_spec(w_out.shape)],
        out_specs=x_tile,
        out_shape=jax.ShapeDtypeStruct(x.shape, F32),
        scratch_shapes=[pltpu.VMEM((SUBLANES, d), F32), pltpu.VMEM((rows, d), F32), pltpu.VMEM((rows, d), F32),
                        pltpu.VMEM((slabs, rows, lanes), F32), pltpu.VMEM((rows, d), BF16),
                        pltpu.VMEM((slabs, rows, lanes), F32)],
        compiler_params=_params(("arbitrary",)),
        name="lru_bwd_out",
    )(xc, wa, ba, wx, bx, lam, hf, gate, x, w_out)


def _ffn_kernel(*refs, n_tiles, ts, final):
    if final:
        (x_ref, xp_ref, xn_ref, g_ref, wu_ref, wv_ref, cw_ref, cb_ref, wo_ref, gf_ref,
         o_ref, xs_ref, act_ref) = refs
    else:
        (x_ref, xp_ref, xn_ref, g_ref, wu_ref, wv_ref, cw_ref, cb_ref, wo_ref,
         o_ref, xs_ref, act_ref) = refs
    x = x_ref[0]
    _store_normed_tile_with_halo(x, xp_ref, xn_ref, g_ref[...], xs_ref, pl.program_id(1), n_tiles, ts)
    for c in range(D_FF // FF_CHUNK):
        cols = slice(c * FF_CHUNK, (c + 1) * FF_CHUNK)
        u = _dot(xs_ref[...], wu_ref[:, cols])
        v = _dot(xs_ref[0:ts], wv_ref[:, cols])
        y = (_time_shift(u, -1, ts) * cw_ref[0:1, cols] + u[0:ts] * cw_ref[1:2, cols]
             + _time_shift(u, 1, ts) * cw_ref[2:3, cols] + cb_ref[:, cols])
        act_ref[:, cols] = (_gelu(y) * v).astype(BF16)

    out = x + _dot(act_ref[...], wo_ref[...])
    if final:
        out = _rms(out, gf_ref[...])
    o_ref[0] = out


def _ffn(x, g, wu, wv, cw, cb, wo, ts, final_g=None):
    b, s, d = x.shape
    n_tiles = s // ts
    tile, prev, nxt = _tile_and_halo_specs(s, ts, d)
    final = final_g is not None
    in_specs = [tile, prev, nxt, _const_spec((1, d)), _const_spec(wu.shape), _const_spec(wv.shape),
                _const_spec(cw.shape), _const_spec(cb.shape), _const_spec(wo.shape)]
    args = [x, x, x, g, wu, wv, cw, cb, wo]
    if final:
        in_specs.append(_const_spec((1, d)))
        args.append(final_g)
    kern = functools.partial(_ffn_kernel, n_tiles=n_tiles, ts=ts, final=final)
    return pl.pallas_call(
        kern,
        grid=(b, n_tiles),
        in_specs=in_specs,
        out_specs=tile,
        out_shape=jax.ShapeDtypeStruct((b, s, d), F32),
        scratch_shapes=[pltpu.VMEM((ts + 2 * HALO, d), BF16), pltpu.VMEM((ts, D_FF), BF16)],
        compiler_params=_params(BATCH_THEN_TILES),
        name="ffn_final" if final else "ffn",
    )(*args)


def _chunk_pos(rows):
    return (lax.broadcasted_iota(jnp.int32, (rows, 1), 0) % RET_CHUNK).astype(F32)


def _split_lanes(ref, parts):
    width = ref.shape[-1] // parts
    return [ref.at[:, :, i * width:(i + 1) * width] for i in range(parts)]


def _ret_proj_kernel(lg_ref, x_ref, g_ref, w_ref, cos_ref, sin_ref,
                     qkk_ref, v_ref, so_ref, xs_ref, kzb_ref, s_ref, *, ts):
    q_ref, k_ref, kz_ref = _split_lanes(qkk_ref, 3)
    sg_ref, ob_ref = _split_lanes(so_ref, 2)

    @pl.when(pl.program_id(1) == 0)
    def _():
        s_ref[...] = jnp.zeros_like(s_ref)

    xs_ref[...] = _rms(x_ref[0], g_ref[...]).astype(BF16)
    cos = cos_ref[...]
    sin = sin_ref[...]
    half = RET_DK // 2
    pos = _chunk_pos(ts)

    def rotary(t):
        t1 = t[:, :half]
        t2 = t[:, half:]
        return t1 * cos - t2 * sin, t2 * cos + t1 * sin

    for h in range(RET_HEADS):
        lo = h * RET_DK
        mid = lo + half
        hi = lo + RET_DK
        q1, q2 = rotary(_dot(xs_ref[...], w_ref[:, lo:hi]))
        q_ref[0, :, lo:mid] = q1.astype(BF16)
        q_ref[0, :, mid:hi] = q2.astype(BF16)
        k1, k2 = rotary(_dot(xs_ref[...], w_ref[:, RET_QK + lo:RET_QK + hi]) * (RET_DK ** -0.5))
        k_ref[0, :, lo:mid] = k1.astype(BF16)
        k_ref[0, :, mid:hi] = k2.astype(BF16)
        zeta_f = jnp.exp(lg_ref[h] * (RET_CHUNK - 1.0 - pos))
        kz_ref[0, :, lo:mid] = (k1 * zeta_f).astype(BF16)
        kz_ref[0, :, mid:hi] = (k2 * zeta_f).astype(BF16)
        zeta_b = jnp.exp(lg_ref[RET_HEADS + h] * pos)
        kzb_ref[:, lo:mid] = (k1 * zeta_b).astype(BF16)
        kzb_ref[:, mid:hi] = (k2 * zeta_b).astype(BF16)
    for h in range(RET_HEADS):
        vv = slice(h * RET_DV, (h + 1) * RET_DV)
        lo = 2 * RET_QK + h * RET_DV
        v_ref[0, :, vv] = _dot(xs_ref[...], w_ref[:, lo:lo + RET_DV]).astype(BF16)
        lo = 2 * RET_QK + RET_V + h * RET_DV
        sg_ref[0, :, vv] = jax.nn.silu(_dot(xs_ref[...], w_ref[:, lo:lo + RET_DV]))

    cpos = _chunk_pos(RET_CHUNK)
    for c in reversed(range(ts // RET_CHUNK)):
        rows = slice(c * RET_CHUNK, (c + 1) * RET_CHUNK)
        for h in range(RET_HEADS):
            lg = lg_ref[RET_HEADS + h]
            xi = jnp.exp(lg * (RET_CHUNK - cpos))
            g_chunk = jnp.exp(jnp.full((1, 1), lg * RET_CHUNK, F32))
            qk = slice(h * RET_DK, (h + 1) * RET_DK)
            vv = slice(h * RET_DV, (h + 1) * RET_DV)
            state = s_ref[h]
            ob_ref[0, rows, vv] = _dot(q_ref[0, rows, qk], state.astype(BF16)) * xi
            s_ref[h] = state * g_chunk + _dot_tn(kzb_ref[rows, qk], v_ref[0, rows, vv])


def _ret_proj(log_g, x, g, w_in, cos, sin, ts):
    b, s, d = x.shape
    n_tiles = s // ts
    tile = lambda width: pl.BlockSpec((1, ts, width), lambda bi, i: (bi, n_tiles - 1 - i, 0))
    rope = pl.BlockSpec((ts, RET_DK // 2), lambda bi, i: (n_tiles - 1 - i, 0))
    act = lambda width, dtype: jax.ShapeDtypeStruct((b, s, width), dtype)
    return pl.pallas_call(
        functools.partial(_ret_proj_kernel, ts=ts),
        grid=(b, n_tiles),
        in_specs=[pl.BlockSpec(memory_space=pltpu.SMEM), tile(d), _const_spec((1, d)), _const_spec(w_in.shape),
                  rope, rope],
        out_specs=[tile(3 * RET_QK), tile(RET_V), tile(2 * RET_V)],
        out_shape=[act(3 * RET_QK, BF16), act(RET_V, BF16), act(2 * RET_V, F32)],
        scratch_shapes=[pltpu.VMEM((ts, d), BF16), pltpu.VMEM((ts, RET_QK), BF16),
                        pltpu.VMEM((RET_HEADS, RET_DK, RET_DV), F32)],
        compiler_params=_params(BATCH_THEN_TILES),
        name="ret_proj",
    )(log_g, x, g, w_in, cos, sin)


def _ret_fwd_kernel(lg_ref, qkk_ref, v_ref, so_ref, x_ref, ng_ref, w_ref, o_ref, s_ref, z_ref, decay_ref, *, ts):
    q_ref, k_ref, kz_ref = _split_lanes(qkk_ref, 3)
    sg_ref, ob_ref = _split_lanes(so_ref, 2)

    @pl.when(pl.program_id(1) == 0)
    def _():
        s_ref[...] = jnp.zeros_like(s_ref)
        n_idx = lax.broadcasted_iota(jnp.int32, (RET_CHUNK, RET_CHUNK), 0)
        m_idx = lax.broadcasted_iota(jnp.int32, (RET_CHUNK, RET_CHUNK), 1)
        diff = (n_idx - m_idx).astype(F32)
        for h in range(RET_HEADS):
            decay_ref[h] = jnp.where(diff >= 0.0, jnp.exp(lg_ref[h] * jnp.maximum(diff, 0.0)),
                                     jnp.exp(lg_ref[RET_HEADS + h] * jnp.maximum(-diff, 0.0)))

    cpos = _chunk_pos(RET_CHUNK)
    for c in range(ts // RET_CHUNK):
        rows = slice(c * RET_CHUNK, (c + 1) * RET_CHUNK)
        for h in range(RET_HEADS):
            lf = lg_ref[h]
            decay = decay_ref[h]
            xi = jnp.exp(lf * (cpos + 1.0))
            g_chunk = jnp.exp(jnp.full((1, 1), lf * RET_CHUNK, F32))
            qk = slice(h * RET_DK, (h + 1) * RET_DK)
            vv = slice(h * RET_DV, (h + 1) * RET_DV)
            qc = q_ref[0, rows, qk]
            vc = v_ref[0, rows, vv]
            state = s_ref[h]
            scores = _dot_nt(qc, k_ref[0, rows, qk]) * decay
            y = (_dot(scores.astype(BF16), vc) + _dot(qc, state.astype(BF16)) * xi) + ob_ref[0, rows, vv]
            s_ref[h] = state * g_chunk + _dot_tn(kz_ref[0, rows, qk], vc)
            y = y * lax.rsqrt(jnp.mean(y * y, axis=-1, keepdims=True) + EPS)
            y = y * ng_ref[:, vv]
            z_ref[rows, vv] = (sg_ref[0, rows, vv] * y).astype(BF16)
        o_ref[0, rows] = x_ref[0, rows] + _dot(z_ref[rows], w_ref[...])


def _ret_fwd(log_g, qkk, v, so, x, ng, w_out, ts):
    b, s, d = x.shape
    tile = lambda width: pl.BlockSpec((1, ts, width), lambda bi, i: (bi, i, 0))
    return pl.pallas_call(
        functools.partial(_ret_fwd_kernel, ts=ts),
        grid=(b, s // ts),
        in_specs=[pl.BlockSpec(memory_space=pltpu.SMEM), tile(3 * RET_QK), tile(RET_V), tile(2 * RET_V), tile(d),
                  _const_spec((1, RET_V)), _const_spec(w_out.shape)],
        out_specs=tile(d),
        out_shape=jax.ShapeDtypeStruct((b, s, d), F32),
        scratch_shapes=[pltpu.VMEM((RET_HEADS, RET_DK, RET_DV), F32), pltpu.VMEM((ts, RET_V), BF16),
                        pltpu.VMEM((RET_HEADS, RET_CHUNK, RET_CHUNK), F32)],
        compiler_params=_params(BATCH_THEN_TILES),
        name="ret_fwd",
    )(log_g, qkk, v, so, x, ng, w_out)


def _prepare(norm_mix, norm_ffn, norm_final, lru_w_in, lru_conv_w, lru_conv_b, lru_w_a, lru_b_a,
             lru_w_x, lru_b_x, lru_lambda, lru_w_out, ret_w_in, ret_decay_logit, ret_norm, ret_w_out,
             ffn_w_in, ffn_conv_w, ffn_conv_b, ffn_w_out, seq):
    ffn = []
    for i in range(2):
        w_in = ffn_w_in[i].astype(BF16)
        ffn.append(dict(
            g=norm_ffn[i][None, :],
            wu=w_in[:, :D_FF],
            wv=w_in[:, D_FF:],
            cw=ffn_conv_w[i],
            cb=ffn_conv_b[i][None, :],
            wo=ffn_w_out[i].astype(BF16),
        ))
    half = RET_DK // 2
    theta = ROPE_BASE ** (-jnp.arange(half, dtype=F32) / half)
    ang = jnp.arange(seq, dtype=F32)[:, None] * theta[None, :]
    return dict(
        ffn=ffn,
        norm_mix=[norm_mix[0][None, :], norm_mix[1][None, :]],
        norm_final=norm_final[None, :],
        lru_w_in=lru_w_in[0].astype(BF16),
        lru_cw=0.5 * lru_conv_w[0],
        lru_cb=0.5 * lru_conv_b[0][None, :],
        lru_wa=[lru_w_a[0, d].astype(BF16) for d in range(2)],
        lru_ba=[0.5 * lru_b_a[0, d][None, :] for d in range(2)],
        lru_wx=[lru_w_x[0, d].astype(BF16) for d in range(2)],
        lru_bx=[0.5 * lru_b_x[0, d][None, :] for d in range(2)],
        lru_lam=[lru_lambda[0, d][None, :] for d in range(2)],
        lru_w_out=lru_w_out[0].astype(BF16),
        ret_w_in=ret_w_in[0].astype(BF16),
        ret_log_g=jax.nn.log_sigmoid(ret_decay_logit[0].astype(F32)).reshape(2 * RET_HEADS),
        ret_norm=ret_norm[0][None, :],
        ret_w_out=ret_w_out[0].astype(BF16),
        cos=jnp.cos(ang),
        sin=jnp.sin(ang),
    )


def _encoder(x, p, ts):
    b, s, d = x.shape
    in_ts = min(ts, LRU_IN_ROWS // b)
    ffn_ts = ts * (FFN_TILE // SEQ_TILE)
    direction = lambda di: (p["lru_wa"][di], p["lru_ba"][di], p["lru_wx"][di], p["lru_bx"][di], p["lru_lam"][di])
    gate, xc, hf = _lru_in(x, p["norm_mix"][0], p["lru_w_in"], p["lru_cw"], p["lru_cb"], *direction(0), in_ts)
    x = _lru_bwd_out(xc, *direction(1), hf, gate, x, p["lru_w_out"], min(SCAN_ROWS_BWD, ts * b))
    f = p["ffn"][0]
    x = _ffn(x, f["g"], f["wu"], f["wv"], f["cw"], f["cb"], f["wo"], ffn_ts)
    qkk, v, so = _ret_proj(p["ret_log_g"], x, p["norm_mix"][1], p["ret_w_in"], p["cos"], p["sin"], ts)
    x = _ret_fwd(p["ret_log_g"], qkk, v, so, x, p["ret_norm"], p["ret_w_out"], ts)
    f = p["ffn"][1]
    return _ffn(x, f["g"], f["wu"], f["wv"], f["cw"], f["cb"], f["wo"], ffn_ts, final_g=p["norm_final"])


def kernel(x_prompt, x_sample, norm_mix, norm_ffn, norm_final, lru_w_in, lru_conv_w, lru_conv_b, lru_w_a, lru_b_a, lru_w_x, lru_b_x, lru_lambda, lru_w_out, ret_w_in, ret_decay_logit, ret_norm, ret_w_out, ffn_w_in, ffn_conv_w, ffn_conv_b, ffn_w_out):
    assert x_prompt.shape[1] == x_sample.shape[1] and x_prompt.shape[1] % FFN_TILE == 0
    assert all(SUBLANES % x.shape[0] == 0 for x in (x_prompt, x_sample))
    p = _prepare(norm_mix, norm_ffn, norm_final, lru_w_in, lru_conv_w, lru_conv_b, lru_w_a, lru_b_a,
                 lru_w_x, lru_b_x, lru_lambda, lru_w_out, ret_w_in, ret_decay_logit, ret_norm,
                 ret_w_out, ffn_w_in, ffn_conv_w, ffn_conv_b, ffn_w_out, x_prompt.shape[1])
    return (_encoder(x_prompt, p, SEQ_TILE), _encoder(x_sample, p, SEQ_TILE))
```
